```python
import jax, jax.numpy as jnp
from jax import lax
import numpy as np

D_MODEL = 1024
BATCH = 4
SEQ = 4096
DEPTH = 4
DEC_BATCH = 128
DEC_SEQ = 1
PAST_LEN = 2048
PAGE_SIZE = 128

N_A_LAYERS = DEPTH // 2
N_B_LAYERS = DEPTH - N_A_LAYERS
CHUNK = 128
CM_EXPAND = 2
CM_WIDTH = CM_EXPAND * D_MODEL
CM_GROUPS = 8
CM_GROUP_DIM = CM_WIDTH // CM_GROUPS
FA_HEAD_DIM = 64
FA_HEADS = D_MODEL // FA_HEAD_DIM
FA_WIDTH = FA_HEADS * FA_HEAD_DIM
Q_BLOCK = 128
DEEPNORM_ALPHA = (2.0 * DEPTH) ** 0.25
DEEPNORM_BETA = (8.0 * DEPTH) ** -0.25
FORGET_BIAS_INIT = 2.0
LN_EPS = 1e-5

kernel_name = 'hybrid_gmlp_fox_yoco_step'


def layer_norm(x, g, b):
    xf = x.astype(jnp.float32)
    mu = jnp.mean(xf, axis=-1, keepdims=True)
    var = jnp.mean(jnp.square(xf - mu), axis=-1, keepdims=True)
    y = (xf - mu) * lax.rsqrt(var + LN_EPS) * g.astype(jnp.float32) + b.astype(jnp.float32)
    return y.astype(x.dtype)


def ada_modulation(c, w_ada, b_ada):
    m = (jax.nn.silu(c) @ w_ada + b_ada)[:, None, :]
    return m[..., :D_MODEL], m[..., D_MODEL:2 * D_MODEL], m[..., 2 * D_MODEL:]


def spatial_gate(vn, w_s, b_s):
    rows = vn.shape[1]
    causal = jnp.tril(jnp.ones((rows, rows), dtype=bool))
    w = jnp.where(causal[None], w_s[:, :rows, :rows], 0.0)
    return jnp.einsum('gts,nsgd->ntgd', w, vn) + b_s[:, :rows].T[None, :, :, None]


def chunk_mlp_mixer(h, w_in, b_in, ln_v_g, ln_v_b, w_s, b_s, w_out, b_out):
    n, s, _ = h.shape
    p = h @ w_in + b_in
    u = jax.nn.gelu(p[..., :CM_WIDTH])
    v = jax.nn.gelu(p[..., CM_WIDTH:2 * CM_WIDTH])
    z = p[..., 2 * CM_WIDTH:]
    vn = layer_norm(v, ln_v_g, ln_v_b)
    rows = min(s, CHUNK)
    sg = spatial_gate(vn.reshape(n * (s // rows), rows, CM_GROUPS, CM_GROUP_DIM), w_s, b_s)
    y = (u * sg.reshape(n, s, CM_WIDTH) * jax.nn.silu(z)) @ w_out + b_out
    return y, vn


def shared_kv(xs, kv_w, kv_b):
    n, s, _ = xs.shape
    p = xs @ kv_w + kv_b
    k = p[..., :FA_WIDTH].reshape(n, s, FA_HEADS, FA_HEAD_DIM)
    v = p[..., FA_WIDTH:2 * FA_WIDTH].reshape(n, s, FA_HEADS, FA_HEAD_DIM)
    logf = jax.nn.log_sigmoid(p[..., 2 * FA_WIDTH:].astype(jnp.float32))
    return k, v, logf


def forget_attend(q, c_q, q_pos, k, v, c_k, k_pos):
    logits = jnp.einsum('nqhd,nkhd->nhqk', q, k).astype(jnp.float32) * (FA_HEAD_DIM ** -0.5)
    decay = jnp.swapaxes(c_q, 1, 2)[..., :, None] - jnp.swapaxes(c_k, 1, 2)[..., None, :]
    causal = k_pos[None, :] <= q_pos[:, None]
    logits = jnp.where(causal, logits + decay, -jnp.inf)
    p = jax.nn.softmax(logits, axis=-1)
    return jnp.einsum('nhqk,nkhd->nqhd', p.astype(v.dtype), v)


def forgetting_attention_mixer(h, k, v, c_k, c_q, q_pos, k_pos, w_in, b_in, w_out, b_out):
    n, s, _ = h.shape
    p = h @ w_in + b_in
    q = p[..., :FA_WIDTH].reshape(n, s, FA_HEADS, FA_HEAD_DIM)
    z = p[..., FA_WIDTH:]
    blk = Q_BLOCK if s % Q_BLOCK == 0 else s
    nb = s // blk
    qb = jnp.swapaxes(q.reshape(n, nb, blk, FA_HEADS, FA_HEAD_DIM), 0, 1)
    cb = jnp.swapaxes(c_q.reshape(n, nb, blk, FA_HEADS), 0, 1)
    pb = q_pos.reshape(nb, blk)
    o = lax.map(lambda a: forget_attend(a[0], a[1], a[2], k, v, c_k, k_pos), (qb, cb, pb))
    o = jnp.swapaxes(o, 0, 1).reshape(n, s, FA_WIDTH)
    return (o * jax.nn.silu(z)) @ w_out + b_out


def setup_inputs(seed: int = 0) -> dict:
    key = jax.random.key(seed)
    ks = jax.random.split(key, 32)
    f32 = jnp.float32

    def nrm(k, shape, scale):
        return scale * jax.random.normal(k, shape, f32)

    n_pages = PAST_LEN // PAGE_SIZE
    n_used = DEC_BATCH * n_pages
    n_phys = n_used + n_used // 4
    x_prompt = nrm(ks[0], (BATCH, SEQ, D_MODEL), 1.0)
    x_sample = nrm(ks[1], (DEC_BATCH, DEC_SEQ, D_MODEL), 1.0)
    c_prompt = nrm(ks[2], (BATCH, D_MODEL), 1.0)
    c_sample = nrm(ks[3], (DEC_BATCH, D_MODEL), 1.0)
    cache_k = nrm(ks[4], (n_phys, PAGE_SIZE, FA_HEADS, FA_HEAD_DIM), 1.0)
    cache_v = nrm(ks[5], (n_phys, PAGE_SIZE, FA_HEADS, FA_HEAD_DIM), 1.0)
    cache_logf = jax.nn.log_sigmoid(FORGET_BIAS_INIT + jax.random.normal(ks[6], (n_phys, PAGE_SIZE, FA_HEADS), f32))
    page_table = jax.random.permutation(ks[7], n_phys)[:n_used].reshape(DEC_BATCH, n_pages).astype(jnp.int32)
    w_ada = nrm(ks[8], (DEPTH, D_MODEL, 3 * D_MODEL), 0.1 * D_MODEL ** -0.5)
    b_ada = nrm(ks[9], (DEPTH, 3 * D_MODEL), 0.02)
    ln_g = 1.0 + nrm(ks[10], (DEPTH, D_MODEL), 0.02)
    ln_b = nrm(ks[11], (DEPTH, D_MODEL), 0.02)
    cm_w_in = nrm(ks[12], (N_A_LAYERS, D_MODEL, 3 * CM_WIDTH), D_MODEL ** -0.5)
    cm_b_in = nrm(ks[13], (N_A_LAYERS, 3 * CM_WIDTH), 0.02)
    cm_ln_v_g = 1.0 + nrm(ks[14], (N_A_LAYERS, CM_WIDTH), 0.02)
    cm_ln_v_b = nrm(ks[15], (N_A_LAYERS, CM_WIDTH), 0.02)
    cm_w_s = nrm(ks[16], (N_A_LAYERS, CM_GROUPS, CHUNK, CHUNK), CHUNK ** -0.5)
    cm_b_s = 1.0 + nrm(ks[17], (N_A_LAYERS, CM_GROUPS, CHUNK), 0.02)
    cm_w_out = nrm(ks[18], (N_A_LAYERS, CM_WIDTH, D_MODEL), DEEPNORM_BETA * CM_WIDTH ** -0.5)
    cm_b_out = nrm(ks[19], (N_A_LAYERS, D_MODEL), 0.02)
    kv_w = nrm(ks[20], (D_MODEL, 2 * FA_WIDTH + FA_HEADS), D_MODEL ** -0.5)
    kv_b = nrm(ks[21], (2 * FA_WIDTH + FA_HEADS,), 0.02).at[2 * FA_WIDTH:].add(FORGET_BIAS_INIT)
    fa_w_in = nrm(ks[22], (N_B_LAYERS, D_MODEL, 2 * FA_WIDTH), D_MODEL ** -0.5)
    fa_b_in = nrm(ks[23], (N_B_LAYERS, 2 * FA_WIDTH), 0.02)
    fa_w_out = nrm(ks[24], (N_B_LAYERS, FA_WIDTH, D_MODEL), DEEPNORM_BETA * FA_WIDTH ** -0.5)
    fa_b_out = nrm(ks[25], (N_B_LAYERS, D_MODEL), 0.02)
    return {'x_prompt': x_prompt, 'x_sample': x_sample, 'c_prompt': c_prompt, 'c_sample': c_sample,
            'cache_k': cache_k, 'cache_v': cache_v, 'cache_logf': cache_logf, 'page_table': page_table,
            'w_ada': w_ada, 'b_ada': b_ada, 'ln_g': ln_g, 'ln_b': ln_b,
            'cm_w_in': cm_w_in, 'cm_b_in': cm_b_in, 'cm_ln_v_g': cm_ln_v_g, 'cm_ln_v_b': cm_ln_v_b,
            'cm_w_s': cm_w_s, 'cm_b_s': cm_b_s, 'cm_w_out': cm_w_out, 'cm_b_out': cm_b_out,
            'kv_w': kv_w, 'kv_b': kv_b,
            'fa_w_in': fa_w_in, 'fa_b_in': fa_b_in, 'fa_w_out': fa_w_out, 'fa_b_out': fa_b_out}


def reference(x_prompt, x_sample, c_prompt, c_sample, cache_k, cache_v, cache_logf, page_table,
              w_ada, b_ada, ln_g, ln_b,
              cm_w_in, cm_b_in, cm_ln_v_g, cm_ln_v_b, cm_w_s, cm_b_s, cm_w_out, cm_b_out,
              kv_w, kv_b, fa_w_in, fa_b_in, fa_w_out, fa_b_out):

    def run_trunk(x, c, past):
        n, s, _ = x.shape
        pos0 = 0 if past is None else past[0].shape[1]
        q_pos = pos0 + jnp.arange(s, dtype=jnp.int32)
        k_pos = jnp.arange(pos0 + s, dtype=jnp.int32)
        chunk_rows = []
        new_rows = None
        shared = None
        for layer in range(DEPTH):
            if layer == N_A_LAYERS:
                k_new, v_new, logf_new = shared_kv(x, kv_w, kv_b)
                new_rows = (k_new, v_new, logf_new)
                if past is None:
                    k_all, v_all, logf_all = k_new, v_new, logf_new
                else:
                    k_all = jnp.concatenate([past[0], k_new], axis=1)
                    v_all = jnp.concatenate([past[1], v_new], axis=1)
                    logf_all = jnp.concatenate([past[2].astype(jnp.float32), logf_new], axis=1)
                shared = (k_all, v_all, jnp.cumsum(logf_all, axis=1))
            shift, scale, gate = ada_modulation(c, w_ada[layer], b_ada[layer])
            h = x * (1.0 + scale) + shift
            if layer < N_A_LAYERS:
                f, vn = chunk_mlp_mixer(h, cm_w_in[layer], cm_b_in[layer], cm_ln_v_g[layer], cm_ln_v_b[layer],
                                        cm_w_s[layer], cm_b_s[layer], cm_w_out[layer], cm_b_out[layer])
                chunk_rows.append(vn)
            else:
                j = layer - N_A_LAYERS
                k_all, v_all, c_all = shared
                f = forgetting_attention_mixer(h, k_all, v_all, c_all, c_all[:, pos0:], q_pos, k_pos,
                                               fa_w_in[j], fa_b_in[j], fa_w_out[j], fa_b_out[j])
            x = layer_norm(DEEPNORM_ALPHA * x + (1.0 + gate) * f, ln_g[layer], ln_b[layer])
        return x, new_rows, chunk_rows

    y_prompt, rows_p, _ = run_trunk(x_prompt, c_prompt, None)
    k_prompt, v_prompt, logf_prompt = rows_p

    db = page_table.shape[0]
    k_past = cache_k[page_table].reshape(db, -1, FA_HEADS, FA_HEAD_DIM)
    v_past = cache_v[page_table].reshape(db, -1, FA_HEADS, FA_HEAD_DIM)
    logf_past = cache_logf[page_table].reshape(db, -1, FA_HEADS)
    y_sample, rows_s, chunk_rows_s = run_trunk(x_sample, c_sample, (k_past, v_past, logf_past))
    k_sample, v_sample, logf_sample = rows_s
    chunk_v_sample = jnp.stack(chunk_rows_s)
    return (y_prompt, y_sample, k_prompt, v_prompt, logf_prompt, k_sample, v_sample, logf_sample, chunk_v_sample)
```

```python
import functools

import jax
import jax.numpy as jnp
from jax import lax
from jax.experimental import pallas as pl
from jax.experimental.pallas import tpu as pltpu

F32 = jnp.float32
BF16 = jnp.bfloat16

D_MODEL = 1024
DEPTH = 4
N_A_LAYERS = 2
CHUNK = 128
CM_WIDTH = 2048
CM_GROUPS = 8
CM_GROUP_DIM = CM_WIDTH // CM_GROUPS
FA_HEADS = 16
FA_HEAD_DIM = 64
FA_WIDTH = FA_HEADS * FA_HEAD_DIM
HEAD_PAIRS = FA_HEADS // 2
PAGE_SIZE = 128
DEEPNORM_ALPHA = (2.0 * DEPTH) ** 0.25
LN_EPS = 1e-5
LANES = 128
NEG_BIG = -1e30
VMEM_LIMIT = 56 * 1024 * 1024

ROW_TILE = 256
ATT_TILE = 512


def _params(n_axes, vmem=VMEM_LIMIT):
    return pltpu.CompilerParams(dimension_semantics=("arbitrary",) * n_axes, vmem_limit_bytes=vmem)


def _dot(a, b):
    return jnp.dot(a, b, preferred_element_type=F32)


def _dot_nt(a, b):
    return lax.dot_general(a, b, (((1,), (1,)), ((), ())), preferred_element_type=F32)


def _layer_norm(x, g, b):
    mu = jnp.mean(x, axis=-1, keepdims=True)
    d = x - mu
    var = jnp.mean(d * d, axis=-1, keepdims=True)
    return d * lax.rsqrt(var + LN_EPS) * g + b


def _log_sigmoid(x):
    return -(jnp.maximum(-x, 0.0) + jnp.log1p(jnp.exp(-jnp.abs(x))))


def _split3_dot(x, w):
    hi = x.astype(BF16)
    r1 = x - hi.astype(F32)
    mid = r1.astype(BF16)
    lo = (r1 - mid.astype(F32)).astype(BF16)
    return _dot(hi, w) + _dot(mid, w) + _dot(lo, w)


def _const_spec(shape):
    nd = len(shape)
    return pl.BlockSpec(shape, lambda *_: (0,) * nd)


def _ada_kernel(c_ref, w_ref, b_ref, o_ref):
    a = jax.nn.silu(c_ref[...]).astype(BF16)
    o_ref[0] = _dot(a, w_ref[0].astype(BF16)) + b_ref[0]


def _ada_modulation(c_all, w_ada, b_ada):
    n = c_all.shape[0]
    nt = 3 * D_MODEL // D_MODEL
    return pl.pallas_call(
        _ada_kernel,
        grid=(DEPTH, nt),
        in_specs=[pl.BlockSpec((n, D_MODEL), lambda l, j: (0, 0)),
                  pl.BlockSpec((1, D_MODEL, D_MODEL), lambda l, j: (l, 0, j)),
                  pl.BlockSpec((1, 1, D_MODEL), lambda l, j: (l, 0, j))],
        out_specs=pl.BlockSpec((1, n, D_MODEL), lambda l, j: (l, 0, j)),
        out_shape=jax.ShapeDtypeStruct((DEPTH, n, 3 * D_MODEL), F32),
        compiler_params=_params(2),
        name="ada_modulation",
    )(c_all, w_ada, b_ada.reshape(DEPTH, 1, 3 * D_MODEL))


def _chunk_mlp_kernel(x_ref, mod_ref, win_ref, bin_ref, lvg_ref, lvb_ref, ws_ref, bs_ref,
                      wout_ref, bout_ref, lng_ref, lnb_ref, *rest, single_row_chunks):
    if single_row_chunks:
        o_ref, vn_ref, t_scr = rest
    else:
        o_ref, t_scr = rest
    rows = x_ref.shape[0]
    x = x_ref[...]
    m = mod_ref[0]
    shift, scale, gate = m[:, :D_MODEL], m[:, D_MODEL:2 * D_MODEL], m[:, 2 * D_MODEL:]
    h = (x * (1.0 + scale) + shift).astype(BF16)

    v = jax.nn.gelu(_dot(h, win_ref[:, CM_WIDTH:2 * CM_WIDTH]) + bin_ref[:, CM_WIDTH:2 * CM_WIDTH])
    vn = _layer_norm(v, lvg_ref[...], lvb_ref[...])
    if single_row_chunks:
        vn_ref[...] = vn
    else:
        r_i = lax.broadcasted_iota(jnp.int32, (CHUNK, CHUNK), 0)
        c_i = lax.broadcasted_iota(jnp.int32, (CHUNK, CHUNK), 1)
        causal = c_i <= r_i

    for g in range(CM_GROUPS):
        lo, hi = g * CM_GROUP_DIM, (g + 1) * CM_GROUP_DIM
        vn_g = vn[:, lo:hi]
        if single_row_chunks:
            sg = vn_g * ws_ref[:, lo:hi] + bs_ref[:, lo:hi]
        else:
            w_g = jnp.where(causal, ws_ref[g], 0.0).astype(BF16)
            b_g = bs_ref[:, g:g + 1]
            vb = vn_g.astype(BF16)
            sg = jnp.concatenate(
                [_dot(w_g, vb[c * CHUNK:(c + 1) * CHUNK]) + b_g for c in range(rows // CHUNK)], axis=0)
        u_g = jax.nn.gelu(_dot(h, win_ref[:, lo:hi]) + bin_ref[:, lo:hi])
        z_g = _dot(h, win_ref[:, 2 * CM_WIDTH + lo:2 * CM_WIDTH + hi]) + bin_ref[:, 2 * CM_WIDTH + lo:2 * CM_WIDTH + hi]
        t_scr[:, lo:hi] = (u_g * sg * jax.nn.silu(z_g)).astype(BF16)

    y = _dot(t_scr[...], wout_ref[...]) + bout_ref[...]
    r = DEEPNORM_ALPHA * x + (1.0 + gate) * y
    o_ref[...] = _layer_norm(r, lng_ref[...], lnb_ref[...])


def _chunk_mlp_layer(x, mod, w_in, b_in, lvg, lvb, w_s, b_s, w_out, b_out, ln_g, ln_b, *, single_row_chunks):
    rows = x.shape[0]
    nb, mrows = mod.shape[0], mod.shape[1]
    tm = rows if single_row_chunks else ROW_TILE
    tiles_per_batch = rows // nb // tm
    if single_row_chunks:
        ws_arg = jnp.repeat(w_s[:, 0, 0], CM_GROUP_DIM)[None]
        bs_arg = jnp.repeat(b_s[:, 0], CM_GROUP_DIM)[None]
    else:
        ws_arg, bs_arg = w_s, b_s.T
    out_shape = [jax.ShapeDtypeStruct((rows, D_MODEL), F32)]
    out_specs = [pl.BlockSpec((tm, D_MODEL), lambda i: (i, 0))]
    if single_row_chunks:
        out_shape.append(jax.ShapeDtypeStruct((rows, CM_WIDTH), F32))
        out_specs.append(pl.BlockSpec((tm, CM_WIDTH), lambda i: (i, 0)))
    res = pl.pallas_call(
        functools.partial(_chunk_mlp_kernel, single_row_chunks=single_row_chunks),
        grid=(rows // tm,),
        in_specs=[pl.BlockSpec((tm, D_MODEL), lambda i: (i, 0)),
                  pl.BlockSpec((1, mrows, 3 * D_MODEL), lambda i: (i // tiles_per_batch, 0, 0)),
                  _const_spec((D_MODEL, 3 * CM_WIDTH)), _const_spec((1, 3 * CM_WIDTH)),
                  _const_spec((1, CM_WIDTH)), _const_spec((1, CM_WIDTH)),
                  _const_spec(ws_arg.shape), _const_spec(bs_arg.shape),
                  _const_spec((CM_WIDTH, D_MODEL)), _const_spec((1, D_MODEL)),
                  _const_spec((1, D_MODEL)), _const_spec((1, D_MODEL))],
        out_specs=out_specs,
        out_shape=out_shape,
        scratch_shapes=[pltpu.VMEM((tm, CM_WIDTH), BF16)],
        compiler_params=_params(1),
        name="chunk_mlp_sample" if single_row_chunks else "chunk_mlp_prompt",
    )(x, mod, w_in.astype(BF16), b_in[None], lvg[None], lvb[None], ws_arg, bs_arg,
      w_out.astype(BF16), b_out[None], ln_g[None], ln_b[None])
    return res


def _kv_kernel(x_ref, wkv_ref, bkv_ref, wf_ref, bf_ref, *rest, prompt):
    if prompt:
        wft_ref, bft_ref, k_ref, v_ref, lf_ref, kb_ref, vb_ref, ct_ref, carry_scr = rest
    else:
        k_ref, v_ref, lf_ref = rest
    rows = x_ref.shape[0]
    xb = x_ref[...].astype(BF16)
    kv = _dot(xb, wkv_ref[...]) + bkv_ref[...]
    k, v = kv[:, :FA_WIDTH], kv[:, FA_WIDTH:]
    k_ref[...] = k
    v_ref[...] = v
    lf = _log_sigmoid(_dot(xb, wf_ref[...]) + bf_ref[...])
    lf_ref[...] = lf[:, :FA_HEADS]
    if prompt:
        kb, vb = k.astype(BF16), v.astype(BF16)
        for j in range(HEAD_PAIRS):
            kb_ref[0, j] = kb[:, j * LANES:(j + 1) * LANES]
            vb_ref[0, j] = vb[:, j * LANES:(j + 1) * LANES]
        lft = _log_sigmoid(_dot_nt(wft_ref[...], xb) + bft_ref[...])
        r_i = lax.broadcasted_iota(jnp.int32, (rows, rows), 0)
        c_i = lax.broadcasted_iota(jnp.int32, (rows, rows), 1)
        upper = jnp.where(r_i <= c_i, 1.0, 0.0).astype(BF16)

        @pl.when(pl.program_id(1) == 0)
        def _():
            carry_scr[...] = jnp.zeros_like(carry_scr)

        ct = _split3_dot(lft, upper) + carry_scr[:, 0:1]
        carry_scr[...] = jnp.broadcast_to(ct[:, rows - 1:rows], carry_scr.shape)
        for j in range(HEAD_PAIRS):
            ct_ref[0, j, 0] = ct[2 * j:2 * j + 2, :]


def _kv_weights(kv_w, kv_b):
    wkv = kv_w[:, :2 * FA_WIDTH].astype(BF16)
    bkv = kv_b[None, :2 * FA_WIDTH]
    wf = jnp.pad(kv_w[:, 2 * FA_WIDTH:], ((0, 0), (0, LANES - FA_HEADS))).astype(BF16)
    bf = jnp.pad(kv_b[2 * FA_WIDTH:], (0, LANES - FA_HEADS))[None]
    return wkv, bkv, wf, bf


def _kv_prompt(x, kv_w, kv_b, batch, seq):
    tm = ATT_TILE
    nt = seq // tm
    wkv, bkv, wf, bf = _kv_weights(kv_w, kv_b)
    wft = kv_w[:, 2 * FA_WIDTH:].T.astype(BF16)
    bft = kv_b[2 * FA_WIDTH:][:, None]
    row = lambda b, i: (b * nt + i, 0)
    return pl.pallas_call(
        functools.partial(_kv_kernel, prompt=True),
        grid=(batch, nt),
        in_specs=[pl.BlockSpec((tm, D_MODEL), row),
                  _const_spec(wkv.shape), _const_spec(bkv.shape), _const_spec(wf.shape), _const_spec(bf.shape),
                  _const_spec(wft.shape), _const_spec(bft.shape)],
        out_specs=[pl.BlockSpec((tm, FA_WIDTH), row), pl.BlockSpec((tm, FA_WIDTH), row),
                   pl.BlockSpec((tm, FA_HEADS), row),
                   pl.BlockSpec((1, HEAD_PAIRS, tm, LANES), lambda b, i: (b, 0, i, 0)),
                   pl.BlockSpec((1, HEAD_PAIRS, tm, LANES), lambda b, i: (b, 0, i, 0)),
                   pl.BlockSpec((1, HEAD_PAIRS, 1, 2, tm), lambda b, i: (b, 0, i, 0, 0))],
        out_shape=[jax.ShapeDtypeStruct((batch * seq, FA_WIDTH), F32),
                   jax.ShapeDtypeStruct((batch * seq, FA_WIDTH), F32),
                   jax.ShapeDtypeStruct((batch * seq, FA_HEADS), F32),
                   jax.ShapeDtypeStruct((batch, HEAD_PAIRS, seq, LANES), BF16),
                   jax.ShapeDtypeStruct((batch, HEAD_PAIRS, seq, LANES), BF16),
                   jax.ShapeDtypeStruct((batch, HEAD_PAIRS, nt, 2, tm), F32)],
        scratch_shapes=[pltpu.VMEM((FA_HEADS, LANES), F32)],
        compiler_params=_params(2),
        name="kv_prompt",
    )(x, wkv, bkv, wf, bf, wft, bft)


def _kv_sample(x, kv_w, kv_b):
    rows = x.shape[0]
    wkv, bkv, wf, bf = _kv_weights(kv_w, kv_b)
    full = lambda i: (0, 0)
    return pl.pallas_call(
        functools.partial(_kv_kernel, prompt=False),
        grid=(1,),
        in_specs=[pl.BlockSpec((rows, D_MODEL), full),
                  _const_spec(wkv.shape), _const_spec(bkv.shape), _const_spec(wf.shape), _const_spec(bf.shape)],
        out_specs=[pl.BlockSpec((rows, FA_WIDTH), full), pl.BlockSpec((rows, FA_WIDTH), full),
                   pl.BlockSpec((rows, FA_HEADS), full)],
        out_shape=[jax.ShapeDtypeStruct((rows, FA_WIDTH), F32), jax.ShapeDtypeStruct((rows, FA_WIDTH), F32),
                   jax.ShapeDtypeStruct((rows, FA_HEADS), F32)],
        compiler_params=_params(1),
        name="kv_sample",
    )(x, wkv, bkv, wf, bf)


def _q_proj_kernel(x_ref, mod_ref, w_ref, b_ref, q_ref, gz_ref, *, pair_major):
    x = x_ref[...]
    m = mod_ref[0]
    h = (x * (1.0 + m[:, D_MODEL:2 * D_MODEL]) + m[:, :D_MODEL]).astype(BF16)
    p = _dot(h, w_ref[...]) + b_ref[...]
    q = (p[:, :FA_WIDTH] * (FA_HEAD_DIM ** -0.5)).astype(BF16)
    if pair_major:
        for j in range(HEAD_PAIRS):
            q_ref[0, j] = q[:, j * LANES:(j + 1) * LANES]
    else:
        q_ref[...] = q
    gz_ref[...] = jax.nn.silu(p[:, FA_WIDTH:])


def _q_proj(x, mod, w_in, b_in, *, batch=None, seq=None):
    rows = x.shape[0]
    nb, mrows = mod.shape[0], mod.shape[1]
    pair_major = batch is not None
    tm = ROW_TILE if pair_major else rows
    tpb = rows // nb // tm
    if pair_major:
        q_spec = pl.BlockSpec((1, HEAD_PAIRS, tm, LANES), lambda i: (i // tpb, 0, i % tpb, 0))
        q_shape = jax.ShapeDtypeStruct((batch, HEAD_PAIRS, seq, LANES), BF16)
    else:
        q_spec = pl.BlockSpec((tm, FA_WIDTH), lambda i: (i, 0))
        q_shape = jax.ShapeDtypeStruct((rows, FA_WIDTH), BF16)
    return pl.pallas_call(
        functools.partial(_q_proj_kernel, pair_major=pair_major),
        grid=(rows // tm,),
        in_specs=[pl.BlockSpec((tm, D_MODEL), lambda i: (i, 0)),
                  pl.BlockSpec((1, mrows, 3 * D_MODEL), lambda i: (i // tpb, 0, 0)),
                  _const_spec((D_MODEL, 2 * FA_WIDTH)), _const_spec((1, 2 * FA_WIDTH))],
        out_specs=[q_spec, pl.BlockSpec((tm, FA_WIDTH), lambda i: (i, 0))],
        out_shape=[q_shape, jax.ShapeDtypeStruct((rows, FA_WIDTH), F32)],
        compiler_params=_params(1),
        name="q_proj_prompt" if pair_major else "q_proj_sample",
    )(x, mod, w_in.astype(BF16), b_in[None])


def _out_proj_kernel(o_ref, gz_ref, x_ref, mod_ref, w_ref, b_ref, g_ref, bb_ref, y_ref):
    t = (o_ref[...] * gz_ref[...]).astype(BF16)
    y = _dot(t, w_ref[...]) + b_ref[...]
    gate = mod_ref[0][:, 2 * D_MODEL:]
    r = DEEPNORM_ALPHA * x_ref[...] + (1.0 + gate) * y
    y_ref[...] = _layer_norm(r, g_ref[...], bb_ref[...])


def _out_proj(o, gz, x, mod, w_out, b_out, ln_g, ln_b, *, tm):
    rows = x.shape[0]
    nb, mrows = mod.shape[0], mod.shape[1]
    tpb = rows // nb // tm
    row = lambda i: (i, 0)
    return pl.pallas_call(
        _out_proj_kernel,
        grid=(rows // tm,),
        in_specs=[pl.BlockSpec((tm, FA_WIDTH), row), pl.BlockSpec((tm, FA_WIDTH), row),
                  pl.BlockSpec((tm, D_MODEL), row),
                  pl.BlockSpec((1, mrows, 3 * D_MODEL), lambda i: (i // tpb, 0, 0)),
                  _const_spec((FA_WIDTH, D_MODEL)), _const_spec((1, D_MODEL)),
                  _const_spec((1, D_MODEL)), _const_spec((1, D_MODEL))],
        out_specs=pl.BlockSpec((tm, D_MODEL), row),
        out_shape=jax.ShapeDtypeStruct((rows, D_MODEL), F32),
        compiler_params=_params(1),
        name="out_proj",
    )(o, gz, x, mod, w_out.astype(BF16), b_out[None], ln_g[None], ln_b[None])


def _flash_kernel(q_ref, k_ref, v_ref, c_ref, o_ref, *, tile, n_tiles):
    lane = lax.broadcasted_iota(jnp.int32, (1, LANES), 1)
    first = lane < FA_HEAD_DIM
    r_i = lax.broadcasted_iota(jnp.int32, (tile, tile), 0)
    c_i = lax.broadcasted_iota(jnp.int32, (tile, tile), 1)
    causal = c_i <= r_i

    def q_body(i, _):
        q2 = q_ref[0, 0, pl.ds(pl.multiple_of(i * tile, tile), tile), :]
        zero = jnp.zeros_like(q2)
        q_heads = (jnp.where(first, q2, zero), jnp.where(first, zero, q2))

        def kv_step(t, carry, masked):
            start = pl.multiple_of(t * tile, tile)
            k2 = k_ref[0, 0, pl.ds(start, tile), :]
            v2 = v_ref[0, 0, pl.ds(start, tile), :]
            c2 = c_ref[0, 0, t]
            new = []
            for a in range(2):
                m, l, acc = carry[a]
                s = _dot_nt(q_heads[a], k2) - c2[a:a + 1, :]
                if masked:
                    s = jnp.where(causal, s, NEG_BIG)
                m_new = jnp.maximum(m, jnp.max(s, axis=-1, keepdims=True))
                alpha = jnp.exp(m - m_new)
                p = jnp.exp(s - m_new)
                l_new = alpha * l + jnp.sum(p, axis=-1, keepdims=True)
                acc_new = alpha * acc + _dot(p.astype(BF16), v2)
                new.append((m_new, l_new, acc_new))
            return tuple(new)

        init = tuple((jnp.full((tile, 1), NEG_BIG, F32), jnp.zeros((tile, 1), F32),
                      jnp.zeros((tile, LANES), F32)) for _ in range(2))
        carry = lax.fori_loop(0, i, lambda t, c: kv_step(t, c, False), init)
        (_, l_a, acc_a), (_, l_b, acc_b) = kv_step(i, carry, True)
        o = jnp.where(first, acc_a / l_a, acc_b / l_b)
        o_ref[0, pl.ds(pl.multiple_of(i * tile, tile), tile), :] = o
        return 0

    lax.fori_loop(0, n_tiles, q_body, 0)


def _flash_attention(q_pm, k_pm, v_pm, c_t, batch, seq):
    tile = ATT_TILE
    nt = seq // tile
    pm = lambda b, j: (b, j, 0, 0)
    return pl.pallas_call(
        functools.partial(_flash_kernel, tile=tile, n_tiles=nt),
        grid=(batch, HEAD_PAIRS),
        in_specs=[pl.BlockSpec((1, 1, seq, LANES), pm), pl.BlockSpec((1, 1, seq, LANES), pm),
                  pl.BlockSpec((1, 1, seq, LANES), pm),
                  pl.BlockSpec((1, 1, nt, 2, tile), lambda b, j: (b, j, 0, 0, 0))],
        out_specs=pl.BlockSpec((1, seq, LANES), lambda b, j: (b, 0, j)),
        out_shape=jax.ShapeDtypeStruct((batch, seq, FA_WIDTH), F32),
        compiler_params=_params(2),
        name="flash_prompt",
    )(q_pm, k_pm, v_pm, c_t)


def _decode_kernel(pt_ref, q_ref, kn_ref, vn_ref, lfn_ref, *refs, n_pages):
    del pt_ref
    k_refs, v_refs, lf_refs = refs[:n_pages], refs[n_pages:2 * n_pages], refs[2 * n_pages:3 * n_pages]
    o_ref = refs[3 * n_pages]
    head = lax.broadcasted_iota(jnp.int32, (FA_HEADS, FA_WIDTH), 0)
    col = lax.broadcasted_iota(jnp.int32, (FA_HEADS, FA_WIDTH), 1)
    own = (col >= head * FA_HEAD_DIM) & (col < (head + 1) * FA_HEAD_DIM)
    q_rows = jnp.where(own, jnp.broadcast_to(q_ref[0].astype(F32), (FA_HEADS, FA_WIDTH)), 0.0)
    q_bf = q_rows.astype(BF16)

    r_i = lax.broadcasted_iota(jnp.int32, (PAGE_SIZE, PAGE_SIZE), 0)
    c_i = lax.broadcasted_iota(jnp.int32, (PAGE_SIZE, PAGE_SIZE), 1)
    upper = jnp.where(r_i <= c_i, 1.0, 0.0).astype(BF16)
    lf_all = jnp.concatenate([lf_refs[i][0] for i in range(n_pages)], axis=0)
    cum_all = _split3_dot(lf_all, upper)
    cums, offs = [], []
    off = jnp.zeros((FA_HEADS, 1), F32)
    for i in range(n_pages):
        cum_i = cum_all[i * FA_HEADS:(i + 1) * FA_HEADS]
        cums.append(cum_i)
        offs.append(off)
        off = off + cum_i[:, PAGE_SIZE - 1:PAGE_SIZE]
    c_query = off + lfn_ref[0]

    s_parts = []
    for i in range(n_pages):
        s_i = _dot_nt(q_bf, k_refs[i][0].astype(BF16))
        s_parts.append(s_i + (c_query - (offs[i] + cums[i])))
    s = jnp.concatenate(s_parts, axis=1)
    s_new = jnp.sum(q_rows * kn_ref[0], axis=1, keepdims=True)
    m = jnp.maximum(jnp.max(s, axis=1, keepdims=True), s_new)
    p = jnp.exp(s - m)
    p_new = jnp.exp(s_new - m)
    l = jnp.sum(p, axis=1, keepdims=True) + p_new
    pb = p.astype(BF16)
    acc = p_new * vn_ref[0]
    for i in range(n_pages):
        acc = acc + _dot(pb[:, i * PAGE_SIZE:(i + 1) * PAGE_SIZE], v_refs[i][0].astype(BF16))
    acc = acc / l
    o_ref[0] = jnp.sum(jnp.where(own, acc, 0.0), axis=0, keepdims=True)


def _decode_attention(q, k_new, v_new, lf_new, cache_k, cache_v, cache_lft, page_table):
    n, n_pages = page_table.shape

    def page_map(i):
        return lambda b, pt: (pt[b, i], 0, 0)

    row = lambda b, pt: (b, 0, 0)
    in_specs = [pl.BlockSpec((1, 1, FA_WIDTH), row), pl.BlockSpec((1, 1, FA_WIDTH), row),
                pl.BlockSpec((1, 1, FA_WIDTH), row), pl.BlockSpec((1, FA_HEADS, 1), row)]
    in_specs += [pl.BlockSpec((1, PAGE_SIZE, FA_WIDTH), page_map(i)) for i in range(n_pages)]
    in_specs += [pl.BlockSpec((1, PAGE_SIZE, FA_WIDTH), page_map(i)) for i in range(n_pages)]
    in_specs += [pl.BlockSpec((1, FA_HEADS, PAGE_SIZE), page_map(i)) for i in range(n_pages)]
    out = pl.pallas_call(
        functools.partial(_decode_kernel, n_pages=n_pages),
        grid_spec=pltpu.PrefetchScalarGridSpec(
            num_scalar_prefetch=1, grid=(n,), in_specs=in_specs,
            out_specs=pl.BlockSpec((1, 1, FA_WIDTH), row)),
        out_shape=jax.ShapeDtypeStruct((n, 1, FA_WIDTH), F32),
        compiler_params=_params(1),
        name="decode_attention",
    )(page_table, q.reshape(n, 1, FA_WIDTH), k_new.reshape(n, 1, FA_WIDTH), v_new.reshape(n, 1, FA_WIDTH),
      lf_new.reshape(n, FA_HEADS, 1), *([cache_k] * n_pages), *([cache_v] * n_pages), *([cache_lft] * n_pages))
    return out.reshape(n, FA_WIDTH)


def kernel(x_prompt, x_sample, c_prompt, c_sample, cache_k, cache_v, cache_logf, page_table, w_ada, b_ada, ln_g, ln_b, cm_w_in, cm_b_in, cm_ln_v_g, cm_ln_v_b, cm_w_s, cm_b_s, cm_w_out, cm_b_out, kv_w, kv_b, fa_w_in, fa_b_in, fa_w_out, fa_b_out):
    batch, seq, _ = x_prompt.shape
    n_dec = x_sample.shape[0]
    n_phys = cache_k.shape[0]

    n_c = batch + n_dec
    c_all = jnp.pad(jnp.concatenate([c_prompt, c_sample], axis=0), ((0, (-n_c) % 8), (0, 0)))
    mod = _ada_modulation(c_all, w_ada, b_ada)
    mod_p = mod[:, :batch].reshape(DEPTH, batch, 1, 3 * D_MODEL)
    mod_s = mod[:, batch:n_c].reshape(DEPTH, 1, n_dec, 3 * D_MODEL)

    xp = x_prompt.reshape(batch * seq, D_MODEL)
    xs = x_sample.reshape(n_dec, D_MODEL)
    chunk_rows = []
    for layer in range(N_A_LAYERS):
        args = (cm_w_in[layer], cm_b_in[layer], cm_ln_v_g[layer], cm_ln_v_b[layer], cm_w_s[layer], cm_b_s[layer],
                cm_w_out[layer], cm_b_out[layer], ln_g[layer], ln_b[layer])
        (xp,) = _chunk_mlp_layer(xp, mod_p[layer], *args, single_row_chunks=False)
        xs, vn_s = _chunk_mlp_layer(xs, mod_s[layer], *args, single_row_chunks=True)
        chunk_rows.append(vn_s)

    k_p, v_p, lf_p, kb_p, vb_p, ct_p = _kv_prompt(xp, kv_w, kv_b, batch, seq)
    k_s, v_s, lf_s = _kv_sample(xs, kv_w, kv_b)

    cache_k2 = cache_k.reshape(n_phys, PAGE_SIZE, FA_WIDTH)
    cache_v2 = cache_v.reshape(n_phys, PAGE_SIZE, FA_WIDTH)
    cache_lft = jnp.swapaxes(cache_logf, 1, 2)

    for j in range(DEPTH - N_A_LAYERS):
        layer = N_A_LAYERS + j
        q_pm, gz_p = _q_proj(xp, mod_p[layer], fa_w_in[j], fa_b_in[j], batch=batch, seq=seq)
        o_p = _flash_attention(q_pm, kb_p, vb_p, ct_p, batch, seq).reshape(batch * seq, FA_WIDTH)
        xp = _out_proj(o_p, gz_p, xp, mod_p[layer], fa_w_out[j], fa_b_out[j], ln_g[layer], ln_b[layer], tm=ROW_TILE)

        q_s, gz_s = _q_proj(xs, mod_s[layer], fa_w_in[j], fa_b_in[j])
        o_s = _decode_attention(q_s, k_s, v_s, lf_s, cache_k2, cache_v2, cache_lft, page_table)
        xs = _out_proj(o_s, gz_s, xs, mod_s[layer], fa_w_out[j], fa_b_out[j], ln_g[layer], ln_b[layer], tm=n_dec)

    y_prompt = xp.reshape(batch, seq, D_MODEL)
    y_sample = xs.reshape(n_dec, 1, D_MODEL)
    k_prompt = k_p.reshape(batch, seq, FA_HEADS, FA_HEAD_DIM)
    v_prompt = v_p.reshape(batch, seq, FA_HEADS, FA_HEAD_DIM)
    logf_prompt = lf_p.reshape(batch, seq, FA_HEADS)
    k_sample = k_s.reshape(n_dec, 1, FA_HEADS, FA_HEAD_DIM)
    v_sample = v_s.reshape(n_dec, 1, FA_HEADS, FA_HEAD_DIM)
    logf_sample = lf_s.reshape(n_dec, 1, FA_HEADS)
    chunk_v_sample = jnp.stack(chunk_rows).reshape(N_A_LAYERS, n_dec, 1, CM_WIDTH)
    return (y_prompt, y_sample, k_prompt, v_prompt, logf_prompt, k_sample, v_sample, logf_sample, chunk_v_sample)
```

```python
import functools

import jax
import jax.numpy as jnp
from jax import lax
from jax.experimental import pallas as pl
from jax.experimental.pallas import tpu as pltpu

F32 = jnp.float32
BF16 = jnp.bfloat16

D_MODEL = 1024
DEPTH = 4
N_A_LAYERS = 2
CHUNK = 128
CM_WIDTH = 2048
CM_GROUPS = 8
CM_GROUP_DIM = CM_WIDTH // CM_GROUPS
FA_HEADS = 16
FA_HEAD_DIM = 64
FA_WIDTH = FA_HEADS * FA_HEAD_DIM
HEAD_PAIRS = FA_HEADS // 2
PAGE_SIZE = 128
FLAT = PAGE_SIZE * FA_HEADS
DEEPNORM_ALPHA = (2.0 * DEPTH) ** 0.25
LN_EPS = 1e-5
LANES = 128
NEG_BIG = -1e30
VMEM_LIMIT = 56 * 1024 * 1024

ROW_TILE = 256
ATT_TILE = 512
FLASH_ROWS = 32
PAGE_GROUP = 8
LOG2E = 1.4426950408889634


def _params(n_axes, vmem=VMEM_LIMIT):
    return pltpu.CompilerParams(dimension_semantics=("arbitrary",) * n_axes, vmem_limit_bytes=vmem)


def _dot(a, b):
    return jnp.dot(a, b, preferred_element_type=F32)


def _dot_nt(a, b):
    return lax.dot_general(a, b, (((1,), (1,)), ((), ())), preferred_element_type=F32)


def _layer_norm(x, g, b):
    mu = jnp.mean(x, axis=-1, keepdims=True)
    d = x - mu
    var = jnp.mean(d * d, axis=-1, keepdims=True)
    return d * lax.rsqrt(var + LN_EPS) * g + b


def _log_sigmoid(x):
    return -(jnp.maximum(-x, 0.0) + jnp.log1p(jnp.exp(-jnp.abs(x))))


def _split3_dot(x, w):
    hi = x.astype(BF16)
    r1 = x - hi.astype(F32)
    mid = r1.astype(BF16)
    lo = (r1 - mid.astype(F32)).astype(BF16)
    return _dot(hi, w) + _dot(mid, w) + _dot(lo, w)


def _const_spec(shape):
    nd = len(shape)
    return pl.BlockSpec(shape, lambda *_: (0,) * nd)


def _ada_kernel(c_ref, w_ref, b_ref, o_ref):
    a = jax.nn.silu(c_ref[...]).astype(BF16)
    o_ref[0] = _dot(a, w_ref[0].astype(BF16)) + b_ref[0]


def _ada_modulation(c_all, w_ada, b_ada):
    n = c_all.shape[0]
    nt = 3 * D_MODEL // D_MODEL
    return pl.pallas_call(
        _ada_kernel,
        grid=(DEPTH, nt),
        in_specs=[pl.BlockSpec((n, D_MODEL), lambda l, j: (0, 0)),
                  pl.BlockSpec((1, D_MODEL, D_MODEL), lambda l, j: (l, 0, j)),
                  pl.BlockSpec((1, 1, D_MODEL), lambda l, j: (l, 0, j))],
        out_specs=pl.BlockSpec((1, n, D_MODEL), lambda l, j: (l, 0, j)),
        out_shape=jax.ShapeDtypeStruct((DEPTH, n, 3 * D_MODEL), F32),
        compiler_params=_params(2),
        name="ada_modulation",
    )(c_all, w_ada, b_ada.reshape(DEPTH, 1, 3 * D_MODEL))


def _chunk_mlp_kernel(x_ref, mod_ref, win_ref, bin_ref, lvg_ref, lvb_ref, ws_ref, bs_ref,
                      wout_ref, bout_ref, lng_ref, lnb_ref, *rest, single_row_chunks):
    if single_row_chunks:
        o_ref, vn_ref, t_scr = rest
    else:
        o_ref, t_scr = rest
    rows = x_ref.shape[0]
    x = x_ref[...]
    m = mod_ref[0]
    shift, scale, gate = m[:, :D_MODEL], m[:, D_MODEL:2 * D_MODEL], m[:, 2 * D_MODEL:]
    h = (x * (1.0 + scale) + shift).astype(BF16)

    v = jax.nn.gelu(_dot(h, win_ref[:, CM_WIDTH:2 * CM_WIDTH]) + bin_ref[:, CM_WIDTH:2 * CM_WIDTH])
    vn = _layer_norm(v, lvg_ref[...], lvb_ref[...])
    if single_row_chunks:
        vn_ref[...] = vn
    else:
        r_i = lax.broadcasted_iota(jnp.int32, (CHUNK, CHUNK), 0)
        c_i = lax.broadcasted_iota(jnp.int32, (CHUNK, CHUNK), 1)
        causal = c_i <= r_i

    for g in range(CM_GROUPS):
        lo, hi = g * CM_GROUP_DIM, (g + 1) * CM_GROUP_DIM
        vn_g = vn[:, lo:hi]
        if single_row_chunks:
            sg = vn_g * ws_ref[:, lo:hi] + bs_ref[:, lo:hi]
        else:
            w_g = jnp.where(causal, ws_ref[g], 0.0).astype(BF16)
            b_g = bs_ref[:, g:g + 1]
            vb = vn_g.astype(BF16)
            sg = jnp.concatenate(
                [_dot(w_g, vb[c * CHUNK:(c + 1) * CHUNK]) + b_g for c in range(rows // CHUNK)], axis=0)
        u_g = jax.nn.gelu(_dot(h, win_ref[:, lo:hi]) + bin_ref[:, lo:hi])
        z_g = _dot(h, win_ref[:, 2 * CM_WIDTH + lo:2 * CM_WIDTH + hi]) + bin_ref[:, 2 * CM_WIDTH + lo:2 * CM_WIDTH + hi]
        t_scr[:, lo:hi] = (u_g * sg * jax.nn.silu(z_g)).astype(BF16)

    y = _dot(t_scr[...], wout_ref[...]) + bout_ref[...]
    r = DEEPNORM_ALPHA * x + (1.0 + gate) * y
    o_ref[...] = _layer_norm(r, lng_ref[...], lnb_ref[...])


def _chunk_mlp_layer(x, mod, w_in, b_in, lvg, lvb, w_s, b_s, w_out, b_out, ln_g, ln_b, *, single_row_chunks):
    rows = x.shape[0]
    nb, mrows = mod.shape[0], mod.shape[1]
    tm = rows if single_row_chunks else ROW_TILE
    tiles_per_batch = rows // nb // tm
    if single_row_chunks:
        ws_arg = jnp.repeat(w_s[:, 0, 0], CM_GROUP_DIM)[None]
        bs_arg = jnp.repeat(b_s[:, 0], CM_GROUP_DIM)[None]
    else:
        ws_arg, bs_arg = w_s, b_s.T
    out_shape = [jax.ShapeDtypeStruct((rows, D_MODEL), F32)]
    out_specs = [pl.BlockSpec((tm, D_MODEL), lambda i: (i, 0))]
    if single_row_chunks:
        out_shape.append(jax.ShapeDtypeStruct((rows, CM_WIDTH), F32))
        out_specs.append(pl.BlockSpec((tm, CM_WIDTH), lambda i: (i, 0)))
    res = pl.pallas_call(
        functools.partial(_chunk_mlp_kernel, single_row_chunks=single_row_chunks),
        grid=(rows // tm,),
        in_specs=[pl.BlockSpec((tm, D_MODEL), lambda i: (i, 0)),
                  pl.BlockSpec((1, mrows, 3 * D_MODEL), lambda i: (i // tiles_per_batch, 0, 0)),
                  _const_spec((D_MODEL, 3 * CM_WIDTH)), _const_spec((1, 3 * CM_WIDTH)),
                  _const_spec((1, CM_WIDTH)), _const_spec((1, CM_WIDTH)),
                  _const_spec(ws_arg.shape), _const_spec(bs_arg.shape),
                  _const_spec((CM_WIDTH, D_MODEL)), _const_spec((1, D_MODEL)),
                  _const_spec((1, D_MODEL)), _const_spec((1, D_MODEL))],
        out_specs=out_specs,
        out_shape=out_shape,
        scratch_shapes=[pltpu.VMEM((tm, CM_WIDTH), BF16)],
        compiler_params=_params(1),
        name="chunk_mlp_sample" if single_row_chunks else "chunk_mlp_prompt",
    )(x, mod, w_in.astype(BF16), b_in[None], lvg[None], lvb[None], ws_arg, bs_arg,
      w_out.astype(BF16), b_out[None], ln_g[None], ln_b[None])
    return res


def _kv_kernel(x_ref, wkv_ref, bkv_ref, wf_ref, bf_ref, *rest, prompt):
    if prompt:
        wft_ref, bft_ref, k_ref, v_ref, lf_ref, kb_ref, vb_ref, ct_ref, carry_scr = rest
    else:
        k_ref, v_ref, lf_ref = rest
    rows = x_ref.shape[0]
    xb = x_ref[...].astype(BF16)
    kv = _dot(xb, wkv_ref[...]) + bkv_ref[...]
    k, v = kv[:, :FA_WIDTH], kv[:, FA_WIDTH:]
    for h in range(FA_HEADS):
        k_ref[:, h, :] = k[:, h * FA_HEAD_DIM:(h + 1) * FA_HEAD_DIM]
        v_ref[:, h, :] = v[:, h * FA_HEAD_DIM:(h + 1) * FA_HEAD_DIM]
    lf = _log_sigmoid(_dot(xb, wf_ref[...]) + bf_ref[...])
    lf_ref[...] = lf[:, :FA_HEADS]
    if prompt:
        kb, vb = k.astype(BF16), v.astype(BF16)
        for j in range(HEAD_PAIRS):
            kb_ref[0, j] = kb[:, j * LANES:(j + 1) * LANES]
            vb_ref[0, j] = vb[:, j * LANES:(j + 1) * LANES]
        lft = _log_sigmoid(_dot_nt(wft_ref[...], xb) + bft_ref[...])
        r_i = lax.broadcasted_iota(jnp.int32, (rows, rows), 0)
        c_i = lax.broadcasted_iota(jnp.int32, (rows, rows), 1)
        upper = jnp.where(r_i <= c_i, 1.0, 0.0).astype(BF16)

        @pl.when(pl.program_id(1) == 0)
        def _():
            carry_scr[...] = jnp.zeros_like(carry_scr)

        ct = _split3_dot(lft, upper) + carry_scr[:, 0:1]
        carry_scr[...] = jnp.broadcast_to(ct[:, rows - 1:rows], carry_scr.shape)
        ct2 = ct * LOG2E
        for j in range(HEAD_PAIRS):
            ct_ref[0, j, 0] = ct2[2 * j:2 * j + 2, :]


def _kv_weights(kv_w, kv_b):
    wkv = kv_w[:, :2 * FA_WIDTH].astype(BF16)
    bkv = kv_b[None, :2 * FA_WIDTH]
    wf = jnp.pad(kv_w[:, 2 * FA_WIDTH:], ((0, 0), (0, LANES - FA_HEADS))).astype(BF16)
    bf = jnp.pad(kv_b[2 * FA_WIDTH:], (0, LANES - FA_HEADS))[None]
    return wkv, bkv, wf, bf


def _kv_prompt(x, kv_w, kv_b, batch, seq):
    tm = ATT_TILE
    nt = seq // tm
    wkv, bkv, wf, bf = _kv_weights(kv_w, kv_b)
    wft = kv_w[:, 2 * FA_WIDTH:].T.astype(BF16)
    bft = kv_b[2 * FA_WIDTH:][:, None]
    row = lambda b, i: (b * nt + i, 0)
    row3 = lambda b, i: (b * nt + i, 0, 0)
    return pl.pallas_call(
        functools.partial(_kv_kernel, prompt=True),
        grid=(batch, nt),
        in_specs=[pl.BlockSpec((tm, D_MODEL), row),
                  _const_spec(wkv.shape), _const_spec(bkv.shape), _const_spec(wf.shape), _const_spec(bf.shape),
                  _const_spec(wft.shape), _const_spec(bft.shape)],
        out_specs=[pl.BlockSpec((tm, FA_HEADS, FA_HEAD_DIM), row3), pl.BlockSpec((tm, FA_HEADS, FA_HEAD_DIM), row3),
                   pl.BlockSpec((tm, FA_HEADS), row),
                   pl.BlockSpec((1, HEAD_PAIRS, tm, LANES), lambda b, i: (b, 0, i, 0)),
                   pl.BlockSpec((1, HEAD_PAIRS, tm, LANES), lambda b, i: (b, 0, i, 0)),
                   pl.BlockSpec((1, HEAD_PAIRS, 1, 2, tm), lambda b, i: (b, 0, i, 0, 0))],
        out_shape=[jax.ShapeDtypeStruct((batch * seq, FA_HEADS, FA_HEAD_DIM), F32),
                   jax.ShapeDtypeStruct((batch * seq, FA_HEADS, FA_HEAD_DIM), F32),
                   jax.ShapeDtypeStruct((batch * seq, FA_HEADS), F32),
                   jax.ShapeDtypeStruct((batch, HEAD_PAIRS, seq, LANES), BF16),
                   jax.ShapeDtypeStruct((batch, HEAD_PAIRS, seq, LANES), BF16),
                   jax.ShapeDtypeStruct((batch, HEAD_PAIRS, nt, 2, tm), F32)],
        scratch_shapes=[pltpu.VMEM((FA_HEADS, LANES), F32)],
        compiler_params=_params(2),
        name="kv_prompt",
    )(x, wkv, bkv, wf, bf, wft, bft)


def _kv_sample(x, kv_w, kv_b):
    rows = x.shape[0]
    wkv, bkv, wf, bf = _kv_weights(kv_w, kv_b)
    full = lambda i: (0, 0)
    return pl.pallas_call(
        functools.partial(_kv_kernel, prompt=False),
        grid=(1,),
        in_specs=[pl.BlockSpec((rows, D_MODEL), full),
                  _const_spec(wkv.shape), _const_spec(bkv.shape), _const_spec(wf.shape), _const_spec(bf.shape)],
        out_specs=[pl.BlockSpec((rows, FA_HEADS, FA_HEAD_DIM), lambda i: (0, 0, 0)),
                   pl.BlockSpec((rows, FA_HEADS, FA_HEAD_DIM), lambda i: (0, 0, 0)),
                   pl.BlockSpec((rows, FA_HEADS), full)],
        out_shape=[jax.ShapeDtypeStruct((rows, FA_HEADS, FA_HEAD_DIM), F32),
                   jax.ShapeDtypeStruct((rows, FA_HEADS, FA_HEAD_DIM), F32),
                   jax.ShapeDtypeStruct((rows, FA_HEADS), F32)],
        compiler_params=_params(1),
        name="kv_sample",
    )(x, wkv, bkv, wf, bf)


def _q_proj_kernel(x_ref, mod_ref, w_ref, b_ref, q_ref, gz_ref, *, pair_major):
    x = x_ref[...]
    m = mod_ref[0]
    h = (x * (1.0 + m[:, D_MODEL:2 * D_MODEL]) + m[:, :D_MODEL]).astype(BF16)
    p = _dot(h, w_ref[...]) + b_ref[...]
    if pair_major:
        q = (p[:, :FA_WIDTH] * (LOG2E * FA_HEAD_DIM ** -0.5)).astype(BF16)
        for j in range(HEAD_PAIRS):
            q_ref[0, j] = q[:, j * LANES:(j + 1) * LANES]
    else:
        q = p[:, :FA_WIDTH] * (FA_HEAD_DIM ** -0.5)
        for h in range(FA_HEADS):
            q_ref[:, h, :] = q[:, h * FA_HEAD_DIM:(h + 1) * FA_HEAD_DIM]
    gz_ref[...] = jax.nn.silu(p[:, FA_WIDTH:])


def _q_proj(x, mod, w_in, b_in, *, batch=None, seq=None):
    rows = x.shape[0]
    nb, mrows = mod.shape[0], mod.shape[1]
    pair_major = batch is not None
    tm = ROW_TILE if pair_major else rows
    tpb = rows // nb // tm
    if pair_major:
        q_spec = pl.BlockSpec((1, HEAD_PAIRS, tm, LANES), lambda i: (i // tpb, 0, i % tpb, 0))
        q_shape = jax.ShapeDtypeStruct((batch, HEAD_PAIRS, seq, LANES), BF16)
    else:
        q_spec = pl.BlockSpec((tm, FA_HEADS, FA_HEAD_DIM), lambda i: (i, 0, 0))
        q_shape = jax.ShapeDtypeStruct((rows, FA_HEADS, FA_HEAD_DIM), F32)
    return pl.pallas_call(
        functools.partial(_q_proj_kernel, pair_major=pair_major),
        grid=(rows // tm,),
        in_specs=[pl.BlockSpec((tm, D_MODEL), lambda i: (i, 0)),
                  pl.BlockSpec((1, mrows, 3 * D_MODEL), lambda i: (i // tpb, 0, 0)),
                  _const_spec((D_MODEL, 2 * FA_WIDTH)), _const_spec((1, 2 * FA_WIDTH))],
        out_specs=[q_spec, pl.BlockSpec((tm, FA_WIDTH), lambda i: (i, 0))],
        out_shape=[q_shape, jax.ShapeDtypeStruct((rows, FA_WIDTH), F32)],
        compiler_params=_params(1),
        name="q_proj_prompt" if pair_major else "q_proj_sample",
    )(x, mod, w_in.astype(BF16), b_in[None])


def _out_proj_kernel(o_ref, gz_ref, x_ref, mod_ref, w_ref, b_ref, g_ref, bb_ref, y_ref):
    t = (o_ref[...] * gz_ref[...]).astype(BF16)
    y = _dot(t, w_ref[...]) + b_ref[...]
    gate = mod_ref[0][:, 2 * D_MODEL:]
    r = DEEPNORM_ALPHA * x_ref[...] + (1.0 + gate) * y
    y_ref[...] = _layer_norm(r, g_ref[...], bb_ref[...])


def _out_proj(o, gz, x, mod, w_out, b_out, ln_g, ln_b, *, tm):
    rows = x.shape[0]
    nb, mrows = mod.shape[0], mod.shape[1]
    tpb = rows // nb // tm
    row = lambda i: (i, 0)
    return pl.pallas_call(
        _out_proj_kernel,
        grid=(rows // tm,),
        in_specs=[pl.BlockSpec((tm, FA_WIDTH), row), pl.BlockSpec((tm, FA_WIDTH), row),
                  pl.BlockSpec((tm, D_MODEL), row),
                  pl.BlockSpec((1, mrows, 3 * D_MODEL), lambda i: (i // tpb, 0, 0)),
                  _const_spec((FA_WIDTH, D_MODEL)), _const_spec((1, D_MODEL)),
                  _const_spec((1, D_MODEL)), _const_spec((1, D_MODEL))],
        out_specs=pl.BlockSpec((tm, D_MODEL), row),
        out_shape=jax.ShapeDtypeStruct((rows, D_MODEL), F32),
        compiler_params=_params(1),
        name="out_proj",
    )(o, gz, x, mod, w_out.astype(BF16), b_out[None], ln_g[None], ln_b[None])


def _flash_kernel(q_ref, k_ref, v_ref, c_ref, o_ref, s_scr, *head_scr, tile, n_tiles):
    p_scr, m_scr, mnew_scr, alpha_scr, acc_scr = (head_scr[0:2], head_scr[2:4], head_scr[4:6],
                                                  head_scr[6:8], head_scr[8:10])
    lane = lax.broadcasted_iota(jnp.int32, (1, LANES), 1)
    first = lane < FA_HEAD_DIM

    def softmax_blocks(a, c_row, masked):
        for r0 in range(0, tile, FLASH_ROWS):
            rows = slice(r0, r0 + FLASH_ROWS)
            ncols = min(tile, -(-(r0 + FLASH_ROWS) // LANES) * LANES) if masked else tile
            s = s_scr[a * tile + r0:a * tile + r0 + FLASH_ROWS, :ncols] - c_row[:, :ncols]
            if masked:
                r_i = r0 + lax.broadcasted_iota(jnp.int32, (FLASH_ROWS, ncols), 0)
                c_i = lax.broadcasted_iota(jnp.int32, (FLASH_ROWS, ncols), 1)
                s = jnp.where(c_i <= r_i, s, NEG_BIG)
            m_old = m_scr[a][rows, :]
            m_new = jnp.maximum(m_old, jnp.max(s, axis=-1, keepdims=True))
            mnew_scr[a][rows, :] = m_new
            alpha_scr[a][rows, :] = jnp.exp2(m_old - m_new)
            p_scr[a][rows, :ncols] = jnp.exp2(s - m_new).astype(BF16)
            if ncols < tile:
                p_scr[a][rows, ncols:] = jnp.zeros((FLASH_ROWS, tile - ncols), BF16)

    def kv_tile(t, q_stack, masked):
        start = pl.multiple_of(t * tile, tile)
        k2 = k_ref[0, 0, pl.ds(start, tile), :]
        v2 = v_ref[0, 0, pl.ds(start, tile), :]
        one = jnp.ones_like(v2)
        v_heads = (jnp.where(first, v2, one), jnp.where(first, one, v2))
        c2 = c_ref[0, 0, t]
        s_scr[...] = _dot_nt(q_stack, k2)
        for a in range(2):
            softmax_blocks(a, c2[a:a + 1, :], masked)
            acc_scr[a][...] = alpha_scr[a][...] * acc_scr[a][...] + _dot(p_scr[a][...], v_heads[a])
        for a in range(2):
            m_scr[a][...] = mnew_scr[a][...]

    def q_body(i, _):
        q2 = q_ref[0, 0, pl.ds(pl.multiple_of(i * tile, tile), tile), :]
        zero = jnp.zeros_like(q2)
        q_stack = jnp.concatenate([jnp.where(first, q2, zero), jnp.where(first, zero, q2)], axis=0)
        for a in range(2):
            m_scr[a][...] = jnp.full(m_scr[a].shape, NEG_BIG, F32)
            acc_scr[a][...] = jnp.zeros(acc_scr[a].shape, F32)

        def off_diagonal(t, _):
            kv_tile(t, q_stack, False)
            return 0

        lax.fori_loop(0, i, off_diagonal, 0)
        kv_tile(i, q_stack, True)
        acc_a, acc_b = acc_scr[0][...], acc_scr[1][...]
        o_a = acc_a / pltpu.roll(acc_a, FA_HEAD_DIM, axis=1)
        o_b = acc_b / pltpu.roll(acc_b, FA_HEAD_DIM, axis=1)
        o_ref[0, pl.ds(pl.multiple_of(i * tile, tile), tile), :] = jnp.where(first, o_a, o_b)
        return 0

    lax.fori_loop(0, n_tiles, q_body, 0)


def _flash_attention(q_pm, k_pm, v_pm, c_t, batch, seq):
    tile = ATT_TILE
    nt = seq // tile
    pm = lambda b, j: (b, j, 0, 0)
    return pl.pallas_call(
        functools.partial(_flash_kernel, tile=tile, n_tiles=nt),
        grid=(batch, HEAD_PAIRS),
        in_specs=[pl.BlockSpec((1, 1, seq, LANES), pm), pl.BlockSpec((1, 1, seq, LANES), pm),
                  pl.BlockSpec((1, 1, seq, LANES), pm),
                  pl.BlockSpec((1, 1, nt, 2, tile), lambda b, j: (b, j, 0, 0, 0))],
        out_specs=pl.BlockSpec((1, seq, LANES), lambda b, j: (b, 0, j)),
        out_shape=jax.ShapeDtypeStruct((batch, seq, FA_WIDTH), F32),
        scratch_shapes=([pltpu.VMEM((2 * tile, tile), F32)]
                        + [pltpu.VMEM((tile, tile), BF16)] * 2
                        + [pltpu.VMEM((tile, 1), F32)] * 6
                        + [pltpu.VMEM((tile, LANES), F32)] * 2),
        compiler_params=_params(2),
        name="flash_prompt",
    )(q_pm, k_pm, v_pm, c_t)


def _decode_kernel(pt_ref, q_ref, kn_ref, vn_ref, lfn_ref, *refs, pg, n_groups):
    del pt_ref
    k_refs, v_refs, lf_refs = refs[:pg], refs[pg:2 * pg], refs[2 * pg:3 * pg]
    o_ref, m_scr, l_scr, acc_scr, off_scr = refs[3 * pg:]
    g = pl.program_id(1)

    @pl.when(g == 0)
    def _():
        m_scr[...] = jnp.full(m_scr.shape, NEG_BIG, F32)
        l_scr[...] = jnp.zeros(l_scr.shape, F32)
        acc_scr[...] = jnp.zeros(acc_scr.shape, F32)
        off_scr[...] = jnp.zeros(off_scr.shape, F32)

    q = q_ref[0]
    q_bf = q.astype(BF16)

    lanef = lax.broadcasted_iota(jnp.int32, (pg, FLAT), 1)
    rowf = lax.broadcasted_iota(jnp.int32, (pg, FLAT), 0)
    x = jnp.concatenate([lf_refs[i][0] for i in range(pg)], axis=0)
    for step in range(PAGE_SIZE.bit_length() - 1):
        shift = FA_HEADS << step
        x = x + jnp.where(lanef >= shift, pltpu.roll(x, shift, axis=1), 0.0)
    tot = jnp.where(lanef >= FLAT - FA_HEADS, x, 0.0)
    for step in range(PAGE_SIZE.bit_length() - 1):
        tot = tot + pltpu.roll(tot, FLAT - (FA_HEADS << step), axis=1)
    inc = tot
    for step in range(pg.bit_length() - 1):
        inc = inc + jnp.where(rowf >= (1 << step), pltpu.roll(inc, 1 << step, axis=0), 0.0)
    carry = off_scr[...]
    c_flat = x + (inc - tot) + carry
    carry_new = carry + jnp.broadcast_to(inc[pg - 1:pg], (pg, FLAT))
    off_scr[...] = carry_new

    row_h = lax.broadcasted_iota(jnp.int32, (FA_HEADS, FLAT), 0)
    col_h = lax.broadcasted_iota(jnp.int32, (FA_HEADS, FLAT), 1) & (FA_HEADS - 1)
    own = row_h == col_h
    s_pages = []
    for i in range(pg):
        k_i = k_refs[i][0].reshape(FLAT, FA_HEAD_DIM).astype(BF16)
        s_i = _dot_nt(q_bf, k_i) - c_flat[i:i + 1, :]
        s_pages.append(jnp.where(own, s_i, NEG_BIG))
    m_old = m_scr[:, 0:1]
    m_new = jnp.maximum(m_old, jnp.max(functools.reduce(jnp.maximum, s_pages), axis=1, keepdims=True))
    alpha = jnp.exp(m_old - m_new)
    p_sum = jnp.zeros((FA_HEADS, FLAT), F32)
    upd = jnp.zeros((FA_HEADS, FA_HEAD_DIM), F32)
    for i in range(pg):
        p_i = jnp.exp(s_pages[i] - m_new)
        p_sum = p_sum + p_i
        upd = upd + _dot(p_i.astype(BF16), v_refs[i][0].reshape(FLAT, FA_HEAD_DIM).astype(BF16))
    l_new = alpha * l_scr[:, 0:1] + jnp.sum(p_sum, axis=1, keepdims=True)
    acc_new = alpha * acc_scr[...] + upd
    m_scr[...] = jnp.broadcast_to(m_new, m_scr.shape)
    l_scr[...] = jnp.broadcast_to(l_new, l_scr.shape)
    acc_scr[...] = acc_new

    @pl.when(g == n_groups - 1)
    def _():
        head_lane = lax.broadcasted_iota(jnp.int32, (FA_HEADS, LANES), 1) == lax.broadcasted_iota(
            jnp.int32, (FA_HEADS, LANES), 0)
        total = jnp.sum(jnp.where(head_lane, jnp.broadcast_to(carry_new[0:1, :LANES], (FA_HEADS, LANES)), 0.0),
                        axis=1, keepdims=True)
        c_query = total + lfn_ref[0]
        s_new = jnp.sum(q * kn_ref[0], axis=1, keepdims=True)
        m_past = m_new + c_query
        m_fin = jnp.maximum(m_past, s_new)
        w_past = jnp.exp(m_past - m_fin)
        w_new = jnp.exp(s_new - m_fin)
        o_ref[0] = (acc_new * w_past + w_new * vn_ref[0]) / (l_new * w_past + w_new)


def _decode_attention(q, k_new, v_new, lf_new, cache_k, cache_v, cache_logf, page_table):
    n, n_pages = page_table.shape
    n_phys = cache_k.shape[0]
    pg = PAGE_GROUP
    n_groups = n_pages // pg

    def kv_map(i):
        return lambda b, g, pt: (pt[b, g * pg + i], 0, 0, 0)

    def lf_map(i):
        return lambda b, g, pt: (pt[b, g * pg + i], 0, 0)

    row = lambda b, g, pt: (b, 0, 0)
    head_block = pl.BlockSpec((1, FA_HEADS, FA_HEAD_DIM), row)
    in_specs = [head_block, head_block, head_block, pl.BlockSpec((1, FA_HEADS, 1), row)]
    in_specs += [pl.BlockSpec((1, PAGE_SIZE, FA_HEADS, FA_HEAD_DIM), kv_map(i)) for i in range(pg)]
    in_specs += [pl.BlockSpec((1, PAGE_SIZE, FA_HEADS, FA_HEAD_DIM), kv_map(i)) for i in range(pg)]
    in_specs += [pl.BlockSpec((1, 1, FLAT), lf_map(i)) for i in range(pg)]
    return pl.pallas_call(
        functools.partial(_decode_kernel, pg=pg, n_groups=n_groups),
        grid_spec=pltpu.PrefetchScalarGridSpec(
            num_scalar_prefetch=1, grid=(n, n_groups), in_specs=in_specs, out_specs=head_block,
            scratch_shapes=[pltpu.VMEM((FA_HEADS, LANES), F32), pltpu.VMEM((FA_HEADS, LANES), F32),
                            pltpu.VMEM((FA_HEADS, FA_HEAD_DIM), F32), pltpu.VMEM((pg, FLAT), F32)]),
        out_shape=jax.ShapeDtypeStruct((n, FA_HEADS, FA_HEAD_DIM), F32),
        compiler_params=_params(2),
        name="decode_attention",
    )(page_table, q, k_new, v_new, lf_new.reshape(n, FA_HEADS, 1),
      *([cache_k] * pg), *([cache_v] * pg), *([cache_logf.reshape(n_phys, 1, FLAT)] * pg))


def kernel(x_prompt, x_sample, c_prompt, c_sample, cache_k, cache_v, cache_logf, page_table, w_ada, b_ada, ln_g, ln_b, cm_w_in, cm_b_in, cm_ln_v_g, cm_ln_v_b, cm_w_s, cm_b_s, cm_w_out, cm_b_out, kv_w, kv_b, fa_w_in, fa_b_in, fa_w_out, fa_b_out):
    batch, seq, _ = x_prompt.shape
    n_dec = x_sample.shape[0]

    n_c = batch + n_dec
    c_all = jnp.pad(jnp.concatenate([c_prompt, c_sample], axis=0), ((0, (-n_c) % 8), (0, 0)))
    mod = _ada_modulation(c_all, w_ada, b_ada)
    mod_p = mod[:, :batch].reshape(DEPTH, batch, 1, 3 * D_MODEL)
    mod_s = mod[:, batch:n_c].reshape(DEPTH, 1, n_dec, 3 * D_MODEL)

    xp = x_prompt.reshape(batch * seq, D_MODEL)
    xs = x_sample.reshape(n_dec, D_MODEL)
    chunk_rows = []
    for layer in range(N_A_LAYERS):
        args = (cm_w_in[layer], cm_b_in[layer], cm_ln_v_g[layer], cm_ln_v_b[layer], cm_w_s[layer], cm_b_s[layer],
                cm_w_out[layer], cm_b_out[layer], ln_g[layer], ln_b[layer])
        (xp,) = _chunk_mlp_layer(xp, mod_p[layer], *args, single_row_chunks=False)
        xs, vn_s = _chunk_mlp_layer(xs, mod_s[layer], *args, single_row_chunks=True)
        chunk_rows.append(vn_s)

    k_p, v_p, lf_p, kb_p, vb_p, ct_p = _kv_prompt(xp, kv_w, kv_b, batch, seq)
    k_s, v_s, lf_s = _kv_sample(xs, kv_w, kv_b)


    for j in range(DEPTH - N_A_LAYERS):
        layer = N_A_LAYERS + j
        q_pm, gz_p = _q_proj(xp, mod_p[layer], fa_w_in[j], fa_b_in[j], batch=batch, seq=seq)
        o_p = _flash_attention(q_pm, kb_p, vb_p, ct_p, batch, seq).reshape(batch * seq, FA_WIDTH)
        xp = _out_proj(o_p, gz_p, xp, mod_p[layer], fa_w_out[j], fa_b_out[j], ln_g[layer], ln_b[layer], tm=ROW_TILE)

        q_s, gz_s = _q_proj(xs, mod_s[layer], fa_w_in[j], fa_b_in[j])
        o_s = _decode_attention(q_s, k_s, v_s, lf_s, cache_k, cache_v, cache_logf, page_table).reshape(n_dec, FA_WIDTH)
        xs = _out_proj(o_s, gz_s, xs, mod_s[layer], fa_w_out[j], fa_b_out[j], ln_g[layer], ln_b[layer], tm=n_dec)

    y_prompt = xp.reshape(batch, seq, D_MODEL)
    y_sample = xs.reshape(n_dec, 1, D_MODEL)
    k_prompt = k_p.reshape(batch, seq, FA_HEADS, FA_HEAD_DIM)
    v_prompt = v_p.reshape(batch, seq, FA_HEADS, FA_HEAD_DIM)
    logf_prompt = lf_p.reshape(batch, seq, FA_HEADS)
    k_sample = k_s.reshape(n_dec, 1, FA_HEADS, FA_HEAD_DIM)
    v_sample = v_s.reshape(n_dec, 1, FA_HEADS, FA_HEAD_DIM)
    logf_sample = lf_s.reshape(n_dec, 1, FA_HEADS)
    chunk_v_sample = jnp.stack(chunk_rows).reshape(N_A_LAYERS, n_dec, 1, CM_WIDTH)
    return (y_prompt, y_sample, k_prompt, v_prompt, logf_prompt, k_sample, v_sample, logf_sample, chunk_v_sample)
```

```python
import functools

import jax
import jax.numpy as jnp
from jax import lax
from jax.experimental import pallas as pl
from jax.experimental.pallas import tpu as pltpu

F32 = jnp.float32
BF16 = jnp.bfloat16

D_MODEL = 1024
DEPTH = 4
N_A_LAYERS = 2
CHUNK = 128
CM_WIDTH = 2048
CM_GROUPS = 8
CM_GROUP_DIM = CM_WIDTH // CM_GROUPS
FA_HEADS = 16
FA_HEAD_DIM = 64
FA_WIDTH = FA_HEADS * FA_HEAD_DIM
HEAD_PAIRS = FA_HEADS // 2
PAGE_SIZE = 128
DEEPNORM_ALPHA = (2.0 * DEPTH) ** 0.25
LN_EPS = 1e-5
LANES = 128
NEG_BIG = -1e30
VMEM_LIMIT = 56 * 1024 * 1024

ROW_TILE = 256
ATT_TILE = 512
FLASH_ROWS = 32
LOG2E = 1.4426950408889634


def _params(n_axes, vmem=VMEM_LIMIT):
    return pltpu.CompilerParams(dimension_semantics=("arbitrary",) * n_axes, vmem_limit_bytes=vmem)


def _dot(a, b):
    return jnp.dot(a, b, preferred_element_type=F32)


def _dot_nt(a, b):
    return lax.dot_general(a, b, (((1,), (1,)), ((), ())), preferred_element_type=F32)


def _layer_norm(x, g, b):
    mu = jnp.mean(x, axis=-1, keepdims=True)
    d = x - mu
    var = jnp.mean(d * d, axis=-1, keepdims=True)
    return d * lax.rsqrt(var + LN_EPS) * g + b


def _log_sigmoid(x):
    return -(jnp.maximum(-x, 0.0) + jnp.log1p(jnp.exp(-jnp.abs(x))))


def _split3_dot(x, w):
    hi = x.astype(BF16)
    r1 = x - hi.astype(F32)
    mid = r1.astype(BF16)
    lo = (r1 - mid.astype(F32)).astype(BF16)
    return _dot(hi, w) + _dot(mid, w) + _dot(lo, w)


def _const_spec(shape):
    nd = len(shape)
    return pl.BlockSpec(shape, lambda *_: (0,) * nd)


def _ada_kernel(c_ref, w_ref, b_ref, o_ref):
    a = jax.nn.silu(c_ref[...]).astype(BF16)
    o_ref[0] = _dot(a, w_ref[0].astype(BF16)) + b_ref[0]


def _ada_modulation(c_all, w_ada, b_ada):
    n = c_all.shape[0]
    nt = 3 * D_MODEL // D_MODEL
    return pl.pallas_call(
        _ada_kernel,
        grid=(DEPTH, nt),
        in_specs=[pl.BlockSpec((n, D_MODEL), lambda l, j: (0, 0)),
                  pl.BlockSpec((1, D_MODEL, D_MODEL), lambda l, j: (l, 0, j)),
                  pl.BlockSpec((1, 1, D_MODEL), lambda l, j: (l, 0, j))],
        out_specs=pl.BlockSpec((1, n, D_MODEL), lambda l, j: (l, 0, j)),
        out_shape=jax.ShapeDtypeStruct((DEPTH, n, 3 * D_MODEL), F32),
        compiler_params=_params(2),
        name="ada_modulation",
    )(c_all, w_ada, b_ada.reshape(DEPTH, 1, 3 * D_MODEL))


def _chunk_mlp_kernel(x_ref, mod_ref, win_ref, bin_ref, lvg_ref, lvb_ref, ws_ref, bs_ref,
                      wout_ref, bout_ref, lng_ref, lnb_ref, *rest, single_row_chunks):
    if single_row_chunks:
        o_ref, vn_ref, t_scr = rest
    else:
        o_ref, t_scr = rest
    rows = x_ref.shape[0]
    x = x_ref[...]
    m = mod_ref[0]
    shift, scale, gate = m[:, :D_MODEL], m[:, D_MODEL:2 * D_MODEL], m[:, 2 * D_MODEL:]
    h = (x * (1.0 + scale) + shift).astype(BF16)

    v = jax.nn.gelu(_dot(h, win_ref[:, CM_WIDTH:2 * CM_WIDTH]) + bin_ref[:, CM_WIDTH:2 * CM_WIDTH])
    vn = _layer_norm(v, lvg_ref[...], lvb_ref[...])
    if single_row_chunks:
        vn_ref[...] = vn
    else:
        r_i = lax.broadcasted_iota(jnp.int32, (CHUNK, CHUNK), 0)
        c_i = lax.broadcasted_iota(jnp.int32, (CHUNK, CHUNK), 1)
        causal = c_i <= r_i

    for g in range(CM_GROUPS):
        lo, hi = g * CM_GROUP_DIM, (g + 1) * CM_GROUP_DIM
        vn_g = vn[:, lo:hi]
        if single_row_chunks:
            sg = vn_g * ws_ref[:, lo:hi] + bs_ref[:, lo:hi]
        else:
            w_g = jnp.where(causal, ws_ref[g], 0.0).astype(BF16)
            b_g = bs_ref[:, g:g + 1]
            vb = vn_g.astype(BF16)
            sg = jnp.concatenate(
                [_dot(w_g, vb[c * CHUNK:(c + 1) * CHUNK]) + b_g for c in range(rows // CHUNK)], axis=0)
        u_g = jax.nn.gelu(_dot(h, win_ref[:, lo:hi]) + bin_ref[:, lo:hi])
        z_g = _dot(h, win_ref[:, 2 * CM_WIDTH + lo:2 * CM_WIDTH + hi]) + bin_ref[:, 2 * CM_WIDTH + lo:2 * CM_WIDTH + hi]
        t_scr[:, lo:hi] = (u_g * sg * jax.nn.silu(z_g)).astype(BF16)

    y = _dot(t_scr[...], wout_ref[...]) + bout_ref[...]
    r = DEEPNORM_ALPHA * x + (1.0 + gate) * y
    o_ref[...] = _layer_norm(r, lng_ref[...], lnb_ref[...])


def _chunk_mlp_layer(x, mod, w_in, b_in, lvg, lvb, w_s, b_s, w_out, b_out, ln_g, ln_b, *, single_row_chunks):
    rows = x.shape[0]
    nb, mrows = mod.shape[0], mod.shape[1]
    tm = rows if single_row_chunks else ROW_TILE
    tiles_per_batch = rows // nb // tm
    if single_row_chunks:
        ws_arg = jnp.repeat(w_s[:, 0, 0], CM_GROUP_DIM)[None]
        bs_arg = jnp.repeat(b_s[:, 0], CM_GROUP_DIM)[None]
    else:
        ws_arg, bs_arg = w_s, b_s.T
    out_shape = [jax.ShapeDtypeStruct((rows, D_MODEL), F32)]
    out_specs = [pl.BlockSpec((tm, D_MODEL), lambda i: (i, 0))]
    if single_row_chunks:
        out_shape.append(jax.ShapeDtypeStruct((rows, CM_WIDTH), F32))
        out_specs.append(pl.BlockSpec((tm, CM_WIDTH), lambda i: (i, 0)))
    res = pl.pallas_call(
        functools.partial(_chunk_mlp_kernel, single_row_chunks=single_row_chunks),
        grid=(rows // tm,),
        in_specs=[pl.BlockSpec((tm, D_MODEL), lambda i: (i, 0)),
                  pl.BlockSpec((1, mrows, 3 * D_MODEL), lambda i: (i // tiles_per_batch, 0, 0)),
                  _const_spec((D_MODEL, 3 * CM_WIDTH)), _const_spec((1, 3 * CM_WIDTH)),
                  _const_spec((1, CM_WIDTH)), _const_spec((1, CM_WIDTH)),
                  _const_spec(ws_arg.shape), _const_spec(bs_arg.shape),
                  _const_spec((CM_WIDTH, D_MODEL)), _const_spec((1, D_MODEL)),
                  _const_spec((1, D_MODEL)), _const_spec((1, D_MODEL))],
        out_specs=out_specs,
        out_shape=out_shape,
        scratch_shapes=[pltpu.VMEM((tm, CM_WIDTH), BF16)],
        compiler_params=_params(1),
        name="chunk_mlp_sample" if single_row_chunks else "chunk_mlp_prompt",
    )(x, mod, w_in.astype(BF16), b_in[None], lvg[None], lvb[None], ws_arg, bs_arg,
      w_out.astype(BF16), b_out[None], ln_g[None], ln_b[None])
    return res


def _kv_kernel(x_ref, wkv_ref, bkv_ref, wkvt_ref, bkvt_ref, wft_ref, bft_ref, *rest, prompt):
    if prompt:
        kt_ref, vt_ref, lft_ref, kb_ref, vb_ref, ct_ref, carry_scr = rest
    else:
        wf_ref, bf_ref, kt_ref, vt_ref, lft_ref, k_ref, v_ref, lf_ref = rest
    rows = x_ref.shape[0]
    xb = x_ref[...].astype(BF16)
    kvt = _dot_nt(wkvt_ref[...], xb) + bkvt_ref[...]
    kt_ref[0] = kvt[:FA_WIDTH]
    vt_ref[0] = kvt[FA_WIDTH:]
    lft = _log_sigmoid(_dot_nt(wft_ref[...], xb) + bft_ref[...])
    lft_ref[0] = lft
    kv = _dot(xb, wkv_ref[...]) + bkv_ref[...]
    k, v = kv[:, :FA_WIDTH], kv[:, FA_WIDTH:]
    if not prompt:
        k_ref[...] = k
        v_ref[...] = v
        lf_ref[...] = _log_sigmoid(_dot(xb, wf_ref[...]) + bf_ref[...])[:, :FA_HEADS]
    else:
        kb, vb = k.astype(BF16), v.astype(BF16)
        for j in range(HEAD_PAIRS):
            kb_ref[0, j] = kb[:, j * LANES:(j + 1) * LANES]
            vb_ref[0, j] = vb[:, j * LANES:(j + 1) * LANES]
        r_i = lax.broadcasted_iota(jnp.int32, (rows, rows), 0)
        c_i = lax.broadcasted_iota(jnp.int32, (rows, rows), 1)
        upper = jnp.where(r_i <= c_i, 1.0, 0.0).astype(BF16)

        @pl.when(pl.program_id(1) == 0)
        def _():
            carry_scr[...] = jnp.zeros_like(carry_scr)

        ct = _split3_dot(lft, upper) + carry_scr[:, 0:1]
        carry_scr[...] = jnp.broadcast_to(ct[:, rows - 1:rows], carry_scr.shape)
        ct2 = ct * LOG2E
        for j in range(HEAD_PAIRS):
            ct_ref[0, j, 0] = ct2[2 * j:2 * j + 2, :]


def _kv_weights(kv_w, kv_b):
    wkv = kv_w[:, :2 * FA_WIDTH].astype(BF16)
    bkv = kv_b[None, :2 * FA_WIDTH]
    wkvt = kv_w[:, :2 * FA_WIDTH].T.astype(BF16)
    bkvt = kv_b[:2 * FA_WIDTH, None]
    wft = kv_w[:, 2 * FA_WIDTH:].T.astype(BF16)
    bft = kv_b[2 * FA_WIDTH:, None]
    return wkv, bkv, wkvt, bkvt, wft, bft


def _kv_prompt(x, kv_w, kv_b, batch, seq):
    tm = ATT_TILE
    nt = seq // tm
    weights = _kv_weights(kv_w, kv_b)
    return pl.pallas_call(
        functools.partial(_kv_kernel, prompt=True),
        grid=(batch, nt),
        in_specs=[pl.BlockSpec((tm, D_MODEL), lambda b, i: (b * nt + i, 0))] + [_const_spec(w.shape) for w in weights],
        out_specs=[pl.BlockSpec((1, FA_WIDTH, tm), lambda b, i: (b, 0, i)),
                   pl.BlockSpec((1, FA_WIDTH, tm), lambda b, i: (b, 0, i)),
                   pl.BlockSpec((1, FA_HEADS, tm), lambda b, i: (b, 0, i)),
                   pl.BlockSpec((1, HEAD_PAIRS, tm, LANES), lambda b, i: (b, 0, i, 0)),
                   pl.BlockSpec((1, HEAD_PAIRS, tm, LANES), lambda b, i: (b, 0, i, 0)),
                   pl.BlockSpec((1, HEAD_PAIRS, 1, 2, tm), lambda b, i: (b, 0, i, 0, 0))],
        out_shape=[jax.ShapeDtypeStruct((batch, FA_WIDTH, seq), F32),
                   jax.ShapeDtypeStruct((batch, FA_WIDTH, seq), F32),
                   jax.ShapeDtypeStruct((batch, FA_HEADS, seq), F32),
                   jax.ShapeDtypeStruct((batch, HEAD_PAIRS, seq, LANES), BF16),
                   jax.ShapeDtypeStruct((batch, HEAD_PAIRS, seq, LANES), BF16),
                   jax.ShapeDtypeStruct((batch, HEAD_PAIRS, nt, 2, tm), F32)],
        scratch_shapes=[pltpu.VMEM((FA_HEADS, LANES), F32)],
        compiler_params=_params(2),
        name="kv_prompt",
    )(x, *weights)


def _kv_sample(x, kv_w, kv_b):
    rows = x.shape[0]
    weights = _kv_weights(kv_w, kv_b)
    wf = jnp.pad(kv_w[:, 2 * FA_WIDTH:], ((0, 0), (0, LANES - FA_HEADS))).astype(BF16)
    bf = jnp.pad(kv_b[2 * FA_WIDTH:], (0, LANES - FA_HEADS))[None]
    weights = weights + (wf, bf)
    full = lambda i: (0, 0)
    full3 = lambda i: (0, 0, 0)
    return pl.pallas_call(
        functools.partial(_kv_kernel, prompt=False),
        grid=(1,),
        in_specs=[pl.BlockSpec((rows, D_MODEL), full)] + [_const_spec(w.shape) for w in weights],
        out_specs=[pl.BlockSpec((1, FA_WIDTH, rows), full3), pl.BlockSpec((1, FA_WIDTH, rows), full3),
                   pl.BlockSpec((1, FA_HEADS, rows), full3),
                   pl.BlockSpec((rows, FA_WIDTH), full), pl.BlockSpec((rows, FA_WIDTH), full),
                   pl.BlockSpec((rows, FA_HEADS), full)],
        out_shape=[jax.ShapeDtypeStruct((1, FA_WIDTH, rows), F32), jax.ShapeDtypeStruct((1, FA_WIDTH, rows), F32),
                   jax.ShapeDtypeStruct((1, FA_HEADS, rows), F32),
                   jax.ShapeDtypeStruct((rows, FA_WIDTH), F32), jax.ShapeDtypeStruct((rows, FA_WIDTH), F32),
                   jax.ShapeDtypeStruct((rows, FA_HEADS), F32)],
        compiler_params=_params(1),
        name="kv_sample",
    )(x, *weights)


def _q_proj_kernel(x_ref, mod_ref, w_ref, b_ref, q_ref, gz_ref, *, pair_major):
    x = x_ref[...]
    m = mod_ref[0]
    h = (x * (1.0 + m[:, D_MODEL:2 * D_MODEL]) + m[:, :D_MODEL]).astype(BF16)
    p = _dot(h, w_ref[...]) + b_ref[...]
    if pair_major:
        q = (p[:, :FA_WIDTH] * (LOG2E * FA_HEAD_DIM ** -0.5)).astype(BF16)
        for j in range(HEAD_PAIRS):
            q_ref[0, j] = q[:, j * LANES:(j + 1) * LANES]
    else:
        q_ref[...] = (p[:, :FA_WIDTH] * (FA_HEAD_DIM ** -0.5)).astype(BF16)
    gz_ref[...] = jax.nn.silu(p[:, FA_WIDTH:])


def _q_proj(x, mod, w_in, b_in, *, batch=None, seq=None):
    rows = x.shape[0]
    nb, mrows = mod.shape[0], mod.shape[1]
    pair_major = batch is not None
    tm = ROW_TILE if pair_major else rows
    tpb = rows // nb // tm
    if pair_major:
        q_spec = pl.BlockSpec((1, HEAD_PAIRS, tm, LANES), lambda i: (i // tpb, 0, i % tpb, 0))
        q_shape = jax.ShapeDtypeStruct((batch, HEAD_PAIRS, seq, LANES), BF16)
    else:
        q_spec = pl.BlockSpec((tm, FA_WIDTH), lambda i: (i, 0))
        q_shape = jax.ShapeDtypeStruct((rows, FA_WIDTH), BF16)
    return pl.pallas_call(
        functools.partial(_q_proj_kernel, pair_major=pair_major),
        grid=(rows // tm,),
        in_specs=[pl.BlockSpec((tm, D_MODEL), lambda i: (i, 0)),
                  pl.BlockSpec((1, mrows, 3 * D_MODEL), lambda i: (i // tpb, 0, 0)),
                  _const_spec((D_MODEL, 2 * FA_WIDTH)), _const_spec((1, 2 * FA_WIDTH))],
        out_specs=[q_spec, pl.BlockSpec((tm, FA_WIDTH), lambda i: (i, 0))],
        out_shape=[q_shape, jax.ShapeDtypeStruct((rows, FA_WIDTH), F32)],
        compiler_params=_params(1),
        name="q_proj_prompt" if pair_major else "q_proj_sample",
    )(x, mod, w_in.astype(BF16), b_in[None])


def _out_proj_kernel(o_ref, gz_ref, x_ref, mod_ref, w_ref, b_ref, g_ref, bb_ref, y_ref):
    t = (o_ref[...] * gz_ref[...]).astype(BF16)
    y = _dot(t, w_ref[...]) + b_ref[...]
    gate = mod_ref[0][:, 2 * D_MODEL:]
    r = DEEPNORM_ALPHA * x_ref[...] + (1.0 + gate) * y
    y_ref[...] = _layer_norm(r, g_ref[...], bb_ref[...])


def _out_proj(o, gz, x, mod, w_out, b_out, ln_g, ln_b, *, tm):
    rows = x.shape[0]
    nb, mrows = mod.shape[0], mod.shape[1]
    tpb = rows // nb // tm
    row = lambda i: (i, 0)
    return pl.pallas_call(
        _out_proj_kernel,
        grid=(rows // tm,),
        in_specs=[pl.BlockSpec((tm, FA_WIDTH), row), pl.BlockSpec((tm, FA_WIDTH), row),
                  pl.BlockSpec((tm, D_MODEL), row),
                  pl.BlockSpec((1, mrows, 3 * D_MODEL), lambda i: (i // tpb, 0, 0)),
                  _const_spec((FA_WIDTH, D_MODEL)), _const_spec((1, D_MODEL)),
                  _const_spec((1, D_MODEL)), _const_spec((1, D_MODEL))],
        out_specs=pl.BlockSpec((tm, D_MODEL), row),
        out_shape=jax.ShapeDtypeStruct((rows, D_MODEL), F32),
        compiler_params=_params(1),
        name="out_proj",
    )(o, gz, x, mod, w_out.astype(BF16), b_out[None], ln_g[None], ln_b[None])


def _flash_kernel(q_ref, k_ref, v_ref, c_ref, o_ref, s_scr, *head_scr, tile, n_tiles):
    p_scr, m_scr, mnew_scr, alpha_scr, acc_scr = (head_scr[0:2], head_scr[2:4], head_scr[4:6],
                                                  head_scr[6:8], head_scr[8:10])
    lane = lax.broadcasted_iota(jnp.int32, (1, LANES), 1)
    first = lane < FA_HEAD_DIM

    def softmax_blocks(a, c_row, masked):
        for r0 in range(0, tile, FLASH_ROWS):
            rows = slice(r0, r0 + FLASH_ROWS)
            ncols = min(tile, -(-(r0 + FLASH_ROWS) // LANES) * LANES) if masked else tile
            s = s_scr[a * tile + r0:a * tile + r0 + FLASH_ROWS, :ncols] - c_row[:, :ncols]
            if masked:
                r_i = r0 + lax.broadcasted_iota(jnp.int32, (FLASH_ROWS, ncols), 0)
                c_i = lax.broadcasted_iota(jnp.int32, (FLASH_ROWS, ncols), 1)
                s = jnp.where(c_i <= r_i, s, NEG_BIG)
            m_old = m_scr[a][rows, :]
            m_new = jnp.maximum(m_old, jnp.max(s, axis=-1, keepdims=True))
            mnew_scr[a][rows, :] = m_new
            alpha_scr[a][rows, :] = jnp.exp2(m_old - m_new)
            p_scr[a][rows, :ncols] = jnp.exp2(s - m_new).astype(BF16)
            if ncols < tile:
                p_scr[a][rows, ncols:] = jnp.zeros((FLASH_ROWS, tile - ncols), BF16)

    def kv_tile(t, q_stack, masked):
        start = pl.multiple_of(t * tile, tile)
        k2 = k_ref[0, 0, pl.ds(start, tile), :]
        v2 = v_ref[0, 0, pl.ds(start, tile), :]
        one = jnp.ones_like(v2)
        v_heads = (jnp.where(first, v2, one), jnp.where(first, one, v2))
        c2 = c_ref[0, 0, t]
        s_scr[...] = _dot_nt(q_stack, k2)
        for a in range(2):
            softmax_blocks(a, c2[a:a + 1, :], masked)
            acc_scr[a][...] = alpha_scr[a][...] * acc_scr[a][...] + _dot(p_scr[a][...], v_heads[a])
        for a in range(2):
            m_scr[a][...] = mnew_scr[a][...]

    def q_body(i, _):
        q2 = q_ref[0, 0, pl.ds(pl.multiple_of(i * tile, tile), tile), :]
        zero = jnp.zeros_like(q2)
        q_stack = jnp.concatenate([jnp.where(first, q2, zero), jnp.where(first, zero, q2)], axis=0)
        for a in range(2):
            m_scr[a][...] = jnp.full(m_scr[a].shape, NEG_BIG, F32)
            acc_scr[a][...] = jnp.zeros(acc_scr[a].shape, F32)

        def off_diagonal(t, _):
            kv_tile(t, q_stack, False)
            return 0

        lax.fori_loop(0, i, off_diagonal, 0)
        kv_tile(i, q_stack, True)
        acc_a, acc_b = acc_scr[0][...], acc_scr[1][...]
        o_a = acc_a / pltpu.roll(acc_a, FA_HEAD_DIM, axis=1)
        o_b = acc_b / pltpu.roll(acc_b, FA_HEAD_DIM, axis=1)
        o_ref[0, pl.ds(pl.multiple_of(i * tile, tile), tile), :] = jnp.where(first, o_a, o_b)
        return 0

    lax.fori_loop(0, n_tiles, q_body, 0)


def _flash_attention(q_pm, k_pm, v_pm, c_t, batch, seq):
    tile = ATT_TILE
    nt = seq // tile
    pm = lambda b, j: (b, j, 0, 0)
    return pl.pallas_call(
        functools.partial(_flash_kernel, tile=tile, n_tiles=nt),
        grid=(batch, HEAD_PAIRS),
        in_specs=[pl.BlockSpec((1, 1, seq, LANES), pm), pl.BlockSpec((1, 1, seq, LANES), pm),
                  pl.BlockSpec((1, 1, seq, LANES), pm),
                  pl.BlockSpec((1, 1, nt, 2, tile), lambda b, j: (b, j, 0, 0, 0))],
        out_specs=pl.BlockSpec((1, seq, LANES), lambda b, j: (b, 0, j)),
        out_shape=jax.ShapeDtypeStruct((batch, seq, FA_WIDTH), F32),
        scratch_shapes=([pltpu.VMEM((2 * tile, tile), F32)]
                        + [pltpu.VMEM((tile, tile), BF16)] * 2
                        + [pltpu.VMEM((tile, 1), F32)] * 6
                        + [pltpu.VMEM((tile, LANES), F32)] * 2),
        compiler_params=_params(2),
        name="flash_prompt",
    )(q_pm, k_pm, v_pm, c_t)


def _decode_kernel(pt_ref, q_ref, kn_ref, vn_ref, lfn_ref, *refs, n_pages):
    del pt_ref
    k_refs, v_refs, lf_refs = refs[:n_pages], refs[n_pages:2 * n_pages], refs[2 * n_pages:3 * n_pages]
    o_ref = refs[3 * n_pages]
    head = lax.broadcasted_iota(jnp.int32, (FA_HEADS, FA_WIDTH), 0)
    col = lax.broadcasted_iota(jnp.int32, (FA_HEADS, FA_WIDTH), 1)
    own = (col >= head * FA_HEAD_DIM) & (col < (head + 1) * FA_HEAD_DIM)
    q_rows = jnp.where(own, jnp.broadcast_to(q_ref[0].astype(F32), (FA_HEADS, FA_WIDTH)), 0.0)
    q_bf = q_rows.astype(BF16)

    r_i = lax.broadcasted_iota(jnp.int32, (PAGE_SIZE, PAGE_SIZE), 0)
    c_i = lax.broadcasted_iota(jnp.int32, (PAGE_SIZE, PAGE_SIZE), 1)
    upper = jnp.where(r_i <= c_i, 1.0, 0.0).astype(BF16)
    lf_all = jnp.concatenate([lf_refs[i][0] for i in range(n_pages)], axis=0)
    cum_all = _split3_dot(lf_all, upper)
    cums, offs = [], []
    off = jnp.zeros((FA_HEADS, 1), F32)
    for i in range(n_pages):
        cum_i = cum_all[i * FA_HEADS:(i + 1) * FA_HEADS]
        cums.append(cum_i)
        offs.append(off)
        off = off + cum_i[:, PAGE_SIZE - 1:PAGE_SIZE]
    c_query = off + lfn_ref[0]

    s_parts = []
    for i in range(n_pages):
        s_i = _dot(q_bf, k_refs[i][0].astype(BF16))
        s_parts.append(s_i + (c_query - (offs[i] + cums[i])))
    s = jnp.concatenate(s_parts, axis=1)
    s_new = jnp.sum(q_rows * kn_ref[0], axis=1, keepdims=True)
    m = jnp.maximum(jnp.max(s, axis=1, keepdims=True), s_new)
    p = jnp.exp(s - m)
    p_new = jnp.exp(s_new - m)
    l = jnp.sum(p, axis=1, keepdims=True) + p_new
    pb = p.astype(BF16)
    acc = p_new * vn_ref[0]
    for i in range(n_pages):
        acc = acc + _dot_nt(pb[:, i * PAGE_SIZE:(i + 1) * PAGE_SIZE], v_refs[i][0].astype(BF16))
    acc = acc / l
    o_ref[0] = jnp.sum(jnp.where(own, acc, 0.0), axis=0, keepdims=True)


def _decode_attention(q, k_new, v_new, lf_new, cache_kt, cache_vt, cache_lft, page_table):
    n, n_pages = page_table.shape

    def page_map(i):
        return lambda b, pt: (pt[b, i], 0, 0)

    row = lambda b, pt: (b, 0, 0)
    in_specs = [pl.BlockSpec((1, 1, FA_WIDTH), row), pl.BlockSpec((1, 1, FA_WIDTH), row),
                pl.BlockSpec((1, 1, FA_WIDTH), row), pl.BlockSpec((1, FA_HEADS, 1), row)]
    in_specs += [pl.BlockSpec((1, FA_WIDTH, PAGE_SIZE), page_map(i)) for i in range(n_pages)]
    in_specs += [pl.BlockSpec((1, FA_WIDTH, PAGE_SIZE), page_map(i)) for i in range(n_pages)]
    in_specs += [pl.BlockSpec((1, FA_HEADS, PAGE_SIZE), page_map(i)) for i in range(n_pages)]
    out = pl.pallas_call(
        functools.partial(_decode_kernel, n_pages=n_pages),
        grid_spec=pltpu.PrefetchScalarGridSpec(
            num_scalar_prefetch=1, grid=(n,), in_specs=in_specs,
            out_specs=pl.BlockSpec((1, 1, FA_WIDTH), row)),
        out_shape=jax.ShapeDtypeStruct((n, 1, FA_WIDTH), F32),
        compiler_params=_params(1),
        name="decode_attention",
    )(page_table, q.reshape(n, 1, FA_WIDTH), k_new.reshape(n, 1, FA_WIDTH), v_new.reshape(n, 1, FA_WIDTH),
      lf_new.reshape(n, FA_HEADS, 1), *([cache_kt] * n_pages), *([cache_vt] * n_pages), *([cache_lft] * n_pages))
    return out.reshape(n, FA_WIDTH)


def kernel(x_prompt, x_sample, c_prompt, c_sample, cache_k, cache_v, cache_logf, page_table, w_ada, b_ada, ln_g, ln_b, cm_w_in, cm_b_in, cm_ln_v_g, cm_ln_v_b, cm_w_s, cm_b_s, cm_w_out, cm_b_out, kv_w, kv_b, fa_w_in, fa_b_in, fa_w_out, fa_b_out):
    batch, seq, _ = x_prompt.shape
    n_dec = x_sample.shape[0]

    n_c = batch + n_dec
    c_all = jnp.pad(jnp.concatenate([c_prompt, c_sample], axis=0), ((0, (-n_c) % 8), (0, 0)))
    mod = _ada_modulation(c_all, w_ada, b_ada)
    mod_p = mod[:, :batch].reshape(DEPTH, batch, 1, 3 * D_MODEL)
    mod_s = mod[:, batch:n_c].reshape(DEPTH, 1, n_dec, 3 * D_MODEL)

    xp = x_prompt.reshape(batch * seq, D_MODEL)
    xs = x_sample.reshape(n_dec, D_MODEL)
    chunk_rows = []
    for layer in range(N_A_LAYERS):
        args = (cm_w_in[layer], cm_b_in[layer], cm_ln_v_g[layer], cm_ln_v_b[layer], cm_w_s[layer], cm_b_s[layer],
                cm_w_out[layer], cm_b_out[layer], ln_g[layer], ln_b[layer])
        (xp,) = _chunk_mlp_layer(xp, mod_p[layer], *args, single_row_chunks=False)
        xs, vn_s = _chunk_mlp_layer(xs, mod_s[layer], *args, single_row_chunks=True)
        chunk_rows.append(vn_s)

    kt_p, vt_p, lft_p, kb_p, vb_p, ct_p = _kv_prompt(xp, kv_w, kv_b, batch, seq)
    kt_s, vt_s, lft_s, k_s, v_s, lf_s = _kv_sample(xs, kv_w, kv_b)

    n_phys = cache_k.shape[0]
    cache_kt = jnp.transpose(cache_k, (0, 2, 3, 1)).reshape(n_phys, FA_WIDTH, PAGE_SIZE)
    cache_vt = jnp.transpose(cache_v, (0, 2, 3, 1)).reshape(n_phys, FA_WIDTH, PAGE_SIZE)
    cache_lft = jnp.swapaxes(cache_logf, 1, 2)

    for j in range(DEPTH - N_A_LAYERS):
        layer = N_A_LAYERS + j
        q_pm, gz_p = _q_proj(xp, mod_p[layer], fa_w_in[j], fa_b_in[j], batch=batch, seq=seq)
        o_p = _flash_attention(q_pm, kb_p, vb_p, ct_p, batch, seq).reshape(batch * seq, FA_WIDTH)
        xp = _out_proj(o_p, gz_p, xp, mod_p[layer], fa_w_out[j], fa_b_out[j], ln_g[layer], ln_b[layer], tm=ROW_TILE)

        q_s, gz_s = _q_proj(xs, mod_s[layer], fa_w_in[j], fa_b_in[j])
        o_s = _decode_attention(q_s, k_s, v_s, lf_s, cache_kt, cache_vt, cache_lft, page_table)
        xs = _out_proj(o_s, gz_s, xs, mod_s[layer], fa_w_out[j], fa_b_out[j], ln_g[layer], ln_b[layer], tm=n_dec)

    y_prompt = xp.reshape(batch, seq, D_MODEL)
    y_sample = xs.reshape(n_dec, 1, D_MODEL)
    k_prompt = jnp.transpose(kt_p.reshape(batch, FA_HEADS, FA_HEAD_DIM, seq), (0, 3, 1, 2))
    v_prompt = jnp.transpose(vt_p.reshape(batch, FA_HEADS, FA_HEAD_DIM, seq), (0, 3, 1, 2))
    logf_prompt = jnp.transpose(lft_p, (0, 2, 1))
    k_sample = jnp.transpose(kt_s.reshape(1, FA_HEADS, FA_HEAD_DIM, n_dec), (3, 0, 1, 2))
    v_sample = jnp.transpose(vt_s.reshape(1, FA_HEADS, FA_HEAD_DIM, n_dec), (3, 0, 1, 2))
    logf_sample = jnp.transpose(lft_s, (2, 0, 1))
    chunk_v_sample = jnp.stack(chunk_rows).reshape(N_A_LAYERS, n_dec, 1, CM_WIDTH)
    return (y_prompt, y_sample, k_prompt, v_prompt, logf_prompt, k_sample, v_sample, logf_sample, chunk_v_sample)
```

```python
import functools

import jax
import jax.numpy as jnp
from jax import lax
from jax.experimental import pallas as pl
from jax.experimental.pallas import tpu as pltpu

F32 = jnp.float32
BF16 = jnp.bfloat16

D_MODEL = 1024
DEPTH = 4
N_A_LAYERS = 2
CHUNK = 128
CM_WIDTH = 2048
CM_GROUPS = 8
CM_GROUP_DIM = CM_WIDTH // CM_GROUPS
FA_HEADS = 16
FA_HEAD_DIM = 64
FA_WIDTH = FA_HEADS * FA_HEAD_DIM
HEAD_PAIRS = FA_HEADS // 2
PAGE_SIZE = 128
DEEPNORM_ALPHA = (2.0 * DEPTH) ** 0.25
LN_EPS = 1e-5
LANES = 128
NEG_BIG = -1e30
VMEM_LIMIT = 56 * 1024 * 1024

ROW_TILE = 512
ATT_TILE = 512
FLASH_ROWS = 32
LOG2E = 1.4426950408889634


def _params(n_axes, vmem=VMEM_LIMIT):
    return pltpu.CompilerParams(dimension_semantics=("arbitrary",) * n_axes, vmem_limit_bytes=vmem)


def _dot(a, b):
    return jnp.dot(a, b, preferred_element_type=F32)


def _dot_nt(a, b):
    return lax.dot_general(a, b, (((1,), (1,)), ((), ())), preferred_element_type=F32)


def _layer_norm(x, g, b):
    mu = jnp.mean(x, axis=-1, keepdims=True)
    d = x - mu
    var = jnp.mean(d * d, axis=-1, keepdims=True)
    return d * lax.rsqrt(var + LN_EPS) * g + b


def _log_sigmoid(x):
    return -(jnp.maximum(-x, 0.0) + jnp.log1p(jnp.exp(-jnp.abs(x))))


def _split3_dot(x, w):
    hi = x.astype(BF16)
    r1 = x - hi.astype(F32)
    mid = r1.astype(BF16)
    lo = (r1 - mid.astype(F32)).astype(BF16)
    return _dot(hi, w) + _dot(mid, w) + _dot(lo, w)


def _const_spec(shape):
    nd = len(shape)
    return pl.BlockSpec(shape, lambda *_: (0,) * nd, pipeline_mode=pl.Buffered(1))


def _ada_kernel(c_ref, w_ref, b_ref, o_ref):
    a = jax.nn.silu(c_ref[...]).astype(BF16)
    o_ref[0] = _dot(a, w_ref[0].astype(BF16)) + b_ref[0]


def _ada_modulation(c_all, w_ada, b_ada):
    n = c_all.shape[0]
    nt = 3 * D_MODEL // D_MODEL
    return pl.pallas_call(
        _ada_kernel,
        grid=(DEPTH, nt),
        in_specs=[pl.BlockSpec((n, D_MODEL), lambda l, j: (0, 0)),
                  pl.BlockSpec((1, D_MODEL, D_MODEL), lambda l, j: (l, 0, j)),
                  pl.BlockSpec((1, 1, D_MODEL), lambda l, j: (l, 0, j))],
        out_specs=pl.BlockSpec((1, n, D_MODEL), lambda l, j: (l, 0, j)),
        out_shape=jax.ShapeDtypeStruct((DEPTH, n, 3 * D_MODEL), F32),
        compiler_params=_params(2),
        name="ada_modulation",
    )(c_all, w_ada, b_ada.reshape(DEPTH, 1, 3 * D_MODEL))


def _chunk_mlp_kernel(x_ref, mod_ref, win_ref, bin_ref, lvg_ref, lvb_ref, ws_ref, bs_ref,
                      wout_ref, bout_ref, lng_ref, lnb_ref, *rest, single_row_chunks):
    if single_row_chunks:
        o_ref, vn_ref, t_scr = rest
    else:
        o_ref, t_scr = rest
    rows = x_ref.shape[0]
    x = x_ref[...]
    m = mod_ref[0]
    shift, scale, gate = m[:, :D_MODEL], m[:, D_MODEL:2 * D_MODEL], m[:, 2 * D_MODEL:]
    h = (x * (1.0 + scale) + shift).astype(BF16)

    v = jax.nn.gelu(_dot(h, win_ref[:, CM_WIDTH:2 * CM_WIDTH]) + bin_ref[:, CM_WIDTH:2 * CM_WIDTH])
    vn = _layer_norm(v, lvg_ref[...], lvb_ref[...])
    if single_row_chunks:
        vn_ref[...] = vn
    else:
        r_i = lax.broadcasted_iota(jnp.int32, (CHUNK, CHUNK), 0)
        c_i = lax.broadcasted_iota(jnp.int32, (CHUNK, CHUNK), 1)
        causal = c_i <= r_i

    for g in range(CM_GROUPS):
        lo, hi = g * CM_GROUP_DIM, (g + 1) * CM_GROUP_DIM
        vn_g = vn[:, lo:hi]
        if single_row_chunks:
            sg = vn_g * ws_ref[:, lo:hi] + bs_ref[:, lo:hi]
        else:
            w_g = jnp.where(causal, ws_ref[g], 0.0).astype(BF16)
            b_g = bs_ref[:, g:g + 1]
            vb = vn_g.astype(BF16)
            sg = jnp.concatenate(
                [_dot(w_g, vb[c * CHUNK:(c + 1) * CHUNK]) + b_g for c in range(rows // CHUNK)], axis=0)
        u_g = jax.nn.gelu(_dot(h, win_ref[:, lo:hi]) + bin_ref[:, lo:hi])
        z_g = _dot(h, win_ref[:, 2 * CM_WIDTH + lo:2 * CM_WIDTH + hi]) + bin_ref[:, 2 * CM_WIDTH + lo:2 * CM_WIDTH + hi]
        t_scr[:, lo:hi] = (u_g * sg * jax.nn.silu(z_g)).astype(BF16)

    y = _dot(t_scr[...], wout_ref[...]) + bout_ref[...]
    r = DEEPNORM_ALPHA * x + (1.0 + gate) * y
    o_ref[...] = _layer_norm(r, lng_ref[...], lnb_ref[...])


def _chunk_mlp_layer(x, mod, w_in, b_in, lvg, lvb, w_s, b_s, w_out, b_out, ln_g, ln_b, *, single_row_chunks):
    rows = x.shape[0]
    nb, mrows = mod.shape[0], mod.shape[1]
    tm = rows if single_row_chunks else ROW_TILE
    tiles_per_batch = rows // nb // tm
    if single_row_chunks:
        ws_arg = jnp.repeat(w_s[:, 0, 0], CM_GROUP_DIM)[None]
        bs_arg = jnp.repeat(b_s[:, 0], CM_GROUP_DIM)[None]
    else:
        ws_arg, bs_arg = w_s, b_s.T
    out_shape = [jax.ShapeDtypeStruct((rows, D_MODEL), F32)]
    out_specs = [pl.BlockSpec((tm, D_MODEL), lambda i: (i, 0))]
    if single_row_chunks:
        out_shape.append(jax.ShapeDtypeStruct((rows, CM_WIDTH), F32))
        out_specs.append(pl.BlockSpec((tm, CM_WIDTH), lambda i: (i, 0)))
    res = pl.pallas_call(
        functools.partial(_chunk_mlp_kernel, single_row_chunks=single_row_chunks),
        grid=(rows // tm,),
        in_specs=[pl.BlockSpec((tm, D_MODEL), lambda i: (i, 0)),
                  pl.BlockSpec((1, mrows, 3 * D_MODEL), lambda i: (i // tiles_per_batch, 0, 0)),
                  _const_spec((D_MODEL, 3 * CM_WIDTH)), _const_spec((1, 3 * CM_WIDTH)),
                  _const_spec((1, CM_WIDTH)), _const_spec((1, CM_WIDTH)),
                  _const_spec(ws_arg.shape), _const_spec(bs_arg.shape),
                  _const_spec((CM_WIDTH, D_MODEL)), _const_spec((1, D_MODEL)),
                  _const_spec((1, D_MODEL)), _const_spec((1, D_MODEL))],
        out_specs=out_specs,
        out_shape=out_shape,
        scratch_shapes=[pltpu.VMEM((tm, CM_WIDTH), BF16)],
        compiler_params=_params(1),
        name="chunk_mlp_sample" if single_row_chunks else "chunk_mlp_prompt",
    )(x, mod, w_in.astype(BF16), b_in[None], lvg[None], lvb[None], ws_arg, bs_arg,
      w_out.astype(BF16), b_out[None], ln_g[None], ln_b[None])
    return res


def _kv_kernel(x_ref, wkv_ref, bkv_ref, wkvt_ref, bkvt_ref, wft_ref, bft_ref, *rest, prompt):
    if prompt:
        kt_ref, vt_ref, lft_ref, kb_ref, vb_ref, ct_ref, carry_scr = rest
    else:
        wf_ref, bf_ref, kt_ref, vt_ref, lft_ref, k_ref, v_ref, lf_ref = rest
    rows = x_ref.shape[0]
    xb = x_ref[...].astype(BF16)
    kvt = _dot_nt(wkvt_ref[...], xb) + bkvt_ref[...]
    kt_ref[0] = kvt[:FA_WIDTH]
    vt_ref[0] = kvt[FA_WIDTH:]
    lft = _log_sigmoid(_dot_nt(wft_ref[...], xb) + bft_ref[...])
    lft_ref[0] = lft
    if not prompt:
        kv = _dot(xb, wkv_ref[...]) + bkv_ref[...]
        k_ref[...] = kv[:, :FA_WIDTH]
        v_ref[...] = kv[:, FA_WIDTH:]
        lf_ref[...] = _log_sigmoid(_dot(xb, wf_ref[...]) + bf_ref[...])[:, :FA_HEADS]
    else:
        ktb = kvt[:FA_WIDTH].astype(BF16)
        vb = (_dot(xb, wkv_ref[...]) + bkv_ref[...]).astype(BF16)
        for j in range(HEAD_PAIRS):
            kb_ref[0, j, 0] = ktb[j * LANES:(j + 1) * LANES, :]
            vb_ref[0, j] = vb[:, j * LANES:(j + 1) * LANES]
        r_i = lax.broadcasted_iota(jnp.int32, (rows, rows), 0)
        c_i = lax.broadcasted_iota(jnp.int32, (rows, rows), 1)
        upper = jnp.where(r_i <= c_i, 1.0, 0.0).astype(BF16)

        @pl.when(pl.program_id(1) == 0)
        def _():
            carry_scr[...] = jnp.zeros_like(carry_scr)

        ct = _split3_dot(lft, upper) + carry_scr[:, 0:1]
        carry_scr[...] = jnp.broadcast_to(ct[:, rows - 1:rows], carry_scr.shape)
        ct2 = ct * LOG2E
        for j in range(HEAD_PAIRS):
            ct_ref[0, j, 0] = ct2[2 * j:2 * j + 2, :]


def _kv_weights(kv_w, kv_b):
    wkv = kv_w[:, :2 * FA_WIDTH].astype(BF16)
    bkv = kv_b[None, :2 * FA_WIDTH]
    wkvt = kv_w[:, :2 * FA_WIDTH].T.astype(BF16)
    bkvt = kv_b[:2 * FA_WIDTH, None]
    wft = kv_w[:, 2 * FA_WIDTH:].T.astype(BF16)
    bft = kv_b[2 * FA_WIDTH:, None]
    return wkv, bkv, wkvt, bkvt, wft, bft


def _kv_prompt(x, kv_w, kv_b, batch, seq):
    tm = ATT_TILE
    nt = seq // tm
    wkv, bkv, wkvt, bkvt, wft, bft = _kv_weights(kv_w, kv_b)
    weights = (wkv[:, FA_WIDTH:], bkv[:, FA_WIDTH:], wkvt, bkvt, wft, bft)
    return pl.pallas_call(
        functools.partial(_kv_kernel, prompt=True),
        grid=(batch, nt),
        in_specs=[pl.BlockSpec((tm, D_MODEL), lambda b, i: (b * nt + i, 0))] + [_const_spec(w.shape) for w in weights],
        out_specs=[pl.BlockSpec((1, FA_WIDTH, tm), lambda b, i: (b, 0, i)),
                   pl.BlockSpec((1, FA_WIDTH, tm), lambda b, i: (b, 0, i)),
                   pl.BlockSpec((1, FA_HEADS, tm), lambda b, i: (b, 0, i)),
                   pl.BlockSpec((1, HEAD_PAIRS, 1, LANES, tm), lambda b, i: (b, 0, i, 0, 0)),
                   pl.BlockSpec((1, HEAD_PAIRS, tm, LANES), lambda b, i: (b, 0, i, 0)),
                   pl.BlockSpec((1, HEAD_PAIRS, 1, 2, tm), lambda b, i: (b, 0, i, 0, 0))],
        out_shape=[jax.ShapeDtypeStruct((batch, FA_WIDTH, seq), F32),
                   jax.ShapeDtypeStruct((batch, FA_WIDTH, seq), F32),
                   jax.ShapeDtypeStruct((batch, FA_HEADS, seq), F32),
                   jax.ShapeDtypeStruct((batch, HEAD_PAIRS, nt, LANES, tm), BF16),
                   jax.ShapeDtypeStruct((batch, HEAD_PAIRS, seq, LANES), BF16),
                   jax.ShapeDtypeStruct((batch, HEAD_PAIRS, nt, 2, tm), F32)],
        scratch_shapes=[pltpu.VMEM((FA_HEADS, LANES), F32)],
        compiler_params=_params(2),
        name="kv_prompt",
    )(x, *weights)


def _kv_sample(x, kv_w, kv_b):
    rows = x.shape[0]
    weights = _kv_weights(kv_w, kv_b)
    wf = jnp.pad(kv_w[:, 2 * FA_WIDTH:], ((0, 0), (0, LANES - FA_HEADS))).astype(BF16)
    bf = jnp.pad(kv_b[2 * FA_WIDTH:], (0, LANES - FA_HEADS))[None]
    weights = weights + (wf, bf)
    full = lambda i: (0, 0)
    full3 = lambda i: (0, 0, 0)
    return pl.pallas_call(
        functools.partial(_kv_kernel, prompt=False),
        grid=(1,),
        in_specs=[pl.BlockSpec((rows, D_MODEL), full)] + [_const_spec(w.shape) for w in weights],
        out_specs=[pl.BlockSpec((1, FA_WIDTH, rows), full3), pl.BlockSpec((1, FA_WIDTH, rows), full3),
                   pl.BlockSpec((1, FA_HEADS, rows), full3),
                   pl.BlockSpec((rows, FA_WIDTH), full), pl.BlockSpec((rows, FA_WIDTH), full),
                   pl.BlockSpec((rows, FA_HEADS), full)],
        out_shape=[jax.ShapeDtypeStruct((1, FA_WIDTH, rows), F32), jax.ShapeDtypeStruct((1, FA_WIDTH, rows), F32),
                   jax.ShapeDtypeStruct((1, FA_HEADS, rows), F32),
                   jax.ShapeDtypeStruct((rows, FA_WIDTH), F32), jax.ShapeDtypeStruct((rows, FA_WIDTH), F32),
                   jax.ShapeDtypeStruct((rows, FA_HEADS), F32)],
        compiler_params=_params(1),
        name="kv_sample",
    )(x, *weights)


def _q_proj_kernel(x_ref, mod_ref, w_ref, b_ref, q_ref, gz_ref, *, pair_major):
    x = x_ref[...]
    m = mod_ref[0]
    h = (x * (1.0 + m[:, D_MODEL:2 * D_MODEL]) + m[:, :D_MODEL]).astype(BF16)
    p = _dot(h, w_ref[...]) + b_ref[...]
    if pair_major:
        q = (p[:, :FA_WIDTH] * (LOG2E * FA_HEAD_DIM ** -0.5)).astype(BF16)
        for j in range(HEAD_PAIRS):
            q_ref[0, j] = q[:, j * LANES:(j + 1) * LANES]
    else:
        q_ref[...] = (p[:, :FA_WIDTH] * (FA_HEAD_DIM ** -0.5)).astype(BF16)
    gz_ref[...] = jax.nn.silu(p[:, FA_WIDTH:])


def _q_proj(x, mod, w_in, b_in, *, batch=None, seq=None):
    rows = x.shape[0]
    nb, mrows = mod.shape[0], mod.shape[1]
    pair_major = batch is not None
    tm = ROW_TILE if pair_major else rows
    tpb = rows // nb // tm
    if pair_major:
        q_spec = pl.BlockSpec((1, HEAD_PAIRS, tm, LANES), lambda i: (i // tpb, 0, i % tpb, 0))
        q_shape = jax.ShapeDtypeStruct((batch, HEAD_PAIRS, seq, LANES), BF16)
    else:
        q_spec = pl.BlockSpec((tm, FA_WIDTH), lambda i: (i, 0))
        q_shape = jax.ShapeDtypeStruct((rows, FA_WIDTH), BF16)
    return pl.pallas_call(
        functools.partial(_q_proj_kernel, pair_major=pair_major),
        grid=(rows // tm,),
        in_specs=[pl.BlockSpec((tm, D_MODEL), lambda i: (i, 0)),
                  pl.BlockSpec((1, mrows, 3 * D_MODEL), lambda i: (i // tpb, 0, 0)),
                  _const_spec((D_MODEL, 2 * FA_WIDTH)), _const_spec((1, 2 * FA_WIDTH))],
        out_specs=[q_spec, pl.BlockSpec((tm, FA_WIDTH), lambda i: (i, 0))],
        out_shape=[q_shape, jax.ShapeDtypeStruct((rows, FA_WIDTH), F32)],
        compiler_params=_params(1),
        name="q_proj_prompt" if pair_major else "q_proj_sample",
    )(x, mod, w_in.astype(BF16), b_in[None])


def _out_proj_kernel(o_ref, gz_ref, x_ref, mod_ref, w_ref, b_ref, g_ref, bb_ref, y_ref):
    t = (o_ref[...] * gz_ref[...]).astype(BF16)
    y = _dot(t, w_ref[...]) + b_ref[...]
    gate = mod_ref[0][:, 2 * D_MODEL:]
    r = DEEPNORM_ALPHA * x_ref[...] + (1.0 + gate) * y
    y_ref[...] = _layer_norm(r, g_ref[...], bb_ref[...])


def _out_proj(o, gz, x, mod, w_out, b_out, ln_g, ln_b, *, tm):
    rows = x.shape[0]
    nb, mrows = mod.shape[0], mod.shape[1]
    tpb = rows // nb // tm
    row = lambda i: (i, 0)
    return pl.pallas_call(
        _out_proj_kernel,
        grid=(rows // tm,),
        in_specs=[pl.BlockSpec((tm, FA_WIDTH), row), pl.BlockSpec((tm, FA_WIDTH), row),
                  pl.BlockSpec((tm, D_MODEL), row),
                  pl.BlockSpec((1, mrows, 3 * D_MODEL), lambda i: (i // tpb, 0, 0)),
                  _const_spec((FA_WIDTH, D_MODEL)), _const_spec((1, D_MODEL)),
                  _const_spec((1, D_MODEL)), _const_spec((1, D_MODEL))],
        out_specs=pl.BlockSpec((tm, D_MODEL), row),
        out_shape=jax.ShapeDtypeStruct((rows, D_MODEL), F32),
        compiler_params=_params(1),
        name="out_proj",
    )(o, gz, x, mod, w_out.astype(BF16), b_out[None], ln_g[None], ln_b[None])


def _flash_kernel(q_ref, k_ref, v_ref, c_ref, o_ref, s_scr, *head_scr, tile, n_tiles):
    p_scr, m_scr, mnew_scr, alpha_scr, acc_scr = (head_scr[0:2], head_scr[2:4], head_scr[4:6],
                                                  head_scr[6:8], head_scr[8:10])
    lane = lax.broadcasted_iota(jnp.int32, (1, LANES), 1)
    first = lane < FA_HEAD_DIM

    def softmax_blocks(a, masked):
        for r0 in range(0, tile, FLASH_ROWS):
            rows = slice(r0, r0 + FLASH_ROWS)
            ncols = min(tile, -(-(r0 + FLASH_ROWS) // LANES) * LANES) if masked else tile
            s = s_scr[a * tile + r0:a * tile + r0 + FLASH_ROWS, :ncols]
            if masked:
                r_i = r0 + lax.broadcasted_iota(jnp.int32, (FLASH_ROWS, ncols), 0)
                c_i = lax.broadcasted_iota(jnp.int32, (FLASH_ROWS, ncols), 1)
                s = jnp.where(c_i <= r_i, s, NEG_BIG)
            m_old = m_scr[a][rows, :]
            m_new = jnp.maximum(m_old, jnp.max(s, axis=-1, keepdims=True))
            mnew_scr[a][rows, :] = m_new
            alpha_scr[a][rows, :] = jnp.exp2(m_old - m_new)
            p_scr[a][rows, :ncols] = jnp.exp2(s - m_new).astype(BF16)
            if ncols < tile:
                p_scr[a][rows, ncols:] = jnp.zeros((FLASH_ROWS, tile - ncols), BF16)

    sub8 = lax.broadcasted_iota(jnp.int32, (8, tile), 0)
    col_q = lax.broadcasted_iota(jnp.int32, (2 * tile, LANES), 1)
    row_q = lax.broadcasted_iota(jnp.int32, (2 * tile, LANES), 0)
    q_minus = jnp.where((col_q < 6) & ((col_q >= 3) == (row_q >= tile)), -1.0, 0.0).astype(BF16)

    def kv_tile(t, q_stack, masked):
        start = pl.multiple_of(t * tile, tile)
        v2 = v_ref[0, 0, pl.ds(start, tile), :]
        one = jnp.ones_like(v2)
        v_heads = (jnp.where(first, v2, one), jnp.where(first, one, v2))
        c2 = c_ref[0, 0, t]
        hi = c2.astype(BF16).astype(F32)
        mid = (c2 - hi).astype(BF16).astype(F32)
        lo = c2 - hi - mid
        pieces = (hi[0:1], mid[0:1], lo[0:1], hi[1:2], mid[1:2], lo[1:2])
        c_rows = jnp.zeros((8, tile), F32)
        for r, piece in enumerate(pieces):
            c_rows = jnp.where(sub8 == r, piece, c_rows)
        c_block = jnp.concatenate([c_rows, jnp.zeros((LANES - 8, tile), F32)], axis=0).astype(BF16)
        k_aug = jnp.concatenate([k_ref[0, 0, t], c_block], axis=0)
        s_scr[...] = _dot(q_stack, k_aug)
        for a in range(2):
            softmax_blocks(a, masked)
            acc_scr[a][...] = alpha_scr[a][...] * acc_scr[a][...] + _dot(p_scr[a][...], v_heads[a])
        for a in range(2):
            m_scr[a][...] = mnew_scr[a][...]

    def q_body(i, _):
        q2 = q_ref[0, 0, pl.ds(pl.multiple_of(i * tile, tile), tile), :]
        zero = jnp.zeros_like(q2)
        q_stack = jnp.concatenate([jnp.where(first, q2, zero), jnp.where(first, zero, q2)], axis=0)
        q_stack = jnp.concatenate([q_stack, q_minus], axis=1)
        for a in range(2):
            m_scr[a][...] = jnp.full(m_scr[a].shape, NEG_BIG, F32)
            acc_scr[a][...] = jnp.zeros(acc_scr[a].shape, F32)

        def off_diagonal(t, _):
            kv_tile(t, q_stack, False)
            return 0

        lax.fori_loop(0, i, off_diagonal, 0)
        kv_tile(i, q_stack, True)
        acc_a, acc_b = acc_scr[0][...], acc_scr[1][...]
        o_a = acc_a / pltpu.roll(acc_a, FA_HEAD_DIM, axis=1)
        o_b = acc_b / pltpu.roll(acc_b, FA_HEAD_DIM, axis=1)
        o_ref[0, pl.ds(pl.multiple_of(i * tile, tile), tile), :] = jnp.where(first, o_a, o_b)
        return 0

    lax.fori_loop(0, n_tiles, q_body, 0)


def _flash_attention(q_pm, k_pm, v_pm, c_t, batch, seq):
    tile = ATT_TILE
    nt = seq // tile
    pm = lambda b, j: (b, j, 0, 0)
    return pl.pallas_call(
        functools.partial(_flash_kernel, tile=tile, n_tiles=nt),
        grid=(batch, HEAD_PAIRS),
        in_specs=[pl.BlockSpec((1, 1, seq, LANES), pm),
                  pl.BlockSpec((1, 1, nt, LANES, tile), lambda b, j: (b, j, 0, 0, 0)),
                  pl.BlockSpec((1, 1, seq, LANES), pm),
                  pl.BlockSpec((1, 1, nt, 2, tile), lambda b, j: (b, j, 0, 0, 0))],
        out_specs=pl.BlockSpec((1, seq, LANES), lambda b, j: (b, 0, j)),
        out_shape=jax.ShapeDtypeStruct((batch, seq, FA_WIDTH), F32),
        scratch_shapes=([pltpu.VMEM((2 * tile, tile), F32)]
                        + [pltpu.VMEM((tile, tile), BF16)] * 2
                        + [pltpu.VMEM((tile, 1), F32)] * 6
                        + [pltpu.VMEM((tile, LANES), F32)] * 2),
        compiler_params=_params(2),
        name="flash_prompt",
    )(q_pm, k_pm, v_pm, c_t)


def _decode_kernel(pt_ref, q_ref, kn_ref, vn_ref, lfn_ref, *refs, n_pages):
    del pt_ref
    k_refs, v_refs, lf_refs = refs[:n_pages], refs[n_pages:2 * n_pages], refs[2 * n_pages:3 * n_pages]
    o_ref = refs[3 * n_pages]
    head = lax.broadcasted_iota(jnp.int32, (FA_HEADS, FA_WIDTH), 0)
    col = lax.broadcasted_iota(jnp.int32, (FA_HEADS, FA_WIDTH), 1)
    own = (col >= head * FA_HEAD_DIM) & (col < (head + 1) * FA_HEAD_DIM)
    q_rows = jnp.where(own, jnp.broadcast_to(q_ref[0].astype(F32), (FA_HEADS, FA_WIDTH)), 0.0)
    q_bf = q_rows.astype(BF16)

    r_i = lax.broadcasted_iota(jnp.int32, (PAGE_SIZE, PAGE_SIZE), 0)
    c_i = lax.broadcasted_iota(jnp.int32, (PAGE_SIZE, PAGE_SIZE), 1)
    upper = jnp.where(r_i <= c_i, 1.0, 0.0).astype(BF16)
    lf_all = jnp.concatenate([lf_refs[i][0] for i in range(n_pages)], axis=0)
    cum_all = _split3_dot(lf_all, upper)
    cums, offs = [], []
    off = jnp.zeros((FA_HEADS, 1), F32)
    for i in range(n_pages):
        cum_i = cum_all[i * FA_HEADS:(i + 1) * FA_HEADS]
        cums.append(cum_i)
        offs.append(off)
        off = off + cum_i[:, PAGE_SIZE - 1:PAGE_SIZE]
    c_query = off + lfn_ref[0]

    s_parts = []
    for i in range(n_pages):
        s_i = _dot(q_bf, k_refs[i][0].astype(BF16))
        s_parts.append(s_i + (c_query - (offs[i] + cums[i])))
    s = jnp.concatenate(s_parts, axis=1)
    s_new = jnp.sum(q_rows * kn_ref[0], axis=1, keepdims=True)
    m = jnp.maximum(jnp.max(s, axis=1, keepdims=True), s_new)
    p = jnp.exp(s - m)
    p_new = jnp.exp(s_new - m)
    l = jnp.sum(p, axis=1, keepdims=True) + p_new
    pb = p.astype(BF16)
    acc = p_new * vn_ref[0]
    for i in range(n_pages):
        acc = acc + _dot_nt(pb[:, i * PAGE_SIZE:(i + 1) * PAGE_SIZE], v_refs[i][0].astype(BF16))
    acc = acc / l
    o_ref[0] = jnp.sum(jnp.where(own, acc, 0.0), axis=0, keepdims=True)


def _decode_attention(q, k_new, v_new, lf_new, cache_kt, cache_vt, cache_lft, page_table):
    n, n_pages = page_table.shape

    def page_map(i):
        return lambda b, pt: (pt[b, i], 0, 0)

    row = lambda b, pt: (b, 0, 0)
    in_specs = [pl.BlockSpec((1, 1, FA_WIDTH), row), pl.BlockSpec((1, 1, FA_WIDTH), row),
                pl.BlockSpec((1, 1, FA_WIDTH), row), pl.BlockSpec((1, FA_HEADS, 1), row)]
    in_specs += [pl.BlockSpec((1, FA_WIDTH, PAGE_SIZE), page_map(i)) for i in range(n_pages)]
    in_specs += [pl.BlockSpec((1, FA_WIDTH, PAGE_SIZE), page_map(i)) for i in range(n_pages)]
    in_specs += [pl.BlockSpec((1, FA_HEADS, PAGE_SIZE), page_map(i)) for i in range(n_pages)]
    out = pl.pallas_call(
        functools.partial(_decode_kernel, n_pages=n_pages),
        grid_spec=pltpu.PrefetchScalarGridSpec(
            num_scalar_prefetch=1, grid=(n,), in_specs=in_specs,
            out_specs=pl.BlockSpec((1, 1, FA_WIDTH), row)),
        out_shape=jax.ShapeDtypeStruct((n, 1, FA_WIDTH), F32),
        compiler_params=_params(1),
        name="decode_attention",
    )(page_table, q.reshape(n, 1, FA_WIDTH), k_new.reshape(n, 1, FA_WIDTH), v_new.reshape(n, 1, FA_WIDTH),
      lf_new.reshape(n, FA_HEADS, 1), *([cache_kt] * n_pages), *([cache_vt] * n_pages), *([cache_lft] * n_pages))
    return out.reshape(n, FA_WIDTH)


def kernel(x_prompt, x_sample, c_prompt, c_sample, cache_k, cache_v, cache_logf, page_table, w_ada, b_ada, ln_g, ln_b, cm_w_in, cm_b_in, cm_ln_v_g, cm_ln_v_b, cm_w_s, cm_b_s, cm_w_out, cm_b_out, kv_w, kv_b, fa_w_in, fa_b_in, fa_w_out, fa_b_out):
    batch, seq, _ = x_prompt.shape
    n_dec = x_sample.shape[0]

    n_c = batch + n_dec
    c_all = jnp.pad(jnp.concatenate([c_prompt, c_sample], axis=0), ((0, (-n_c) % 8), (0, 0)))
    mod = _ada_modulation(c_all, w_ada, b_ada)
    mod_p = mod[:, :batch].reshape(DEPTH, batch, 1, 3 * D_MODEL)
    mod_s = mod[:, batch:n_c].reshape(DEPTH, 1, n_dec, 3 * D_MODEL)

    xp = x_prompt.reshape(batch * seq, D_MODEL)
    xs = x_sample.reshape(n_dec, D_MODEL)
    chunk_rows = []
    for layer in range(N_A_LAYERS):
        args = (cm_w_in[layer], cm_b_in[layer], cm_ln_v_g[layer], cm_ln_v_b[layer], cm_w_s[layer], cm_b_s[layer],
                cm_w_out[layer], cm_b_out[layer], ln_g[layer], ln_b[layer])
        (xp,) = _chunk_mlp_layer(xp, mod_p[layer], *args, single_row_chunks=False)
        xs, vn_s = _chunk_mlp_layer(xs, mod_s[layer], *args, single_row_chunks=True)
        chunk_rows.append(vn_s)

    kt_p, vt_p, lft_p, kb_p, vb_p, ct_p = _kv_prompt(xp, kv_w, kv_b, batch, seq)
    kt_s, vt_s, lft_s, k_s, v_s, lf_s = _kv_sample(xs, kv_w, kv_b)

    n_phys = cache_k.shape[0]
    cache_kt = jnp.transpose(cache_k, (0, 2, 3, 1)).reshape(n_phys, FA_WIDTH, PAGE_SIZE)
    cache_vt = jnp.transpose(cache_v, (0, 2, 3, 1)).reshape(n_phys, FA_WIDTH, PAGE_SIZE)
    cache_lft = jnp.swapaxes(cache_logf, 1, 2)

    for j in range(DEPTH - N_A_LAYERS):
        layer = N_A_LAYERS + j
        q_pm, gz_p = _q_proj(xp, mod_p[layer], fa_w_in[j], fa_b_in[j], batch=batch, seq=seq)
        o_p = _flash_attention(q_pm, kb_p, vb_p, ct_p, batch, seq).reshape(batch * seq, FA_WIDTH)
        xp = _out_proj(o_p, gz_p, xp, mod_p[layer], fa_w_out[j], fa_b_out[j], ln_g[layer], ln_b[layer], tm=ROW_TILE)

        q_s, gz_s = _q_proj(xs, mod_s[layer], fa_w_in[j], fa_b_in[j])
        o_s = _decode_attention(q_s, k_s, v_s, lf_s, cache_kt, cache_vt, cache_lft, page_table)
        xs = _out_proj(o_s, gz_s, xs, mod_s[layer], fa_w_out[j], fa_b_out[j], ln_g[layer], ln_b[layer], tm=n_dec)

    y_prompt = xp.reshape(batch, seq, D_MODEL)
    y_sample = xs.reshape(n_dec, 1, D_MODEL)
    k_prompt = jnp.transpose(kt_p.reshape(batch, FA_HEADS, FA_HEAD_DIM, seq), (0, 3, 1, 2))
    v_prompt = jnp.transpose(vt_p.reshape(batch, FA_HEADS, FA_HEAD_DIM, seq), (0, 3, 1, 2))
    logf_prompt = jnp.transpose(lft_p, (0, 2, 1))
    k_sample = jnp.transpose(kt_s.reshape(1, FA_HEADS, FA_HEAD_DIM, n_dec), (3, 0, 1, 2))
    v_sample = jnp.transpose(vt_s.reshape(1, FA_HEADS, FA_HEAD_DIM, n_dec), (3, 0, 1, 2))
    logf_sample = jnp.transpose(lft_s, (2, 0, 1))
    chunk_v_sample = jnp.stack(chunk_rows).reshape(N_A_LAYERS, n_dec, 1, CM_WIDTH)
    return (y_prompt, y_sample, k_prompt, v_prompt, logf_prompt, k_sample, v_sample, logf_sample, chunk_v_sample)
```

```python
import functools

import jax
import jax.numpy as jnp
from jax import lax
from jax.experimental import pallas as pl
from jax.experimental.pallas import tpu as pltpu

F32 = jnp.float32
BF16 = jnp.bfloat16

D_MODEL = 1024
DEPTH = 4
N_A_LAYERS = 2
CHUNK = 128
CM_WIDTH = 2048
CM_GROUPS = 8
CM_GROUP_DIM = CM_WIDTH // CM_GROUPS
FA_HEADS = 16
FA_HEAD_DIM = 64
FA_WIDTH = FA_HEADS * FA_HEAD_DIM
HEAD_PAIRS = FA_HEADS // 2
PAGE_SIZE = 128
DEEPNORM_ALPHA = (2.0 * DEPTH) ** 0.25
LN_EPS = 1e-5
LANES = 128
NEG_BIG = -1e30
VMEM_LIMIT = 56 * 1024 * 1024

ROW_TILE = 512
ATT_TILE = 512
FLASH_ROWS = 32
LOG2E = 1.4426950408889634


def _params(n_axes, vmem=VMEM_LIMIT):
    return pltpu.CompilerParams(dimension_semantics=("arbitrary",) * n_axes, vmem_limit_bytes=vmem)


def _dot(a, b):
    return jnp.dot(a, b, preferred_element_type=F32)


def _dot_nt(a, b):
    return lax.dot_general(a, b, (((1,), (1,)), ((), ())), preferred_element_type=F32)


def _layer_norm(x, g, b):
    mu = jnp.mean(x, axis=-1, keepdims=True)
    d = x - mu
    var = jnp.mean(d * d, axis=-1, keepdims=True)
    return d * lax.rsqrt(var + LN_EPS) * g + b


def _log_sigmoid(x):
    return -(jnp.maximum(-x, 0.0) + jnp.log1p(jnp.exp(-jnp.abs(x))))


def _split3_dot(x, w):
    hi = x.astype(BF16)
    r1 = x - hi.astype(F32)
    mid = r1.astype(BF16)
    lo = (r1 - mid.astype(F32)).astype(BF16)
    return _dot(hi, w) + _dot(mid, w) + _dot(lo, w)


def _const_spec(shape):
    nd = len(shape)
    return pl.BlockSpec(shape, lambda *_: (0,) * nd, pipeline_mode=pl.Buffered(1))


def _ada_kernel(c_ref, w_ref, b_ref, o_ref):
    a = jax.nn.silu(c_ref[...]).astype(BF16)
    o_ref[0] = _dot(a, w_ref[0].astype(BF16)) + b_ref[0]


def _ada_modulation(c_all, w_ada, b_ada):
    n = c_all.shape[0]
    nt = 3 * D_MODEL // D_MODEL
    return pl.pallas_call(
        _ada_kernel,
        grid=(DEPTH, nt),
        in_specs=[pl.BlockSpec((n, D_MODEL), lambda l, j: (0, 0)),
                  pl.BlockSpec((1, D_MODEL, D_MODEL), lambda l, j: (l, 0, j)),
                  pl.BlockSpec((1, 1, D_MODEL), lambda l, j: (l, 0, j))],
        out_specs=pl.BlockSpec((1, n, D_MODEL), lambda l, j: (l, 0, j)),
        out_shape=jax.ShapeDtypeStruct((DEPTH, n, 3 * D_MODEL), F32),
        compiler_params=_params(2),
        name="ada_modulation",
    )(c_all, w_ada, b_ada.reshape(DEPTH, 1, 3 * D_MODEL))


def _chunk_mlp_kernel(x_ref, mod_ref, win_ref, bin_ref, lvg_ref, lvb_ref, ws_ref, bs_ref,
                      wout_ref, bout_ref, lng_ref, lnb_ref, *rest, single_row_chunks):
    if single_row_chunks:
        o_ref, vn_ref, t_scr = rest
    else:
        o_ref, t_scr = rest
    rows = x_ref.shape[0]
    x = x_ref[...]
    m = mod_ref[0]
    shift, scale, gate = m[:, :D_MODEL], m[:, D_MODEL:2 * D_MODEL], m[:, 2 * D_MODEL:]
    h = (x * (1.0 + scale) + shift).astype(BF16)

    v = jax.nn.gelu(_dot(h, win_ref[:, CM_WIDTH:2 * CM_WIDTH]) + bin_ref[:, CM_WIDTH:2 * CM_WIDTH])
    vn = _layer_norm(v, lvg_ref[...], lvb_ref[...])
    if single_row_chunks:
        vn_ref[...] = vn
    else:
        r_i = lax.broadcasted_iota(jnp.int32, (CHUNK, CHUNK), 0)
        c_i = lax.broadcasted_iota(jnp.int32, (CHUNK, CHUNK), 1)
        causal = c_i <= r_i

    for g in range(CM_GROUPS):
        lo, hi = g * CM_GROUP_DIM, (g + 1) * CM_GROUP_DIM
        vn_g = vn[:, lo:hi]
        if single_row_chunks:
            sg = vn_g * ws_ref[:, lo:hi] + bs_ref[:, lo:hi]
        else:
            w_g = jnp.where(causal, ws_ref[g], 0.0).astype(BF16)
            b_g = bs_ref[:, g:g + 1]
            vb = vn_g.astype(BF16)
            sg = jnp.concatenate(
                [_dot(w_g, vb[c * CHUNK:(c + 1) * CHUNK]) + b_g for c in range(rows // CHUNK)], axis=0)
        u_g = jax.nn.gelu(_dot(h, win_ref[:, lo:hi]) + bin_ref[:, lo:hi])
        z_g = _dot(h, win_ref[:, 2 * CM_WIDTH + lo:2 * CM_WIDTH + hi]) + bin_ref[:, 2 * CM_WIDTH + lo:2 * CM_WIDTH + hi]
        t_scr[:, lo:hi] = (u_g * sg * jax.nn.silu(z_g)).astype(BF16)

    y = _dot(t_scr[...], wout_ref[...]) + bout_ref[...]
    r = DEEPNORM_ALPHA * x + (1.0 + gate) * y
    o_ref[...] = _layer_norm(r, lng_ref[...], lnb_ref[...])


def _chunk_mlp_layer(x, mod, w_in, b_in, lvg, lvb, w_s, b_s, w_out, b_out, ln_g, ln_b, *, single_row_chunks):
    rows = x.shape[0]
    nb, mrows = mod.shape[0], mod.shape[1]
    tm = rows if single_row_chunks else ROW_TILE
    tiles_per_batch = rows // nb // tm
    if single_row_chunks:
        ws_arg = jnp.repeat(w_s[:, 0, 0], CM_GROUP_DIM)[None]
        bs_arg = jnp.repeat(b_s[:, 0], CM_GROUP_DIM)[None]
    else:
        ws_arg, bs_arg = w_s, b_s.T
    out_shape = [jax.ShapeDtypeStruct((rows, D_MODEL), F32)]
    out_specs = [pl.BlockSpec((tm, D_MODEL), lambda i: (i, 0))]
    if single_row_chunks:
        out_shape.append(jax.ShapeDtypeStruct((rows, CM_WIDTH), F32))
        out_specs.append(pl.BlockSpec((tm, CM_WIDTH), lambda i: (i, 0)))
    res = pl.pallas_call(
        functools.partial(_chunk_mlp_kernel, single_row_chunks=single_row_chunks),
        grid=(rows // tm,),
        in_specs=[pl.BlockSpec((tm, D_MODEL), lambda i: (i, 0)),
                  pl.BlockSpec((1, mrows, 3 * D_MODEL), lambda i: (i // tiles_per_batch, 0, 0)),
                  _const_spec((D_MODEL, 3 * CM_WIDTH)), _const_spec((1, 3 * CM_WIDTH)),
                  _const_spec((1, CM_WIDTH)), _const_spec((1, CM_WIDTH)),
                  _const_spec(ws_arg.shape), _const_spec(bs_arg.shape),
                  _const_spec((CM_WIDTH, D_MODEL)), _const_spec((1, D_MODEL)),
                  _const_spec((1, D_MODEL)), _const_spec((1, D_MODEL))],
        out_specs=out_specs,
        out_shape=out_shape,
        scratch_shapes=[pltpu.VMEM((tm, CM_WIDTH), BF16)],
        compiler_params=_params(1),
        name="chunk_mlp_sample" if single_row_chunks else "chunk_mlp_prompt",
    )(x, mod, w_in.astype(BF16), b_in[None], lvg[None], lvb[None], ws_arg, bs_arg,
      w_out.astype(BF16), b_out[None], ln_g[None], ln_b[None])
    return res


def _kv_kernel(x_ref, wkv_ref, bkv_ref, wkvt_ref, bkvt_ref, wft_ref, bft_ref, *rest, prompt):
    if prompt:
        kt_ref, vt_ref, lft_ref, kb_ref, vb_ref, ct_ref, carry_scr = rest
    else:
        wf_ref, bf_ref, kt_ref, vt_ref, lft_ref, k_ref, v_ref, lf_ref = rest
    rows = x_ref.shape[0]
    xb = x_ref[...].astype(BF16)
    kvt = _dot_nt(wkvt_ref[...], xb) + bkvt_ref[...]
    kt_ref[0] = kvt[:FA_WIDTH]
    vt_ref[0] = kvt[FA_WIDTH:]
    lft = _log_sigmoid(_dot_nt(wft_ref[...], xb) + bft_ref[...])
    lft_ref[0] = lft
    if not prompt:
        kv = _dot(xb, wkv_ref[...]) + bkv_ref[...]
        k_ref[...] = kv[:, :FA_WIDTH]
        v_ref[...] = kv[:, FA_WIDTH:]
        lf_ref[...] = _log_sigmoid(_dot(xb, wf_ref[...]) + bf_ref[...])[:, :FA_HEADS]
    else:
        ktb = kvt[:FA_WIDTH].astype(BF16)
        vb = (_dot(xb, wkv_ref[...]) + bkv_ref[...]).astype(BF16)
        for j in range(HEAD_PAIRS):
            kb_ref[0, j, 0] = ktb[j * LANES:(j + 1) * LANES, :]
            vb_ref[0, j] = vb[:, j * LANES:(j + 1) * LANES]
        r_i = lax.broadcasted_iota(jnp.int32, (rows, rows), 0)
        c_i = lax.broadcasted_iota(jnp.int32, (rows, rows), 1)
        upper = jnp.where(r_i <= c_i, 1.0, 0.0).astype(BF16)

        @pl.when(pl.program_id(1) == 0)
        def _():
            carry_scr[...] = jnp.zeros_like(carry_scr)

        ct = _split3_dot(lft, upper) + carry_scr[:, 0:1]
        carry_scr[...] = jnp.broadcast_to(ct[:, rows - 1:rows], carry_scr.shape)
        ct2 = ct * LOG2E
        for j in range(HEAD_PAIRS):
            ct_ref[0, j, 0] = ct2[2 * j:2 * j + 2, :]


def _kv_weights(kv_w, kv_b):
    wkv = kv_w[:, :2 * FA_WIDTH].astype(BF16)
    bkv = kv_b[None, :2 * FA_WIDTH]
    wkvt = kv_w[:, :2 * FA_WIDTH].T.astype(BF16)
    bkvt = kv_b[:2 * FA_WIDTH, None]
    wft = kv_w[:, 2 * FA_WIDTH:].T.astype(BF16)
    bft = kv_b[2 * FA_WIDTH:, None]
    return wkv, bkv, wkvt, bkvt, wft, bft


def _kv_prompt(x, kv_w, kv_b, batch, seq):
    tm = ATT_TILE
    nt = seq // tm
    wkv, bkv, wkvt, bkvt, wft, bft = _kv_weights(kv_w, kv_b)
    weights = (wkv[:, FA_WIDTH:], bkv[:, FA_WIDTH:], wkvt, bkvt, wft, bft)
    return pl.pallas_call(
        functools.partial(_kv_kernel, prompt=True),
        grid=(batch, nt),
        in_specs=[pl.BlockSpec((tm, D_MODEL), lambda b, i: (b * nt + i, 0))] + [_const_spec(w.shape) for w in weights],
        out_specs=[pl.BlockSpec((1, FA_WIDTH, tm), lambda b, i: (b, 0, i)),
                   pl.BlockSpec((1, FA_WIDTH, tm), lambda b, i: (b, 0, i)),
                   pl.BlockSpec((1, FA_HEADS, tm), lambda b, i: (b, 0, i)),
                   pl.BlockSpec((1, HEAD_PAIRS, 1, LANES, tm), lambda b, i: (b, 0, i, 0, 0)),
                   pl.BlockSpec((1, HEAD_PAIRS, tm, LANES), lambda b, i: (b, 0, i, 0)),
                   pl.BlockSpec((1, HEAD_PAIRS, 1, 2, tm), lambda b, i: (b, 0, i, 0, 0))],
        out_shape=[jax.ShapeDtypeStruct((batch, FA_WIDTH, seq), F32),
                   jax.ShapeDtypeStruct((batch, FA_WIDTH, seq), F32),
                   jax.ShapeDtypeStruct((batch, FA_HEADS, seq), F32),
                   jax.ShapeDtypeStruct((batch, HEAD_PAIRS, nt, LANES, tm), BF16),
                   jax.ShapeDtypeStruct((batch, HEAD_PAIRS, seq, LANES), BF16),
                   jax.ShapeDtypeStruct((batch, HEAD_PAIRS, nt, 2, tm), F32)],
        scratch_shapes=[pltpu.VMEM((FA_HEADS, LANES), F32)],
        compiler_params=_params(2),
        name="kv_prompt",
    )(x, *weights)


def _kv_sample(x, kv_w, kv_b):
    rows = x.shape[0]
    weights = _kv_weights(kv_w, kv_b)
    wf = jnp.pad(kv_w[:, 2 * FA_WIDTH:], ((0, 0), (0, LANES - FA_HEADS))).astype(BF16)
    bf = jnp.pad(kv_b[2 * FA_WIDTH:], (0, LANES - FA_HEADS))[None]
    weights = weights + (wf, bf)
    full = lambda i: (0, 0)
    full3 = lambda i: (0, 0, 0)
    return pl.pallas_call(
        functools.partial(_kv_kernel, prompt=False),
        grid=(1,),
        in_specs=[pl.BlockSpec((rows, D_MODEL), full)] + [_const_spec(w.shape) for w in weights],
        out_specs=[pl.BlockSpec((1, FA_WIDTH, rows), full3), pl.BlockSpec((1, FA_WIDTH, rows), full3),
                   pl.BlockSpec((1, FA_HEADS, rows), full3),
                   pl.BlockSpec((rows, FA_WIDTH), full), pl.BlockSpec((rows, FA_WIDTH), full),
                   pl.BlockSpec((rows, FA_HEADS), full)],
        out_shape=[jax.ShapeDtypeStruct((1, FA_WIDTH, rows), F32), jax.ShapeDtypeStruct((1, FA_WIDTH, rows), F32),
                   jax.ShapeDtypeStruct((1, FA_HEADS, rows), F32),
                   jax.ShapeDtypeStruct((rows, FA_WIDTH), F32), jax.ShapeDtypeStruct((rows, FA_WIDTH), F32),
                   jax.ShapeDtypeStruct((rows, FA_HEADS), F32)],
        compiler_params=_params(1),
        name="kv_sample",
    )(x, *weights)


def _q_proj_kernel(x_ref, mod_ref, w_ref, b_ref, q_ref, gz_ref, *, pair_major):
    x = x_ref[...]
    m = mod_ref[0]
    h = (x * (1.0 + m[:, D_MODEL:2 * D_MODEL]) + m[:, :D_MODEL]).astype(BF16)
    p = _dot(h, w_ref[...]) + b_ref[...]
    if pair_major:
        q = (p[:, :FA_WIDTH] * (LOG2E * FA_HEAD_DIM ** -0.5)).astype(BF16)
        for j in range(HEAD_PAIRS):
            q_ref[0, j] = q[:, j * LANES:(j + 1) * LANES]
    else:
        q_ref[...] = (p[:, :FA_WIDTH] * (FA_HEAD_DIM ** -0.5)).astype(BF16)
    gz_ref[...] = jax.nn.silu(p[:, FA_WIDTH:])


def _q_proj(x, mod, w_in, b_in, *, batch=None, seq=None):
    rows = x.shape[0]
    nb, mrows = mod.shape[0], mod.shape[1]
    pair_major = batch is not None
    tm = ROW_TILE if pair_major else rows
    tpb = rows // nb // tm
    if pair_major:
        q_spec = pl.BlockSpec((1, HEAD_PAIRS, tm, LANES), lambda i: (i // tpb, 0, i % tpb, 0))
        q_shape = jax.ShapeDtypeStruct((batch, HEAD_PAIRS, seq, LANES), BF16)
    else:
        q_spec = pl.BlockSpec((tm, FA_WIDTH), lambda i: (i, 0))
        q_shape = jax.ShapeDtypeStruct((rows, FA_WIDTH), BF16)
    return pl.pallas_call(
        functools.partial(_q_proj_kernel, pair_major=pair_major),
        grid=(rows // tm,),
        in_specs=[pl.BlockSpec((tm, D_MODEL), lambda i: (i, 0)),
                  pl.BlockSpec((1, mrows, 3 * D_MODEL), lambda i: (i // tpb, 0, 0)),
                  _const_spec((D_MODEL, 2 * FA_WIDTH)), _const_spec((1, 2 * FA_WIDTH))],
        out_specs=[q_spec, pl.BlockSpec((tm, FA_WIDTH), lambda i: (i, 0))],
        out_shape=[q_shape, jax.ShapeDtypeStruct((rows, FA_WIDTH), F32)],
        compiler_params=_params(1),
        name="q_proj_prompt" if pair_major else "q_proj_sample",
    )(x, mod, w_in.astype(BF16), b_in[None])


def _out_proj_kernel(o_ref, gz_ref, x_ref, mod_ref, w_ref, b_ref, g_ref, bb_ref, y_ref):
    t = (o_ref[...] * gz_ref[...]).astype(BF16)
    y = _dot(t, w_ref[...]) + b_ref[...]
    gate = mod_ref[0][:, 2 * D_MODEL:]
    r = DEEPNORM_ALPHA * x_ref[...] + (1.0 + gate) * y
    y_ref[...] = _layer_norm(r, g_ref[...], bb_ref[...])


def _out_proj(o, gz, x, mod, w_out, b_out, ln_g, ln_b, *, tm):
    rows = x.shape[0]
    nb, mrows = mod.shape[0], mod.shape[1]
    tpb = rows // nb // tm
    row = lambda i: (i, 0)
    return pl.pallas_call(
        _out_proj_kernel,
        grid=(rows // tm,),
        in_specs=[pl.BlockSpec((tm, FA_WIDTH), row), pl.BlockSpec((tm, FA_WIDTH), row),
                  pl.BlockSpec((tm, D_MODEL), row),
                  pl.BlockSpec((1, mrows, 3 * D_MODEL), lambda i: (i // tpb, 0, 0)),
                  _const_spec((FA_WIDTH, D_MODEL)), _const_spec((1, D_MODEL)),
                  _const_spec((1, D_MODEL)), _const_spec((1, D_MODEL))],
        out_specs=pl.BlockSpec((tm, D_MODEL), row),
        out_shape=jax.ShapeDtypeStruct((rows, D_MODEL), F32),
        compiler_params=_params(1),
        name="out_proj",
    )(o, gz, x, mod, w_out.astype(BF16), b_out[None], ln_g[None], ln_b[None])


def _flash_kernel(q_ref, k_ref, v_ref, c_ref, o_ref, *scr, tile, n_tiles):
    s_scr = (scr[0:2], scr[2:4])
    m_scr = (scr[4:6], scr[6:8])
    p_scr, alpha_scr, acc_scr = scr[8:10], scr[10:12], scr[12:14]
    lane = lax.broadcasted_iota(jnp.int32, (1, LANES), 1)
    first = lane < FA_HEAD_DIM
    sub8 = lax.broadcasted_iota(jnp.int32, (8, tile), 0)

    def logits(t, q_heads, buf):
        c2 = c_ref[0, 0, t]
        hi = c2.astype(BF16).astype(F32)
        mid = (c2 - hi).astype(BF16).astype(F32)
        lo = c2 - hi - mid
        pieces = (hi[0:1], mid[0:1], lo[0:1], hi[1:2], mid[1:2], lo[1:2])
        c_rows = jnp.zeros((8, tile), F32)
        for r, piece in enumerate(pieces):
            c_rows = jnp.where(sub8 == r, piece, c_rows)
        c_block = jnp.concatenate([c_rows, jnp.zeros((LANES - 8, tile), F32)], axis=0).astype(BF16)
        k_aug = jnp.concatenate([k_ref[0, 0, t], c_block], axis=0)
        for a in range(2):
            s_scr[buf][a][...] = _dot(q_heads[a], k_aug)

    def softmax_blocks(a, buf, masked):
        for r0 in range(0, tile, FLASH_ROWS):
            rows = slice(r0, r0 + FLASH_ROWS)
            ncols = min(tile, -(-(r0 + FLASH_ROWS) // LANES) * LANES) if masked else tile
            s = s_scr[buf][a][rows, :ncols]
            if masked:
                r_i = r0 + lax.broadcasted_iota(jnp.int32, (FLASH_ROWS, ncols), 0)
                c_i = lax.broadcasted_iota(jnp.int32, (FLASH_ROWS, ncols), 1)
                s = jnp.where(c_i <= r_i, s, NEG_BIG)
            m_old = m_scr[buf][a][rows, :]
            m_new = jnp.maximum(m_old, jnp.max(s, axis=-1, keepdims=True))
            m_scr[1 - buf][a][rows, :] = m_new
            alpha_scr[a][rows, :] = jnp.exp2(m_old - m_new)
            p_scr[a][rows, :ncols] = jnp.exp2(s - m_new).astype(BF16)
            if ncols < tile:
                p_scr[a][rows, ncols:] = jnp.zeros((FLASH_ROWS, tile - ncols), BF16)

    def consume(t, buf, masked):
        v2 = v_ref[0, 0, pl.ds(pl.multiple_of(t * tile, tile), tile), :]
        one = jnp.ones_like(v2)
        v_heads = (jnp.where(first, v2, one), jnp.where(first, one, v2))
        for a in range(2):
            softmax_blocks(a, buf, masked)
            acc_scr[a][...] = alpha_scr[a][...] * acc_scr[a][...] + _dot(p_scr[a][...], v_heads[a])

    col_q = lax.broadcasted_iota(jnp.int32, (tile, LANES), 1)
    q_minus = (jnp.where(col_q < 3, -1.0, 0.0).astype(BF16),
               jnp.where((col_q >= 3) & (col_q < 6), -1.0, 0.0).astype(BF16))

    def q_body(i, _):
        q2 = q_ref[0, 0, pl.ds(pl.multiple_of(i * tile, tile), tile), :]
        zero = jnp.zeros_like(q2)
        q_heads = (jnp.concatenate([jnp.where(first, q2, zero), q_minus[0]], axis=1),
                   jnp.concatenate([jnp.where(first, zero, q2), q_minus[1]], axis=1))
        for a in range(2):
            m_scr[0][a][...] = jnp.full((tile, 1), NEG_BIG, F32)
            acc_scr[a][...] = jnp.zeros((tile, LANES), F32)

        logits(0, q_heads, 0)

        def tile_pair(u, _):
            logits(2 * u + 1, q_heads, 1)
            consume(2 * u, 0, False)
            logits(2 * u + 2, q_heads, 0)
            consume(2 * u + 1, 1, False)
            return 0

        lax.fori_loop(0, i // 2, tile_pair, 0)

        @pl.when(i % 2 == 1)
        def _():
            logits(i, q_heads, 1)
            consume(i - 1, 0, False)
            consume(i, 1, True)

        @pl.when(i % 2 == 0)
        def _():
            consume(i, 0, True)

        acc_a, acc_b = acc_scr[0][...], acc_scr[1][...]
        o_a = acc_a / pltpu.roll(acc_a, FA_HEAD_DIM, axis=1)
        o_b = acc_b / pltpu.roll(acc_b, FA_HEAD_DIM, axis=1)
        o_ref[0, pl.ds(pl.multiple_of(i * tile, tile), tile), :] = jnp.where(first, o_a, o_b)
        return 0

    lax.fori_loop(0, n_tiles, q_body, 0)


def _flash_attention(q_pm, k_pm, v_pm, c_t, batch, seq):
    tile = ATT_TILE
    nt = seq // tile
    pm = lambda b, j: (b, j, 0, 0)
    return pl.pallas_call(
        functools.partial(_flash_kernel, tile=tile, n_tiles=nt),
        grid=(batch, HEAD_PAIRS),
        in_specs=[pl.BlockSpec((1, 1, seq, LANES), pm),
                  pl.BlockSpec((1, 1, nt, LANES, tile), lambda b, j: (b, j, 0, 0, 0)),
                  pl.BlockSpec((1, 1, seq, LANES), pm),
                  pl.BlockSpec((1, 1, nt, 2, tile), lambda b, j: (b, j, 0, 0, 0))],
        out_specs=pl.BlockSpec((1, seq, LANES), lambda b, j: (b, 0, j)),
        out_shape=jax.ShapeDtypeStruct((batch, seq, FA_WIDTH), F32),
        scratch_shapes=([pltpu.VMEM((tile, tile), F32)] * 4
                        + [pltpu.VMEM((tile, 1), F32)] * 4
                        + [pltpu.VMEM((tile, tile), BF16)] * 2
                        + [pltpu.VMEM((tile, 1), F32)] * 2
                        + [pltpu.VMEM((tile, LANES), F32)] * 2),
        compiler_params=_params(2),
        name="flash_prompt",
    )(q_pm, k_pm, v_pm, c_t)


def _decode_kernel(pt_ref, q_ref, kn_ref, vn_ref, lfn_ref, *refs, n_pages):
    del pt_ref
    k_refs, v_refs, lf_refs = refs[:n_pages], refs[n_pages:2 * n_pages], refs[2 * n_pages:3 * n_pages]
    o_ref = refs[3 * n_pages]
    head = lax.broadcasted_iota(jnp.int32, (FA_HEADS, FA_WIDTH), 0)
    col = lax.broadcasted_iota(jnp.int32, (FA_HEADS, FA_WIDTH), 1)
    own = (col >= head * FA_HEAD_DIM) & (col < (head + 1) * FA_HEAD_DIM)
    q_rows = jnp.where(own, jnp.broadcast_to(q_ref[0].astype(F32), (FA_HEADS, FA_WIDTH)), 0.0)
    q_bf = q_rows.astype(BF16)

    r_i = lax.broadcasted_iota(jnp.int32, (PAGE_SIZE, PAGE_SIZE), 0)
    c_i = lax.broadcasted_iota(jnp.int32, (PAGE_SIZE, PAGE_SIZE), 1)
    upper = jnp.where(r_i <= c_i, 1.0, 0.0).astype(BF16)
    lf_all = jnp.concatenate([lf_refs[i][0] for i in range(n_pages)], axis=0)
    cum_all = _split3_dot(lf_all, upper)
    cums, offs = [], []
    off = jnp.zeros((FA_HEADS, 1), F32)
    for i in range(n_pages):
        cum_i = cum_all[i * FA_HEADS:(i + 1) * FA_HEADS]
        cums.append(cum_i)
        offs.append(off)
        off = off + cum_i[:, PAGE_SIZE - 1:PAGE_SIZE]
    c_query = off + lfn_ref[0]

    s_parts = []
    for i in range(n_pages):
        s_i = _dot(q_bf, k_refs[i][0].astype(BF16))
        s_parts.append(s_i + (c_query - (offs[i] + cums[i])))
    s = jnp.concatenate(s_parts, axis=1)
    s_new = jnp.sum(q_rows * kn_ref[0], axis=1, keepdims=True)
    m = jnp.maximum(jnp.max(s, axis=1, keepdims=True), s_new)
    p = jnp.exp(s - m)
    p_new = jnp.exp(s_new - m)
    l = jnp.sum(p, axis=1, keepdims=True) + p_new
    pb = p.astype(BF16)
    acc = p_new * vn_ref[0]
    for i in range(n_pages):
        acc = acc + _dot_nt(pb[:, i * PAGE_SIZE:(i + 1) * PAGE_SIZE], v_refs[i][0].astype(BF16))
    acc = acc / l
    o_ref[0] = jnp.sum(jnp.where(own, acc, 0.0), axis=0, keepdims=True)


def _decode_attention(q, k_new, v_new, lf_new, cache_kt, cache_vt, cache_lft, page_table):
    n, n_pages = page_table.shape

    def page_map(i):
        return lambda b, pt: (pt[b, i], 0, 0)

    row = lambda b, pt: (b, 0, 0)
    in_specs = [pl.BlockSpec((1, 1, FA_WIDTH), row), pl.BlockSpec((1, 1, FA_WIDTH), row),
                pl.BlockSpec((1, 1, FA_WIDTH), row), pl.BlockSpec((1, FA_HEADS, 1), row)]
    in_specs += [pl.BlockSpec((1, FA_WIDTH, PAGE_SIZE), page_map(i)) for i in range(n_pages)]
    in_specs += [pl.BlockSpec((1, FA_WIDTH, PAGE_SIZE), page_map(i)) for i in range(n_pages)]
    in_specs += [pl.BlockSpec((1, FA_HEADS, PAGE_SIZE), page_map(i)) for i in range(n_pages)]
    out = pl.pallas_call(
        functools.partial(_decode_kernel, n_pages=n_pages),
        grid_spec=pltpu.PrefetchScalarGridSpec(
            num_scalar_prefetch=1, grid=(n,), in_specs=in_specs,
            out_specs=pl.BlockSpec((1, 1, FA_WIDTH), row)),
        out_shape=jax.ShapeDtypeStruct((n, 1, FA_WIDTH), F32),
        compiler_params=_params(1),
        name="decode_attention",
    )(page_table, q.reshape(n, 1, FA_WIDTH), k_new.reshape(n, 1, FA_WIDTH), v_new.reshape(n, 1, FA_WIDTH),
      lf_new.reshape(n, FA_HEADS, 1), *([cache_kt] * n_pages), *([cache_vt] * n_pages), *([cache_lft] * n_pages))
    return out.reshape(n, FA_WIDTH)


def kernel(x_prompt, x_sample, c_prompt, c_sample, cache_k, cache_v, cache_logf, page_table, w_ada, b_ada, ln_g, ln_b, cm_w_in, cm_b_in, cm_ln_v_g, cm_ln_v_b, cm_w_s, cm_b_s, cm_w_out, cm_b_out, kv_w, kv_b, fa_w_in, fa_b_in, fa_w_out, fa_b_out):
    batch, seq, _ = x_prompt.shape
    n_dec = x_sample.shape[0]

    n_c = batch + n_dec
    c_all = jnp.pad(jnp.concatenate([c_prompt, c_sample], axis=0), ((0, (-n_c) % 8), (0, 0)))
    mod = _ada_modulation(c_all, w_ada, b_ada)
    mod_p = mod[:, :batch].reshape(DEPTH, batch, 1, 3 * D_MODEL)
    mod_s = mod[:, batch:n_c].reshape(DEPTH, 1, n_dec, 3 * D_MODEL)

    xp = x_prompt.reshape(batch * seq, D_MODEL)
    xs = x_sample.reshape(n_dec, D_MODEL)
    chunk_rows = []
    for layer in range(N_A_LAYERS):
        args = (cm_w_in[layer], cm_b_in[layer], cm_ln_v_g[layer], cm_ln_v_b[layer], cm_w_s[layer], cm_b_s[layer],
                cm_w_out[layer], cm_b_out[layer], ln_g[layer], ln_b[layer])
        (xp,) = _chunk_mlp_layer(xp, mod_p[layer], *args, single_row_chunks=False)
        xs, vn_s = _chunk_mlp_layer(xs, mod_s[layer], *args, single_row_chunks=True)
        chunk_rows.append(vn_s)

    kt_p, vt_p, lft_p, kb_p, vb_p, ct_p = _kv_prompt(xp, kv_w, kv_b, batch, seq)
    kt_s, vt_s, lft_s, k_s, v_s, lf_s = _kv_sample(xs, kv_w, kv_b)

    n_phys = cache_k.shape[0]
    cache_kt = jnp.transpose(cache_k, (0, 2, 3, 1)).reshape(n_phys, FA_WIDTH, PAGE_SIZE)
    cache_vt = jnp.transpose(cache_v, (0, 2, 3, 1)).reshape(n_phys, FA_WIDTH, PAGE_SIZE)
    cache_lft = jnp.swapaxes(cache_logf, 1, 2)

    for j in range(DEPTH - N_A_LAYERS):
        layer = N_A_LAYERS + j
        q_pm, gz_p = _q_proj(xp, mod_p[layer], fa_w_in[j], fa_b_in[j], batch=batch, seq=seq)
        o_p = _flash_attention(q_pm, kb_p, vb_p, ct_p, batch, seq).reshape(batch * seq, FA_WIDTH)
        xp = _out_proj(o_p, gz_p, xp, mod_p[layer], fa_w_out[j], fa_b_out[j], ln_g[layer], ln_b[layer], tm=ROW_TILE)

        q_s, gz_s = _q_proj(xs, mod_s[layer], fa_w_in[j], fa_b_in[j])
        o_s = _decode_attention(q_s, k_s, v_s, lf_s, cache_kt, cache_vt, cache_lft, page_table)
        xs = _out_proj(o_s, gz_s, xs, mod_s[layer], fa_w_out[j], fa_b_out[j], ln_g[layer], ln_b[layer], tm=n_dec)

    y_prompt = xp.reshape(batch, seq, D_MODEL)
    y_sample = xs.reshape(n_dec, 1, D_MODEL)
    k_prompt = jnp.transpose(kt_p.reshape(batch, FA_HEADS, FA_HEAD_DIM, seq), (0, 3, 1, 2))
    v_prompt = jnp.transpose(vt_p.reshape(batch, FA_HEADS, FA_HEAD_DIM, seq), (0, 3, 1, 2))
    logf_prompt = jnp.transpose(lft_p, (0, 2, 1))
    k_sample = jnp.transpose(kt_s.reshape(1, FA_HEADS, FA_HEAD_DIM, n_dec), (3, 0, 1, 2))
    v_sample = jnp.transpose(vt_s.reshape(1, FA_HEADS, FA_HEAD_DIM, n_dec), (3, 0, 1, 2))
    logf_sample = jnp.transpose(lft_s, (2, 0, 1))
    chunk_v_sample = jnp.stack(chunk_rows).reshape(N_A_LAYERS, n_dec, 1, CM_WIDTH)
    return (y_prompt, y_sample, k_prompt, v_prompt, logf_prompt, k_sample, v_sample, logf_sample, chunk_v_sample)
```

```python
import functools

import jax
import jax.numpy as jnp
from jax import lax
from jax.experimental import pallas as pl
from jax.experimental.pallas import tpu as pltpu

F32 = jnp.float32
BF16 = jnp.bfloat16

D_MODEL = 1024
DEPTH = 4
N_A_LAYERS = 2
CHUNK = 128
CM_WIDTH = 2048
CM_GROUPS = 8
CM_GROUP_DIM = CM_WIDTH // CM_GROUPS
FA_HEADS = 16
FA_HEAD_DIM = 64
FA_WIDTH = FA_HEADS * FA_HEAD_DIM
HEAD_PAIRS = FA_HEADS // 2
PAGE_SIZE = 128
DEEPNORM_ALPHA = (2.0 * DEPTH) ** 0.25
LN_EPS = 1e-5
LANES = 128
NEG_BIG = -1e30
VMEM_LIMIT = 56 * 1024 * 1024

ROW_TILE = 512
ATT_TILE = 512
FLASH_ROWS = 32
LOG2E = 1.4426950408889634


def _params(n_axes, vmem=VMEM_LIMIT):
    return pltpu.CompilerParams(dimension_semantics=("arbitrary",) * n_axes, vmem_limit_bytes=vmem)


def _dot(a, b):
    return jnp.dot(a, b, preferred_element_type=F32)


def _dot_nt(a, b):
    return lax.dot_general(a, b, (((1,), (1,)), ((), ())), preferred_element_type=F32)


def _layer_norm(x, g, b):
    mu = jnp.mean(x, axis=-1, keepdims=True)
    d = x - mu
    var = jnp.mean(d * d, axis=-1, keepdims=True)
    return d * lax.rsqrt(var + LN_EPS) * g + b


def _log_sigmoid(x):
    return -(jnp.maximum(-x, 0.0) + jnp.log1p(jnp.exp(-jnp.abs(x))))


def _split3_dot(x, w):
    hi = x.astype(BF16)
    r1 = x - hi.astype(F32)
    mid = r1.astype(BF16)
    lo = (r1 - mid.astype(F32)).astype(BF16)
    return _dot(hi, w) + _dot(mid, w) + _dot(lo, w)


def _const_spec(shape):
    nd = len(shape)
    return pl.BlockSpec(shape, lambda *_: (0,) * nd, pipeline_mode=pl.Buffered(1))


def _ada_kernel(c_ref, w_ref, b_ref, o_ref):
    a = jax.nn.silu(c_ref[...]).astype(BF16)
    o_ref[0] = _dot(a, w_ref[0].astype(BF16)) + b_ref[0]


def _ada_modulation(c_all, w_ada, b_ada):
    n = c_all.shape[0]
    nt = 3 * D_MODEL // D_MODEL
    return pl.pallas_call(
        _ada_kernel,
        grid=(DEPTH, nt),
        in_specs=[pl.BlockSpec((n, D_MODEL), lambda l, j: (0, 0)),
                  pl.BlockSpec((1, D_MODEL, D_MODEL), lambda l, j: (l, 0, j)),
                  pl.BlockSpec((1, 1, D_MODEL), lambda l, j: (l, 0, j))],
        out_specs=pl.BlockSpec((1, n, D_MODEL), lambda l, j: (l, 0, j)),
        out_shape=jax.ShapeDtypeStruct((DEPTH, n, 3 * D_MODEL), F32),
        compiler_params=_params(2),
        name="ada_modulation",
    )(c_all, w_ada, b_ada.reshape(DEPTH, 1, 3 * D_MODEL))


def _chunk_mlp_kernel(x_ref, mod_ref, win_ref, bin_ref, lvg_ref, lvb_ref, ws_ref, bs_ref,
                      wout_ref, bout_ref, lng_ref, lnb_ref, *rest, single_row_chunks):
    if single_row_chunks:
        o_ref, vn_ref, t_scr = rest
    else:
        o_ref, t_scr = rest
    rows = x_ref.shape[0]
    x = x_ref[...]
    m = mod_ref[0]
    shift, scale, gate = m[:, :D_MODEL], m[:, D_MODEL:2 * D_MODEL], m[:, 2 * D_MODEL:]
    h = (x * (1.0 + scale) + shift).astype(BF16)

    v = jax.nn.gelu(_dot(h, win_ref[:, CM_WIDTH:2 * CM_WIDTH]) + bin_ref[:, CM_WIDTH:2 * CM_WIDTH])
    vn = _layer_norm(v, lvg_ref[...], lvb_ref[...])
    if single_row_chunks:
        vn_ref[...] = vn
    else:
        r_i = lax.broadcasted_iota(jnp.int32, (CHUNK, CHUNK), 0)
        c_i = lax.broadcasted_iota(jnp.int32, (CHUNK, CHUNK), 1)
        causal = c_i <= r_i

    for g in range(CM_GROUPS):
        lo, hi = g * CM_GROUP_DIM, (g + 1) * CM_GROUP_DIM
        vn_g = vn[:, lo:hi]
        if single_row_chunks:
            sg = vn_g * ws_ref[:, lo:hi] + bs_ref[:, lo:hi]
        else:
            w_g = jnp.where(causal, ws_ref[g], 0.0).astype(BF16)
            b_g = bs_ref[:, g:g + 1]
            vb = vn_g.astype(BF16)
            sg = jnp.concatenate(
                [_dot(w_g, vb[c * CHUNK:(c + 1) * CHUNK]) + b_g for c in range(rows // CHUNK)], axis=0)
        u_g = jax.nn.gelu(_dot(h, win_ref[:, lo:hi]) + bin_ref[:, lo:hi])
        z_g = _dot(h, win_ref[:, 2 * CM_WIDTH + lo:2 * CM_WIDTH + hi]) + bin_ref[:, 2 * CM_WIDTH + lo:2 * CM_WIDTH + hi]
        t_scr[:, lo:hi] = (u_g * sg * jax.nn.silu(z_g)).astype(BF16)

    y = _dot(t_scr[...], wout_ref[...]) + bout_ref[...]
    r = DEEPNORM_ALPHA * x + (1.0 + gate) * y
    o_ref[...] = _layer_norm(r, lng_ref[...], lnb_ref[...])


def _chunk_mlp_layer(x, mod, w_in, b_in, lvg, lvb, w_s, b_s, w_out, b_out, ln_g, ln_b, *, single_row_chunks):
    rows = x.shape[0]
    nb, mrows = mod.shape[0], mod.shape[1]
    tm = rows if single_row_chunks else ROW_TILE
    tiles_per_batch = rows // nb // tm
    if single_row_chunks:
        ws_arg = jnp.repeat(w_s[:, 0, 0], CM_GROUP_DIM)[None]
        bs_arg = jnp.repeat(b_s[:, 0], CM_GROUP_DIM)[None]
    else:
        ws_arg, bs_arg = w_s, b_s.T
    out_shape = [jax.ShapeDtypeStruct((rows, D_MODEL), F32)]
    out_specs = [pl.BlockSpec((tm, D_MODEL), lambda i: (i, 0))]
    if single_row_chunks:
        out_shape.append(jax.ShapeDtypeStruct((rows, CM_WIDTH), F32))
        out_specs.append(pl.BlockSpec((tm, CM_WIDTH), lambda i: (i, 0)))
    res = pl.pallas_call(
        functools.partial(_chunk_mlp_kernel, single_row_chunks=single_row_chunks),
        grid=(rows // tm,),
        in_specs=[pl.BlockSpec((tm, D_MODEL), lambda i: (i, 0)),
                  pl.BlockSpec((1, mrows, 3 * D_MODEL), lambda i: (i // tiles_per_batch, 0, 0)),
                  _const_spec((D_MODEL, 3 * CM_WIDTH)), _const_spec((1, 3 * CM_WIDTH)),
                  _const_spec((1, CM_WIDTH)), _const_spec((1, CM_WIDTH)),
                  _const_spec(ws_arg.shape), _const_spec(bs_arg.shape),
                  _const_spec((CM_WIDTH, D_MODEL)), _const_spec((1, D_MODEL)),
                  _const_spec((1, D_MODEL)), _const_spec((1, D_MODEL))],
        out_specs=out_specs,
        out_shape=out_shape,
        scratch_shapes=[pltpu.VMEM((tm, CM_WIDTH), BF16)],
        compiler_params=_params(1),
        name="chunk_mlp_sample" if single_row_chunks else "chunk_mlp_prompt",
    )(x, mod, w_in.astype(BF16), b_in[None], lvg[None], lvb[None], ws_arg, bs_arg,
      w_out.astype(BF16), b_out[None], ln_g[None], ln_b[None])
    return res


def _kv_kernel(x_ref, wkv_ref, bkv_ref, wkvt_ref, bkvt_ref, wft_ref, bft_ref, *rest, prompt):
    if prompt:
        kt_ref, vt_ref, lft_ref, kb_ref, vb_ref, ct_ref, carry_scr = rest
    else:
        wf_ref, bf_ref, kt_ref, vt_ref, lft_ref, k_ref, v_ref, lf_ref = rest
    rows = x_ref.shape[0]
    xb = x_ref[...].astype(BF16)
    kvt = _dot_nt(wkvt_ref[...], xb) + bkvt_ref[...]
    kt_ref[0] = kvt[:FA_WIDTH]
    vt_ref[0] = kvt[FA_WIDTH:]
    lft = _log_sigmoid(_dot_nt(wft_ref[...], xb) + bft_ref[...])
    lft_ref[0] = lft
    if not prompt:
        kv = _dot(xb, wkv_ref[...]) + bkv_ref[...]
        k_ref[...] = kv[:, :FA_WIDTH]
        v_ref[...] = kv[:, FA_WIDTH:]
        lf_ref[...] = _log_sigmoid(_dot(xb, wf_ref[...]) + bf_ref[...])[:, :FA_HEADS]
    else:
        ktb = kvt[:FA_WIDTH].astype(BF16)
        vb = (_dot(xb, wkv_ref[...]) + bkv_ref[...]).astype(BF16)
        for j in range(HEAD_PAIRS):
            kb_ref[0, j, 0] = ktb[j * LANES:(j + 1) * LANES, :]
            vb_ref[0, j] = vb[:, j * LANES:(j + 1) * LANES]
        r_i = lax.broadcasted_iota(jnp.int32, (rows, rows), 0)
        c_i = lax.broadcasted_iota(jnp.int32, (rows, rows), 1)
        upper = jnp.where(r_i <= c_i, 1.0, 0.0).astype(BF16)

        @pl.when(pl.program_id(1) == 0)
        def _():
            carry_scr[...] = jnp.zeros_like(carry_scr)

        ct = _split3_dot(lft, upper) + carry_scr[:, 0:1]
        carry_scr[...] = jnp.broadcast_to(ct[:, rows - 1:rows], carry_scr.shape)
        ct2 = ct * LOG2E
        for j in range(HEAD_PAIRS):
            ct_ref[0, j, 0] = ct2[2 * j:2 * j + 2, :]


def _kv_weights(kv_w, kv_b):
    wkv = kv_w[:, :2 * FA_WIDTH].astype(BF16)
    bkv = kv_b[None, :2 * FA_WIDTH]
    wkvt = kv_w[:, :2 * FA_WIDTH].T.astype(BF16)
    bkvt = kv_b[:2 * FA_WIDTH, None]
    wft = kv_w[:, 2 * FA_WIDTH:].T.astype(BF16)
    bft = kv_b[2 * FA_WIDTH:, None]
    return wkv, bkv, wkvt, bkvt, wft, bft


def _kv_prompt(x, kv_w, kv_b, batch, seq):
    tm = ATT_TILE
    nt = seq // tm
    wkv, bkv, wkvt, bkvt, wft, bft = _kv_weights(kv_w, kv_b)
    weights = (wkv[:, FA_WIDTH:], bkv[:, FA_WIDTH:], wkvt, bkvt, wft, bft)
    return pl.pallas_call(
        functools.partial(_kv_kernel, prompt=True),
        grid=(batch, nt),
        in_specs=[pl.BlockSpec((tm, D_MODEL), lambda b, i: (b * nt + i, 0))] + [_const_spec(w.shape) for w in weights],
        out_specs=[pl.BlockSpec((1, FA_WIDTH, tm), lambda b, i: (b, 0, i)),
                   pl.BlockSpec((1, FA_WIDTH, tm), lambda b, i: (b, 0, i)),
                   pl.BlockSpec((1, FA_HEADS, tm), lambda b, i: (b, 0, i)),
                   pl.BlockSpec((1, HEAD_PAIRS, 1, LANES, tm), lambda b, i: (b, 0, i, 0, 0)),
                   pl.BlockSpec((1, HEAD_PAIRS, tm, LANES), lambda b, i: (b, 0, i, 0)),
                   pl.BlockSpec((1, HEAD_PAIRS, 1, 2, tm), lambda b, i: (b, 0, i, 0, 0))],
        out_shape=[jax.ShapeDtypeStruct((batch, FA_WIDTH, seq), F32),
                   jax.ShapeDtypeStruct((batch, FA_WIDTH, seq), F32),
                   jax.ShapeDtypeStruct((batch, FA_HEADS, seq), F32),
                   jax.ShapeDtypeStruct((batch, HEAD_PAIRS, nt, LANES, tm), BF16),
                   jax.ShapeDtypeStruct((batch, HEAD_PAIRS, seq, LANES), BF16),
                   jax.ShapeDtypeStruct((batch, HEAD_PAIRS, nt, 2, tm), F32)],
        scratch_shapes=[pltpu.VMEM((FA_HEADS, LANES), F32)],
        compiler_params=_params(2),
        name="kv_prompt",
    )(x, *weights)


def _kv_sample(x, kv_w, kv_b):
    rows = x.shape[0]
    weights = _kv_weights(kv_w, kv_b)
    wf = jnp.pad(kv_w[:, 2 * FA_WIDTH:], ((0, 0), (0, LANES - FA_HEADS))).astype(BF16)
    bf = jnp.pad(kv_b[2 * FA_WIDTH:], (0, LANES - FA_HEADS))[None]
    weights = weights + (wf, bf)
    full = lambda i: (0, 0)
    full3 = lambda i: (0, 0, 0)
    return pl.pallas_call(
        functools.partial(_kv_kernel, prompt=False),
        grid=(1,),
        in_specs=[pl.BlockSpec((rows, D_MODEL), full)] + [_const_spec(w.shape) for w in weights],
        out_specs=[pl.BlockSpec((1, FA_WIDTH, rows), full3), pl.BlockSpec((1, FA_WIDTH, rows), full3),
                   pl.BlockSpec((1, FA_HEADS, rows), full3),
                   pl.BlockSpec((rows, FA_WIDTH), full), pl.BlockSpec((rows, FA_WIDTH), full),
                   pl.BlockSpec((rows, FA_HEADS), full)],
        out_shape=[jax.ShapeDtypeStruct((1, FA_WIDTH, rows), F32), jax.ShapeDtypeStruct((1, FA_WIDTH, rows), F32),
                   jax.ShapeDtypeStruct((1, FA_HEADS, rows), F32),
                   jax.ShapeDtypeStruct((rows, FA_WIDTH), F32), jax.ShapeDtypeStruct((rows, FA_WIDTH), F32),
                   jax.ShapeDtypeStruct((rows, FA_HEADS), F32)],
        compiler_params=_params(1),
        name="kv_sample",
    )(x, *weights)


def _q_proj_kernel(x_ref, mod_ref, w_ref, b_ref, q_ref, gz_ref, *, pair_major):
    x = x_ref[...]
    m = mod_ref[0]
    h = (x * (1.0 + m[:, D_MODEL:2 * D_MODEL]) + m[:, :D_MODEL]).astype(BF16)
    p = _dot(h, w_ref[...]) + b_ref[...]
    if pair_major:
        q = (p[:, :FA_WIDTH] * (LOG2E * FA_HEAD_DIM ** -0.5)).astype(BF16)
        for j in range(HEAD_PAIRS):
            q_ref[0, j] = q[:, j * LANES:(j + 1) * LANES]
    else:
        q_ref[...] = (p[:, :FA_WIDTH] * (FA_HEAD_DIM ** -0.5)).astype(BF16)
    gz_ref[...] = jax.nn.silu(p[:, FA_WIDTH:])


def _q_proj(x, mod, w_in, b_in, *, batch=None, seq=None):
    rows = x.shape[0]
    nb, mrows = mod.shape[0], mod.shape[1]
    pair_major = batch is not None
    tm = ROW_TILE if pair_major else rows
    tpb = rows // nb // tm
    if pair_major:
        q_spec = pl.BlockSpec((1, HEAD_PAIRS, tm, LANES), lambda i: (i // tpb, 0, i % tpb, 0))
        q_shape = jax.ShapeDtypeStruct((batch, HEAD_PAIRS, seq, LANES), BF16)
    else:
        q_spec = pl.BlockSpec((tm, FA_WIDTH), lambda i: (i, 0))
        q_shape = jax.ShapeDtypeStruct((rows, FA_WIDTH), BF16)
    return pl.pallas_call(
        functools.partial(_q_proj_kernel, pair_major=pair_major),
        grid=(rows // tm,),
        in_specs=[pl.BlockSpec((tm, D_MODEL), lambda i: (i, 0)),
                  pl.BlockSpec((1, mrows, 3 * D_MODEL), lambda i: (i // tpb, 0, 0)),
                  _const_spec((D_MODEL, 2 * FA_WIDTH)), _const_spec((1, 2 * FA_WIDTH))],
        out_specs=[q_spec, pl.BlockSpec((tm, FA_WIDTH), lambda i: (i, 0))],
        out_shape=[q_shape, jax.ShapeDtypeStruct((rows, FA_WIDTH), F32)],
        compiler_params=_params(1),
        name="q_proj_prompt" if pair_major else "q_proj_sample",
    )(x, mod, w_in.astype(BF16), b_in[None])


def _out_proj_kernel(o_ref, gz_ref, x_ref, mod_ref, w_ref, b_ref, g_ref, bb_ref, y_ref):
    t = (o_ref[...] * gz_ref[...]).astype(BF16)
    y = _dot(t, w_ref[...]) + b_ref[...]
    gate = mod_ref[0][:, 2 * D_MODEL:]
    r = DEEPNORM_ALPHA * x_ref[...] + (1.0 + gate) * y
    y_ref[...] = _layer_norm(r, g_ref[...], bb_ref[...])


def _out_proj(o, gz, x, mod, w_out, b_out, ln_g, ln_b, *, tm):
    rows = x.shape[0]
    nb, mrows = mod.shape[0], mod.shape[1]
    tpb = rows // nb // tm
    row = lambda i: (i, 0)
    return pl.pallas_call(
        _out_proj_kernel,
        grid=(rows // tm,),
        in_specs=[pl.BlockSpec((tm, FA_WIDTH), row), pl.BlockSpec((tm, FA_WIDTH), row),
                  pl.BlockSpec((tm, D_MODEL), row),
                  pl.BlockSpec((1, mrows, 3 * D_MODEL), lambda i: (i // tpb, 0, 0)),
                  _const_spec((FA_WIDTH, D_MODEL)), _const_spec((1, D_MODEL)),
                  _const_spec((1, D_MODEL)), _const_spec((1, D_MODEL))],
        out_specs=pl.BlockSpec((tm, D_MODEL), row),
        out_shape=jax.ShapeDtypeStruct((rows, D_MODEL), F32),
        compiler_params=_params(1),
        name="out_proj",
    )(o, gz, x, mod, w_out.astype(BF16), b_out[None], ln_g[None], ln_b[None])


def _flash_kernel(q_ref, k_ref, v_ref, c_ref, o_ref, *scr, tile, n_tiles):
    s_scr = (scr[0:2], scr[2:4])
    m_scr = (scr[4:6], scr[6:8])
    p_scr, alpha_scr, acc_scr = scr[8:10], scr[10:12], scr[12:14]
    lane = lax.broadcasted_iota(jnp.int32, (1, LANES), 1)
    first = lane < FA_HEAD_DIM
    sub8 = lax.broadcasted_iota(jnp.int32, (8, tile), 0)

    def logits(t, q_heads, buf):
        c2 = c_ref[0, 0, t]
        hi = c2.astype(BF16).astype(F32)
        mid = (c2 - hi).astype(BF16).astype(F32)
        lo = c2 - hi - mid
        pieces = (hi[0:1], mid[0:1], lo[0:1], hi[1:2], mid[1:2], lo[1:2])
        c_rows = jnp.zeros((8, tile), F32)
        for r, piece in enumerate(pieces):
            c_rows = jnp.where(sub8 == r, piece, c_rows)
        c_block = jnp.concatenate([c_rows, jnp.zeros((LANES - 8, tile), F32)], axis=0).astype(BF16)
        k_aug = jnp.concatenate([k_ref[0, 0, t], c_block], axis=0)
        for a in range(2):
            s_scr[buf][a][...] = _dot(q_heads[a], k_aug)

    def softmax_blocks(a, buf, masked):
        for r0 in range(0, tile, FLASH_ROWS):
            rows = slice(r0, r0 + FLASH_ROWS)
            ncols = min(tile, -(-(r0 + FLASH_ROWS) // LANES) * LANES) if masked else tile
            s = s_scr[buf][a][rows, :ncols]
            if masked:
                r_i = r0 + lax.broadcasted_iota(jnp.int32, (FLASH_ROWS, ncols), 0)
                c_i = lax.broadcasted_iota(jnp.int32, (FLASH_ROWS, ncols), 1)
                s = jnp.where(c_i <= r_i, s, NEG_BIG)
            m_old = m_scr[buf][a][rows, :]
            m_new = jnp.maximum(m_old, jnp.max(s, axis=-1, keepdims=True))
            m_scr[1 - buf][a][rows, :] = m_new
            alpha_scr[a][rows, :] = jnp.exp2(m_old - m_new)
            m_wide = jnp.concatenate([m_new] * (ncols // LANES), axis=1)
            p_scr[a][rows, :ncols] = jnp.exp2((s - m_wide).astype(BF16))
            if ncols < tile:
                p_scr[a][rows, ncols:] = jnp.zeros((FLASH_ROWS, tile - ncols), BF16)

    def consume(t, buf, masked):
        v2 = v_ref[0, 0, pl.ds(pl.multiple_of(t * tile, tile), tile), :]
        one = jnp.ones_like(v2)
        v_heads = (jnp.where(first, v2, one), jnp.where(first, one, v2))
        for a in range(2):
            softmax_blocks(a, buf, masked)
            acc_scr[a][...] = alpha_scr[a][...] * acc_scr[a][...] + _dot(p_scr[a][...], v_heads[a])

    col_q = lax.broadcasted_iota(jnp.int32, (tile, LANES), 1)
    q_minus = (jnp.where(col_q < 3, -1.0, 0.0).astype(BF16),
               jnp.where((col_q >= 3) & (col_q < 6), -1.0, 0.0).astype(BF16))

    def q_body(i, _):
        q2 = q_ref[0, 0, pl.ds(pl.multiple_of(i * tile, tile), tile), :]
        zero = jnp.zeros_like(q2)
        q_heads = (jnp.concatenate([jnp.where(first, q2, zero), q_minus[0]], axis=1),
                   jnp.concatenate([jnp.where(first, zero, q2), q_minus[1]], axis=1))
        for a in range(2):
            m_scr[0][a][...] = jnp.full((tile, LANES), NEG_BIG, F32)
            acc_scr[a][...] = jnp.zeros((tile, LANES), F32)

        logits(0, q_heads, 0)

        def tile_pair(u, _):
            logits(2 * u + 1, q_heads, 1)
            consume(2 * u, 0, False)
            logits(2 * u + 2, q_heads, 0)
            consume(2 * u + 1, 1, False)
            return 0

        lax.fori_loop(0, i // 2, tile_pair, 0)

        @pl.when(i % 2 == 1)
        def _():
            logits(i, q_heads, 1)
            consume(i - 1, 0, False)
            consume(i, 1, True)

        @pl.when(i % 2 == 0)
        def _():
            consume(i, 0, True)

        acc_a, acc_b = acc_scr[0][...], acc_scr[1][...]
        o_a = acc_a / pltpu.roll(acc_a, FA_HEAD_DIM, axis=1)
        o_b = acc_b / pltpu.roll(acc_b, FA_HEAD_DIM, axis=1)
        o_ref[0, pl.ds(pl.multiple_of(i * tile, tile), tile), :] = jnp.where(first, o_a, o_b)
        return 0

    lax.fori_loop(0, n_tiles, q_body, 0)


def _flash_attention(q_pm, k_pm, v_pm, c_t, batch, seq):
    tile = ATT_TILE
    nt = seq // tile
    pm = lambda b, j: (b, j, 0, 0)
    return pl.pallas_call(
        functools.partial(_flash_kernel, tile=tile, n_tiles=nt),
        grid=(batch, HEAD_PAIRS),
        in_specs=[pl.BlockSpec((1, 1, seq, LANES), pm),
                  pl.BlockSpec((1, 1, nt, LANES, tile), lambda b, j: (b, j, 0, 0, 0)),
                  pl.BlockSpec((1, 1, seq, LANES), pm),
                  pl.BlockSpec((1, 1, nt, 2, tile), lambda b, j: (b, j, 0, 0, 0))],
        out_specs=pl.BlockSpec((1, seq, LANES), lambda b, j: (b, 0, j)),
        out_shape=jax.ShapeDtypeStruct((batch, seq, FA_WIDTH), F32),
        scratch_shapes=([pltpu.VMEM((tile, tile), F32)] * 4
                        + [pltpu.VMEM((tile, LANES), F32)] * 4
                        + [pltpu.VMEM((tile, tile), BF16)] * 2
                        + [pltpu.VMEM((tile, LANES), F32)] * 2
                        + [pltpu.VMEM((tile, LANES), F32)] * 2),
        compiler_params=_params(2),
        name="flash_prompt",
    )(q_pm, k_pm, v_pm, c_t)


def _decode_kernel(pt_ref, q_ref, kn_ref, vn_ref, lfn_ref, *refs, n_pages):
    del pt_ref
    k_refs, v_refs, lf_refs = refs[:n_pages], refs[n_pages:2 * n_pages], refs[2 * n_pages:3 * n_pages]
    o_ref = refs[3 * n_pages]
    head = lax.broadcasted_iota(jnp.int32, (FA_HEADS, FA_WIDTH), 0)
    col = lax.broadcasted_iota(jnp.int32, (FA_HEADS, FA_WIDTH), 1)
    own = (col >= head * FA_HEAD_DIM) & (col < (head + 1) * FA_HEAD_DIM)
    q_rows = jnp.where(own, jnp.broadcast_to(q_ref[0].astype(F32), (FA_HEADS, FA_WIDTH)), 0.0)
    q_bf = q_rows.astype(BF16)

    r_i = lax.broadcasted_iota(jnp.int32, (PAGE_SIZE, PAGE_SIZE), 0)
    c_i = lax.broadcasted_iota(jnp.int32, (PAGE_SIZE, PAGE_SIZE), 1)
    upper = jnp.where(r_i <= c_i, 1.0, 0.0).astype(BF16)
    lf_all = jnp.concatenate([lf_refs[i][0] for i in range(n_pages)], axis=0)
    cum_all = _split3_dot(lf_all, upper)
    cums, offs = [], []
    off = jnp.zeros((FA_HEADS, 1), F32)
    for i in range(n_pages):
        cum_i = cum_all[i * FA_HEADS:(i + 1) * FA_HEADS]
        cums.append(cum_i)
        offs.append(off)
        off = off + cum_i[:, PAGE_SIZE - 1:PAGE_SIZE]
    c_query = off + lfn_ref[0]

    s_parts = []
    for i in range(n_pages):
        s_i = _dot(q_bf, k_refs[i][0].astype(BF16))
        s_parts.append(s_i + (c_query - (offs[i] + cums[i])))
    s = jnp.concatenate(s_parts, axis=1)
    s_new = jnp.sum(q_rows * kn_ref[0], axis=1, keepdims=True)
    m = jnp.maximum(jnp.max(s, axis=1, keepdims=True), s_new)
    p = jnp.exp(s - m)
    p_new = jnp.exp(s_new - m)
    l = jnp.sum(p, axis=1, keepdims=True) + p_new
    pb = p.astype(BF16)
    acc = p_new * vn_ref[0]
    for i in range(n_pages):
        acc = acc + _dot_nt(pb[:, i * PAGE_SIZE:(i + 1) * PAGE_SIZE], v_refs[i][0].astype(BF16))
    acc = acc / l
    o_ref[0] = jnp.sum(jnp.where(own, acc, 0.0), axis=0, keepdims=True)


def _decode_attention(q, k_new, v_new, lf_new, cache_kt, cache_vt, cache_lft, page_table):
    n, n_pages = page_table.shape

    def page_map(i):
        return lambda b, pt: (pt[b, i], 0, 0)

    row = lambda b, pt: (b, 0, 0)
    in_specs = [pl.BlockSpec((1, 1, FA_WIDTH), row), pl.BlockSpec((1, 1, FA_WIDTH), row),
                pl.BlockSpec((1, 1, FA_WIDTH), row), pl.BlockSpec((1, FA_HEADS, 1), row)]
    in_specs += [pl.BlockSpec((1, FA_WIDTH, PAGE_SIZE), page_map(i)) for i in range(n_pages)]
    in_specs += [pl.BlockSpec((1, FA_WIDTH, PAGE_SIZE), page_map(i)) for i in range(n_pages)]
    in_specs += [pl.BlockSpec((1, FA_HEADS, PAGE_SIZE), page_map(i)) for i in range(n_pages)]
    out = pl.pallas_call(
        functools.partial(_decode_kernel, n_pages=n_pages),
        grid_spec=pltpu.PrefetchScalarGridSpec(
            num_scalar_prefetch=1, grid=(n,), in_specs=in_specs,
            out_specs=pl.BlockSpec((1, 1, FA_WIDTH), row)),
        out_shape=jax.ShapeDtypeStruct((n, 1, FA_WIDTH), F32),
        compiler_params=_params(1),
        name="decode_attention",
    )(page_table, q.reshape(n, 1, FA_WIDTH), k_new.reshape(n, 1, FA_WIDTH), v_new.reshape(n, 1, FA_WIDTH),
      lf_new.reshape(n, FA_HEADS, 1), *([cache_kt] * n_pages), *([cache_vt] * n_pages), *([cache_lft] * n_pages))
    return out.reshape(n, FA_WIDTH)


def kernel(x_prompt, x_sample, c_prompt, c_sample, cache_k, cache_v, cache_logf, page_table, w_ada, b_ada, ln_g, ln_b, cm_w_in, cm_b_in, cm_ln_v_g, cm_ln_v_b, cm_w_s, cm_b_s, cm_w_out, cm_b_out, kv_w, kv_b, fa_w_in, fa_b_in, fa_w_out, fa_b_out):
    batch, seq, _ = x_prompt.shape
    n_dec = x_sample.shape[0]

    n_c = batch + n_dec
    c_all = jnp.pad(jnp.concatenate([c_prompt, c_sample], axis=0), ((0, (-n_c) % 8), (0, 0)))
    mod = _ada_modulation(c_all, w_ada, b_ada)
    mod_p = mod[:, :batch].reshape(DEPTH, batch, 1, 3 * D_MODEL)
    mod_s = mod[:, batch:n_c].reshape(DEPTH, 1, n_dec, 3 * D_MODEL)

    xp = x_prompt.reshape(batch * seq, D_MODEL)
    xs = x_sample.reshape(n_dec, D_MODEL)
    chunk_rows = []
    for layer in range(N_A_LAYERS):
        args = (cm_w_in[layer], cm_b_in[layer], cm_ln_v_g[layer], cm_ln_v_b[layer], cm_w_s[layer], cm_b_s[layer],
                cm_w_out[layer], cm_b_out[layer], ln_g[layer], ln_b[layer])
        (xp,) = _chunk_mlp_layer(xp, mod_p[layer], *args, single_row_chunks=False)
        xs, vn_s = _chunk_mlp_layer(xs, mod_s[layer], *args, single_row_chunks=True)
        chunk_rows.append(vn_s)

    kt_p, vt_p, lft_p, kb_p, vb_p, ct_p = _kv_prompt(xp, kv_w, kv_b, batch, seq)
    kt_s, vt_s, lft_s, k_s, v_s, lf_s = _kv_sample(xs, kv_w, kv_b)

    n_phys = cache_k.shape[0]
    cache_kt = jnp.transpose(cache_k, (0, 2, 3, 1)).reshape(n_phys, FA_WIDTH, PAGE_SIZE)
    cache_vt = jnp.transpose(cache_v, (0, 2, 3, 1)).reshape(n_phys, FA_WIDTH, PAGE_SIZE)
    cache_lft = jnp.swapaxes(cache_logf, 1, 2)

    for j in range(DEPTH - N_A_LAYERS):
        layer = N_A_LAYERS + j
        q_pm, gz_p = _q_proj(xp, mod_p[layer], fa_w_in[j], fa_b_in[j], batch=batch, seq=seq)
        o_p = _flash_attention(q_pm, kb_p, vb_p, ct_p, batch, seq).reshape(batch * seq, FA_WIDTH)
        xp = _out_proj(o_p, gz_p, xp, mod_p[layer], fa_w_out[j], fa_b_out[j], ln_g[layer], ln_b[layer], tm=ROW_TILE)

        q_s, gz_s = _q_proj(xs, mod_s[layer], fa_w_in[j], fa_b_in[j])
        o_s = _decode_attention(q_s, k_s, v_s, lf_s, cache_kt, cache_vt, cache_lft, page_table)
        xs = _out_proj(o_s, gz_s, xs, mod_s[layer], fa_w_out[j], fa_b_out[j], ln_g[layer], ln_b[layer], tm=n_dec)

    y_prompt = xp.reshape(batch, seq, D_MODEL)
    y_sample = xs.reshape(n_dec, 1, D_MODEL)
    k_prompt = jnp.transpose(kt_p.reshape(batch, FA_HEADS, FA_HEAD_DIM, seq), (0, 3, 1, 2))
    v_prompt = jnp.transpose(vt_p.reshape(batch, FA_HEADS, FA_HEAD_DIM, seq), (0, 3, 1, 2))
    logf_prompt = jnp.transpose(lft_p, (0, 2, 1))
    k_sample = jnp.transpose(kt_s.reshape(1, FA_HEADS, FA_HEAD_DIM, n_dec), (3, 0, 1, 2))
    v_sample = jnp.transpose(vt_s.reshape(1, FA_HEADS, FA_HEAD_DIM, n_dec), (3, 0, 1, 2))
    logf_sample = jnp.transpose(lft_s, (2, 0, 1))
    chunk_v_sample = jnp.stack(chunk_rows).reshape(N_A_LAYERS, n_dec, 1, CM_WIDTH)
    return (y_prompt, y_sample, k_prompt, v_prompt, logf_prompt, k_sample, v_sample, logf_sample, chunk_v_sample)
```

```python
import functools

import jax
import jax.numpy as jnp
from jax import lax
from jax.experimental import pallas as pl
from jax.experimental.pallas import tpu as pltpu

F32 = jnp.float32
BF16 = jnp.bfloat16

D_MODEL = 1024
DEPTH = 4
N_A_LAYERS = 2
CHUNK = 128
CM_WIDTH = 2048
CM_GROUPS = 8
CM_GROUP_DIM = CM_WIDTH // CM_GROUPS
FA_HEADS = 16
FA_HEAD_DIM = 64
FA_WIDTH = FA_HEADS * FA_HEAD_DIM
HEAD_PAIRS = FA_HEADS // 2
PAGE_SIZE = 128
DEEPNORM_ALPHA = (2.0 * DEPTH) ** 0.25
LN_EPS = 1e-5
LANES = 128
NEG_BIG = -1e30
VMEM_LIMIT = 56 * 1024 * 1024

ROW_TILE = 512
ATT_TILE = 512
FLASH_ROWS = 32
LOG2E = 1.4426950408889634


def _params(n_axes, vmem=VMEM_LIMIT):
    return pltpu.CompilerParams(dimension_semantics=("arbitrary",) * n_axes, vmem_limit_bytes=vmem)


def _dot(a, b):
    return jnp.dot(a, b, preferred_element_type=F32)


def _dot_nt(a, b):
    return lax.dot_general(a, b, (((1,), (1,)), ((), ())), preferred_element_type=F32)


def _layer_norm(x, g, b):
    mu = jnp.mean(x, axis=-1, keepdims=True)
    d = x - mu
    var = jnp.mean(d * d, axis=-1, keepdims=True)
    return d * lax.rsqrt(var + LN_EPS) * g + b


def _log_sigmoid(x):
    return -(jnp.maximum(-x, 0.0) + jnp.log1p(jnp.exp(-jnp.abs(x))))


def _split3_dot(x, w):
    hi = x.astype(BF16)
    r1 = x - hi.astype(F32)
    mid = r1.astype(BF16)
    lo = (r1 - mid.astype(F32)).astype(BF16)
    return _dot(hi, w) + _dot(mid, w) + _dot(lo, w)


def _const_spec(shape):
    nd = len(shape)
    return pl.BlockSpec(shape, lambda *_: (0,) * nd, pipeline_mode=pl.Buffered(1))


def _ada_kernel(c_ref, w_ref, b_ref, o_ref):
    a = jax.nn.silu(c_ref[...]).astype(BF16)
    o_ref[0] = _dot(a, w_ref[0].astype(BF16)) + b_ref[0]


def _ada_modulation(c_all, w_ada, b_ada):
    n = c_all.shape[0]
    nt = 3 * D_MODEL // D_MODEL
    return pl.pallas_call(
        _ada_kernel,
        grid=(DEPTH, nt),
        in_specs=[pl.BlockSpec((n, D_MODEL), lambda l, j: (0, 0)),
                  pl.BlockSpec((1, D_MODEL, D_MODEL), lambda l, j: (l, 0, j)),
                  pl.BlockSpec((1, 1, D_MODEL), lambda l, j: (l, 0, j))],
        out_specs=pl.BlockSpec((1, n, D_MODEL), lambda l, j: (l, 0, j)),
        out_shape=jax.ShapeDtypeStruct((DEPTH, n, 3 * D_MODEL), F32),
        compiler_params=_params(2),
        name="ada_modulation",
    )(c_all, w_ada, b_ada.reshape(DEPTH, 1, 3 * D_MODEL))


def _chunk_mlp_kernel(x_ref, mod_ref, win_ref, bin_ref, lvg_ref, lvb_ref, ws_ref, bs_ref,
                      wout_ref, bout_ref, lng_ref, lnb_ref, *rest, single_row_chunks):
    if single_row_chunks:
        o_ref, vn_ref, t_scr = rest
    else:
        o_ref, t_scr = rest
    rows = x_ref.shape[0]
    x = x_ref[...]
    m = mod_ref[0]
    shift, scale, gate = m[:, :D_MODEL], m[:, D_MODEL:2 * D_MODEL], m[:, 2 * D_MODEL:]
    h = (x * (1.0 + scale) + shift).astype(BF16)

    v = jax.nn.gelu(_dot(h, win_ref[:, CM_WIDTH:2 * CM_WIDTH]) + bin_ref[:, CM_WIDTH:2 * CM_WIDTH])
    vn = _layer_norm(v, lvg_ref[...], lvb_ref[...])
    if single_row_chunks:
        vn_ref[...] = vn
    else:
        r_i = lax.broadcasted_iota(jnp.int32, (CHUNK, CHUNK), 0)
        c_i = lax.broadcasted_iota(jnp.int32, (CHUNK, CHUNK), 1)
        causal = c_i <= r_i

    for g in range(CM_GROUPS):
        lo, hi = g * CM_GROUP_DIM, (g + 1) * CM_GROUP_DIM
        vn_g = vn[:, lo:hi]
        if single_row_chunks:
            sg = vn_g * ws_ref[:, lo:hi] + bs_ref[:, lo:hi]
        else:
            w_g = jnp.where(causal, ws_ref[g], 0.0).astype(BF16)
            b_g = bs_ref[:, g:g + 1]
            vb = vn_g.astype(BF16)
            sg = jnp.concatenate(
                [_dot(w_g, vb[c * CHUNK:(c + 1) * CHUNK]) + b_g for c in range(rows // CHUNK)], axis=0)
        u_g = jax.nn.gelu(_dot(h, win_ref[:, lo:hi]) + bin_ref[:, lo:hi])
        z_g = _dot(h, win_ref[:, 2 * CM_WIDTH + lo:2 * CM_WIDTH + hi]) + bin_ref[:, 2 * CM_WIDTH + lo:2 * CM_WIDTH + hi]
        t_scr[:, lo:hi] = (u_g * sg * jax.nn.silu(z_g)).astype(BF16)

    y = _dot(t_scr[...], wout_ref[...]) + bout_ref[...]
    r = DEEPNORM_ALPHA * x + (1.0 + gate) * y
    o_ref[...] = _layer_norm(r, lng_ref[...], lnb_ref[...])


def _chunk_mlp_layer(x, mod, w_in, b_in, lvg, lvb, w_s, b_s, w_out, b_out, ln_g, ln_b, *, single_row_chunks):
    rows = x.shape[0]
    nb, mrows = mod.shape[0], mod.shape[1]
    tm = rows if single_row_chunks else ROW_TILE
    tiles_per_batch = rows // nb // tm
    if single_row_chunks:
        ws_arg = jnp.repeat(w_s[:, 0, 0], CM_GROUP_DIM)[None]
        bs_arg = jnp.repeat(b_s[:, 0], CM_GROUP_DIM)[None]
    else:
        ws_arg, bs_arg = w_s, b_s.T
    out_shape = [jax.ShapeDtypeStruct((rows, D_MODEL), F32)]
    out_specs = [pl.BlockSpec((tm, D_MODEL), lambda i: (i, 0))]
    if single_row_chunks:
        out_shape.append(jax.ShapeDtypeStruct((rows, CM_WIDTH), F32))
        out_specs.append(pl.BlockSpec((tm, CM_WIDTH), lambda i: (i, 0)))
    res = pl.pallas_call(
        functools.partial(_chunk_mlp_kernel, single_row_chunks=single_row_chunks),
        grid=(rows // tm,),
        in_specs=[pl.BlockSpec((tm, D_MODEL), lambda i: (i, 0)),
                  pl.BlockSpec((1, mrows, 3 * D_MODEL), lambda i: (i // tiles_per_batch, 0, 0)),
                  _const_spec((D_MODEL, 3 * CM_WIDTH)), _const_spec((1, 3 * CM_WIDTH)),
                  _const_spec((1, CM_WIDTH)), _const_spec((1, CM_WIDTH)),
                  _const_spec(ws_arg.shape), _const_spec(bs_arg.shape),
                  _const_spec((CM_WIDTH, D_MODEL)), _const_spec((1, D_MODEL)),
                  _const_spec((1, D_MODEL)), _const_spec((1, D_MODEL))],
        out_specs=out_specs,
        out_shape=out_shape,
        scratch_shapes=[pltpu.VMEM((tm, CM_WIDTH), BF16)],
        compiler_params=_params(1),
        name="chunk_mlp_sample" if single_row_chunks else "chunk_mlp_prompt",
    )(x, mod, w_in.astype(BF16), b_in[None], lvg[None], lvb[None], ws_arg, bs_arg,
      w_out.astype(BF16), b_out[None], ln_g[None], ln_b[None])
    return res


def _kv_kernel(x_ref, wkv_ref, bkv_ref, wkvt_ref, bkvt_ref, wft_ref, bft_ref, *rest, prompt):
    if prompt:
        kt_ref, vt_ref, lft_ref, kb_ref, vb_ref, ct_ref, carry_scr = rest
    else:
        wf_ref, bf_ref, kt_ref, vt_ref, lft_ref, k_ref, v_ref, lf_ref = rest
    rows = x_ref.shape[0]
    xb = x_ref[...].astype(BF16)
    kvt = _dot_nt(wkvt_ref[...], xb) + bkvt_ref[...]
    kt_ref[0] = kvt[:FA_WIDTH]
    vt_ref[0] = kvt[FA_WIDTH:]
    lft = _log_sigmoid(_dot_nt(wft_ref[...], xb) + bft_ref[...])
    lft_ref[0] = lft
    if not prompt:
        kv = _dot(xb, wkv_ref[...]) + bkv_ref[...]
        k_ref[...] = kv[:, :FA_WIDTH]
        v_ref[...] = kv[:, FA_WIDTH:]
        lf_ref[...] = _log_sigmoid(_dot(xb, wf_ref[...]) + bf_ref[...])[:, :FA_HEADS]
    else:
        ktb = kvt[:FA_WIDTH].astype(BF16)
        vb = (_dot(xb, wkv_ref[...]) + bkv_ref[...]).astype(BF16)
        for j in range(HEAD_PAIRS):
            kb_ref[0, j, 0] = ktb[j * LANES:(j + 1) * LANES, :]
            vb_ref[0, j] = vb[:, j * LANES:(j + 1) * LANES]
        r_i = lax.broadcasted_iota(jnp.int32, (rows, rows), 0)
        c_i = lax.broadcasted_iota(jnp.int32, (rows, rows), 1)
        upper = jnp.where(r_i <= c_i, 1.0, 0.0).astype(BF16)

        @pl.when(pl.program_id(1) == 0)
        def _():
            carry_scr[...] = jnp.zeros_like(carry_scr)

        ct = _split3_dot(lft, upper) + carry_scr[:, 0:1]
        carry_scr[...] = jnp.broadcast_to(ct[:, rows - 1:rows], carry_scr.shape)
        ct2 = ct * LOG2E
        for j in range(HEAD_PAIRS):
            ct_ref[0, j, 0] = ct2[2 * j:2 * j + 2, :]


def _kv_weights(kv_w, kv_b):
    wkv = kv_w[:, :2 * FA_WIDTH].astype(BF16)
    bkv = kv_b[None, :2 * FA_WIDTH]
    wkvt = kv_w[:, :2 * FA_WIDTH].T.astype(BF16)
    bkvt = kv_b[:2 * FA_WIDTH, None]
    wft = kv_w[:, 2 * FA_WIDTH:].T.astype(BF16)
    bft = kv_b[2 * FA_WIDTH:, None]
    return wkv, bkv, wkvt, bkvt, wft, bft


def _kv_prompt(x, kv_w, kv_b, batch, seq):
    tm = ATT_TILE
    nt = seq // tm
    wkv, bkv, wkvt, bkvt, wft, bft = _kv_weights(kv_w, kv_b)
    weights = (wkv[:, FA_WIDTH:], bkv[:, FA_WIDTH:], wkvt, bkvt, wft, bft)
    return pl.pallas_call(
        functools.partial(_kv_kernel, prompt=True),
        grid=(batch, nt),
        in_specs=[pl.BlockSpec((tm, D_MODEL), lambda b, i: (b * nt + i, 0))] + [_const_spec(w.shape) for w in weights],
        out_specs=[pl.BlockSpec((1, FA_WIDTH, tm), lambda b, i: (b, 0, i)),
                   pl.BlockSpec((1, FA_WIDTH, tm), lambda b, i: (b, 0, i)),
                   pl.BlockSpec((1, FA_HEADS, tm), lambda b, i: (b, 0, i)),
                   pl.BlockSpec((1, HEAD_PAIRS, 1, LANES, tm), lambda b, i: (b, 0, i, 0, 0)),
                   pl.BlockSpec((1, HEAD_PAIRS, tm, LANES), lambda b, i: (b, 0, i, 0)),
                   pl.BlockSpec((1, HEAD_PAIRS, 1, 2, tm), lambda b, i: (b, 0, i, 0, 0))],
        out_shape=[jax.ShapeDtypeStruct((batch, FA_WIDTH, seq), F32),
                   jax.ShapeDtypeStruct((batch, FA_WIDTH, seq), F32),
                   jax.ShapeDtypeStruct((batch, FA_HEADS, seq), F32),
                   jax.ShapeDtypeStruct((batch, HEAD_PAIRS, nt, LANES, tm), BF16),
                   jax.ShapeDtypeStruct((batch, HEAD_PAIRS, seq, LANES), BF16),
                   jax.ShapeDtypeStruct((batch, HEAD_PAIRS, nt, 2, tm), F32)],
        scratch_shapes=[pltpu.VMEM((FA_HEADS, LANES), F32)],
        compiler_params=_params(2),
        name="kv_prompt",
    )(x, *weights)


def _kv_sample(x, kv_w, kv_b):
    rows = x.shape[0]
    weights = _kv_weights(kv_w, kv_b)
    wf = jnp.pad(kv_w[:, 2 * FA_WIDTH:], ((0, 0), (0, LANES - FA_HEADS))).astype(BF16)
    bf = jnp.pad(kv_b[2 * FA_WIDTH:], (0, LANES - FA_HEADS))[None]
    weights = weights + (wf, bf)
    full = lambda i: (0, 0)
    full3 = lambda i: (0, 0, 0)
    return pl.pallas_call(
        functools.partial(_kv_kernel, prompt=False),
        grid=(1,),
        in_specs=[pl.BlockSpec((rows, D_MODEL), full)] + [_const_spec(w.shape) for w in weights],
        out_specs=[pl.BlockSpec((1, FA_WIDTH, rows), full3), pl.BlockSpec((1, FA_WIDTH, rows), full3),
                   pl.BlockSpec((1, FA_HEADS, rows), full3),
                   pl.BlockSpec((rows, FA_WIDTH), full), pl.BlockSpec((rows, FA_WIDTH), full),
                   pl.BlockSpec((rows, FA_HEADS), full)],
        out_shape=[jax.ShapeDtypeStruct((1, FA_WIDTH, rows), F32), jax.ShapeDtypeStruct((1, FA_WIDTH, rows), F32),
                   jax.ShapeDtypeStruct((1, FA_HEADS, rows), F32),
                   jax.ShapeDtypeStruct((rows, FA_WIDTH), F32), jax.ShapeDtypeStruct((rows, FA_WIDTH), F32),
                   jax.ShapeDtypeStruct((rows, FA_HEADS), F32)],
        compiler_params=_params(1),
        name="kv_sample",
    )(x, *weights)


def _q_proj_kernel(x_ref, mod_ref, w_ref, b_ref, q_ref, gz_ref, *, pair_major):
    x = x_ref[...]
    m = mod_ref[0]
    h = (x * (1.0 + m[:, D_MODEL:2 * D_MODEL]) + m[:, :D_MODEL]).astype(BF16)
    p = _dot(h, w_ref[...]) + b_ref[...]
    if pair_major:
        q = (p[:, :FA_WIDTH] * (LOG2E * FA_HEAD_DIM ** -0.5)).astype(BF16)
        for j in range(HEAD_PAIRS):
            q_ref[0, j] = q[:, j * LANES:(j + 1) * LANES]
    else:
        q_ref[...] = (p[:, :FA_WIDTH] * (FA_HEAD_DIM ** -0.5)).astype(BF16)
    gz_ref[...] = jax.nn.silu(p[:, FA_WIDTH:])


def _q_proj(x, mod, w_in, b_in, *, batch=None, seq=None):
    rows = x.shape[0]
    nb, mrows = mod.shape[0], mod.shape[1]
    pair_major = batch is not None
    tm = ROW_TILE if pair_major else rows
    tpb = rows // nb // tm
    if pair_major:
        q_spec = pl.BlockSpec((1, HEAD_PAIRS, tm, LANES), lambda i: (i // tpb, 0, i % tpb, 0))
        q_shape = jax.ShapeDtypeStruct((batch, HEAD_PAIRS, seq, LANES), BF16)
    else:
        q_spec = pl.BlockSpec((tm, FA_WIDTH), lambda i: (i, 0))
        q_shape = jax.ShapeDtypeStruct((rows, FA_WIDTH), BF16)
    return pl.pallas_call(
        functools.partial(_q_proj_kernel, pair_major=pair_major),
        grid=(rows // tm,),
        in_specs=[pl.BlockSpec((tm, D_MODEL), lambda i: (i, 0)),
                  pl.BlockSpec((1, mrows, 3 * D_MODEL), lambda i: (i // tpb, 0, 0)),
                  _const_spec((D_MODEL, 2 * FA_WIDTH)), _const_spec((1, 2 * FA_WIDTH))],
        out_specs=[q_spec, pl.BlockSpec((tm, FA_WIDTH), lambda i: (i, 0))],
        out_shape=[q_shape, jax.ShapeDtypeStruct((rows, FA_WIDTH), F32)],
        compiler_params=_params(1),
        name="q_proj_prompt" if pair_major else "q_proj_sample",
    )(x, mod, w_in.astype(BF16), b_in[None])


def _out_proj_kernel(o_ref, gz_ref, x_ref, mod_ref, w_ref, b_ref, g_ref, bb_ref, y_ref):
    t = (o_ref[...] * gz_ref[...]).astype(BF16)
    y = _dot(t, w_ref[...]) + b_ref[...]
    gate = mod_ref[0][:, 2 * D_MODEL:]
    r = DEEPNORM_ALPHA * x_ref[...] + (1.0 + gate) * y
    y_ref[...] = _layer_norm(r, g_ref[...], bb_ref[...])


def _out_proj(o, gz, x, mod, w_out, b_out, ln_g, ln_b, *, tm):
    rows = x.shape[0]
    nb, mrows = mod.shape[0], mod.shape[1]
    tpb = rows // nb // tm
    row = lambda i: (i, 0)
    return pl.pallas_call(
        _out_proj_kernel,
        grid=(rows // tm,),
        in_specs=[pl.BlockSpec((tm, FA_WIDTH), row), pl.BlockSpec((tm, FA_WIDTH), row),
                  pl.BlockSpec((tm, D_MODEL), row),
                  pl.BlockSpec((1, mrows, 3 * D_MODEL), lambda i: (i // tpb, 0, 0)),
                  _const_spec((FA_WIDTH, D_MODEL)), _const_spec((1, D_MODEL)),
                  _const_spec((1, D_MODEL)), _const_spec((1, D_MODEL))],
        out_specs=pl.BlockSpec((tm, D_MODEL), row),
        out_shape=jax.ShapeDtypeStruct((rows, D_MODEL), F32),
        compiler_params=_params(1),
        name="out_proj",
    )(o, gz, x, mod, w_out.astype(BF16), b_out[None], ln_g[None], ln_b[None])


def _flash_q_tile(i, q2, k_ref, v_ref, c_ref, scr, tile):
    s_scr = (scr[0:2], scr[2:4])
    m_scr = (scr[4:6], scr[6:8])
    p_scr, alpha_scr, acc_scr = scr[8:10], scr[10:12], scr[12:14]
    lane = lax.broadcasted_iota(jnp.int32, (1, LANES), 1)
    first = lane < FA_HEAD_DIM
    sub8 = lax.broadcasted_iota(jnp.int32, (8, tile), 0)

    def logits(t, q_heads, buf):
        c2 = c_ref[0, 0, t]
        hi = c2.astype(BF16).astype(F32)
        mid = (c2 - hi).astype(BF16).astype(F32)
        lo = c2 - hi - mid
        pieces = (hi[0:1], mid[0:1], lo[0:1], hi[1:2], mid[1:2], lo[1:2])
        c_rows = jnp.zeros((8, tile), F32)
        for r, piece in enumerate(pieces):
            c_rows = jnp.where(sub8 == r, piece, c_rows)
        c_block = jnp.concatenate([c_rows, jnp.zeros((LANES - 8, tile), F32)], axis=0).astype(BF16)
        k_aug = jnp.concatenate([k_ref[0, 0, t], c_block], axis=0)
        for a in range(2):
            s_scr[buf][a][...] = _dot(q_heads[a], k_aug)

    def softmax_blocks(a, buf, masked):
        for r0 in range(0, tile, FLASH_ROWS):
            rows = slice(r0, r0 + FLASH_ROWS)
            ncols = min(tile, -(-(r0 + FLASH_ROWS) // LANES) * LANES) if masked else tile
            s = s_scr[buf][a][rows, :ncols]
            if masked:
                r_i = r0 + lax.broadcasted_iota(jnp.int32, (FLASH_ROWS, ncols), 0)
                c_i = lax.broadcasted_iota(jnp.int32, (FLASH_ROWS, ncols), 1)
                s = jnp.where(c_i <= r_i, s, NEG_BIG)
            m_old = m_scr[buf][a][rows, :]
            m_new = jnp.maximum(m_old, jnp.max(s, axis=-1, keepdims=True))
            m_scr[1 - buf][a][rows, :] = m_new
            alpha_scr[a][rows, :] = jnp.exp2(m_old - m_new)
            m_wide = jnp.concatenate([m_new] * (ncols // LANES), axis=1)
            p_scr[a][rows, :ncols] = jnp.exp2((s - m_wide).astype(BF16))
            if ncols < tile:
                p_scr[a][rows, ncols:] = jnp.zeros((FLASH_ROWS, tile - ncols), BF16)

    def consume(t, buf, masked):
        v2 = v_ref[0, 0, pl.ds(pl.multiple_of(t * tile, tile), tile), :]
        one = jnp.ones_like(v2)
        v_heads = (jnp.where(first, v2, one), jnp.where(first, one, v2))
        for a in range(2):
            softmax_blocks(a, buf, masked)
            acc_scr[a][...] = alpha_scr[a][...] * acc_scr[a][...] + _dot(p_scr[a][...], v_heads[a])

    col_q = lax.broadcasted_iota(jnp.int32, (tile, LANES), 1)
    zero = jnp.zeros_like(q2)
    q_heads = (jnp.concatenate([jnp.where(first, q2, zero),
                                jnp.where(col_q < 3, -1.0, 0.0).astype(BF16)], axis=1),
               jnp.concatenate([jnp.where(first, zero, q2),
                                jnp.where((col_q >= 3) & (col_q < 6), -1.0, 0.0).astype(BF16)], axis=1))
    for a in range(2):
        m_scr[0][a][...] = jnp.full((tile, LANES), NEG_BIG, F32)
        acc_scr[a][...] = jnp.zeros((tile, LANES), F32)

    logits(0, q_heads, 0)

    def tile_pair(u, _):
        logits(2 * u + 1, q_heads, 1)
        consume(2 * u, 0, False)
        logits(2 * u + 2, q_heads, 0)
        consume(2 * u + 1, 1, False)
        return 0

    lax.fori_loop(0, i // 2, tile_pair, 0)

    @pl.when(i % 2 == 1)
    def _():
        logits(i, q_heads, 1)
        consume(i - 1, 0, False)
        consume(i, 1, True)

    @pl.when(i % 2 == 0)
    def _():
        consume(i, 0, True)

    acc_a, acc_b = acc_scr[0][...], acc_scr[1][...]
    o_a = acc_a / pltpu.roll(acc_a, FA_HEAD_DIM, axis=1)
    o_b = acc_b / pltpu.roll(acc_b, FA_HEAD_DIM, axis=1)
    return jnp.where(first, o_a, o_b)


def _decode_group(g, n_groups, q_ref, kn_ref, vn_ref, lfn_ref, k_refs, v_refs, lf_refs, o_ref,
                  m_scr, l_scr, acc_scr, off_scr):
    pg = len(k_refs)

    @pl.when(g == 0)
    def _():
        m_scr[...] = jnp.full(m_scr.shape, NEG_BIG, F32)
        l_scr[...] = jnp.zeros(l_scr.shape, F32)
        acc_scr[...] = jnp.zeros(acc_scr.shape, F32)
        off_scr[...] = jnp.zeros(off_scr.shape, F32)

    head = lax.broadcasted_iota(jnp.int32, (FA_HEADS, FA_WIDTH), 0)
    col = lax.broadcasted_iota(jnp.int32, (FA_HEADS, FA_WIDTH), 1)
    own = (col >= head * FA_HEAD_DIM) & (col < (head + 1) * FA_HEAD_DIM)
    q_rows = jnp.where(own, jnp.broadcast_to(q_ref[0].astype(F32), (FA_HEADS, FA_WIDTH)), 0.0)
    q_bf = q_rows.astype(BF16)

    r_i = lax.broadcasted_iota(jnp.int32, (PAGE_SIZE, PAGE_SIZE), 0)
    c_i = lax.broadcasted_iota(jnp.int32, (PAGE_SIZE, PAGE_SIZE), 1)
    upper = jnp.where(r_i <= c_i, 1.0, 0.0).astype(BF16)
    lf_all = jnp.concatenate([lf_refs[p][0] for p in range(pg)], axis=0)
    cum_all = _split3_dot(lf_all, upper)
    off = off_scr[:, 0:1]
    s_parts = []
    for p in range(pg):
        cum_p = cum_all[p * FA_HEADS:(p + 1) * FA_HEADS]
        s_parts.append(_dot(q_bf, k_refs[p][0].astype(BF16)) - (off + cum_p))
        off = off + cum_p[:, PAGE_SIZE - 1:PAGE_SIZE]
    off_scr[...] = jnp.broadcast_to(off, off_scr.shape)
    s = jnp.concatenate(s_parts, axis=1)
    m_old = m_scr[:, 0:1]
    m_new = jnp.maximum(m_old, jnp.max(s, axis=1, keepdims=True))
    alpha = jnp.exp(m_old - m_new)
    prob = jnp.exp(s - m_new)
    l_new = alpha * l_scr[:, 0:1] + jnp.sum(prob, axis=1, keepdims=True)
    pb = prob.astype(BF16)
    acc = alpha * acc_scr[...]
    for p in range(pg):
        acc = acc + _dot_nt(pb[:, p * PAGE_SIZE:(p + 1) * PAGE_SIZE], v_refs[p][0].astype(BF16))
    m_scr[...] = jnp.broadcast_to(m_new, m_scr.shape)
    l_scr[...] = jnp.broadcast_to(l_new, l_scr.shape)
    acc_scr[...] = acc

    @pl.when(g == n_groups - 1)
    def _():
        c_query = off + lfn_ref[0]
        s_new = jnp.sum(q_rows * kn_ref[0], axis=1, keepdims=True)
        m_past = m_new + c_query
        m_fin = jnp.maximum(m_past, s_new)
        w_past = jnp.exp(m_past - m_fin)
        w_new = jnp.exp(s_new - m_fin)
        o_full = (acc * w_past + w_new * vn_ref[0]) / (l_new * w_past + w_new)
        o_ref[0] = jnp.sum(jnp.where(own, o_full, 0.0), axis=0, keepdims=True)


def _attention_kernel(pt_ref, q_ref, k_ref, v_ref, c_ref, qs_ref, kn_ref, vn_ref, lfn_ref, *refs,
                      tile, pg, n_groups):
    del pt_ref
    k_refs, v_refs, lf_refs = refs[:pg], refs[pg:2 * pg], refs[2 * pg:3 * pg]
    o_ref, os_ref = refs[3 * pg:3 * pg + 2]
    scr = refs[3 * pg + 2:]
    i = pl.program_id(2)
    o_ref[0] = _flash_q_tile(i, q_ref[0, 0], k_ref, v_ref, c_ref, scr[:14], tile)
    step = (pl.program_id(0) * pl.num_programs(1) + pl.program_id(1)) * pl.num_programs(2) + i
    _decode_group(step % n_groups, n_groups, qs_ref, kn_ref, vn_ref, lfn_ref, k_refs, v_refs, lf_refs, os_ref,
                  *scr[14:])


def _attention_layer(q_pm, k_pm, v_pm, c_t, batch, seq,
                     q_s, k_new, v_new, lf_new, cache_kt, cache_vt, cache_lft, page_table):
    tile = ATT_TILE
    nt = seq // tile
    n, n_pages = page_table.shape
    steps = batch * HEAD_PAIRS * nt
    n_groups = steps // n
    pg = n_pages // n_groups
    assert n_groups * n == steps and pg * n_groups == n_pages

    def sample_of(b, j, i):
        return ((b * HEAD_PAIRS + j) * nt + i) // n_groups

    def page_map(p):
        def index(b, j, i, pt):
            step = (b * HEAD_PAIRS + j) * nt + i
            return (pt[step // n_groups, (step % n_groups) * pg + p], 0, 0)
        return index

    row = lambda b, j, i, pt: (sample_of(b, j, i), 0, 0)
    in_specs = [pl.BlockSpec((1, 1, tile, LANES), lambda b, j, i, pt: (b, j, i, 0)),
                pl.BlockSpec((1, 1, nt, LANES, tile), lambda b, j, i, pt: (b, j, 0, 0, 0)),
                pl.BlockSpec((1, 1, seq, LANES), lambda b, j, i, pt: (b, j, 0, 0)),
                pl.BlockSpec((1, 1, nt, 2, tile), lambda b, j, i, pt: (b, j, 0, 0, 0)),
                pl.BlockSpec((1, 1, FA_WIDTH), row), pl.BlockSpec((1, 1, FA_WIDTH), row),
                pl.BlockSpec((1, 1, FA_WIDTH), row), pl.BlockSpec((1, FA_HEADS, 1), row)]
    in_specs += [pl.BlockSpec((1, FA_WIDTH, PAGE_SIZE), page_map(p)) for p in range(pg)]
    in_specs += [pl.BlockSpec((1, FA_WIDTH, PAGE_SIZE), page_map(p)) for p in range(pg)]
    in_specs += [pl.BlockSpec((1, FA_HEADS, PAGE_SIZE), page_map(p)) for p in range(pg)]
    o_p, o_s = pl.pallas_call(
        functools.partial(_attention_kernel, tile=tile, pg=pg, n_groups=n_groups),
        grid_spec=pltpu.PrefetchScalarGridSpec(
            num_scalar_prefetch=1, grid=(batch, HEAD_PAIRS, nt), in_specs=in_specs,
            out_specs=[pl.BlockSpec((1, tile, LANES), lambda b, j, i, pt: (b, i, j)),
                       pl.BlockSpec((1, 1, FA_WIDTH), row)],
            scratch_shapes=([pltpu.VMEM((tile, tile), F32)] * 4
                            + [pltpu.VMEM((tile, LANES), F32)] * 4
                            + [pltpu.VMEM((tile, tile), BF16)] * 2
                            + [pltpu.VMEM((tile, LANES), F32)] * 2
                            + [pltpu.VMEM((tile, LANES), F32)] * 2
                            + [pltpu.VMEM((FA_HEADS, LANES), F32)] * 2
                            + [pltpu.VMEM((FA_HEADS, FA_WIDTH), F32)]
                            + [pltpu.VMEM((FA_HEADS, LANES), F32)])),
        out_shape=[jax.ShapeDtypeStruct((batch, seq, FA_WIDTH), F32),
                   jax.ShapeDtypeStruct((n, 1, FA_WIDTH), F32)],
        compiler_params=_params(3),
        name="attention_layer",
    )(page_table, q_pm, k_pm, v_pm, c_t,
      q_s.reshape(n, 1, FA_WIDTH), k_new.reshape(n, 1, FA_WIDTH), v_new.reshape(n, 1, FA_WIDTH),
      lf_new.reshape(n, FA_HEADS, 1), *([cache_kt] * pg), *([cache_vt] * pg), *([cache_lft] * pg))
    return o_p, o_s.reshape(n, FA_WIDTH)


def kernel(x_prompt, x_sample, c_prompt, c_sample, cache_k, cache_v, cache_logf, page_table, w_ada, b_ada, ln_g, ln_b, cm_w_in, cm_b_in, cm_ln_v_g, cm_ln_v_b, cm_w_s, cm_b_s, cm_w_out, cm_b_out, kv_w, kv_b, fa_w_in, fa_b_in, fa_w_out, fa_b_out):
    batch, seq, _ = x_prompt.shape
    n_dec = x_sample.shape[0]

    n_c = batch + n_dec
    c_all = jnp.pad(jnp.concatenate([c_prompt, c_sample], axis=0), ((0, (-n_c) % 8), (0, 0)))
    mod = _ada_modulation(c_all, w_ada, b_ada)
    mod_p = mod[:, :batch].reshape(DEPTH, batch, 1, 3 * D_MODEL)
    mod_s = mod[:, batch:n_c].reshape(DEPTH, 1, n_dec, 3 * D_MODEL)

    xp = x_prompt.reshape(batch * seq, D_MODEL)
    xs = x_sample.reshape(n_dec, D_MODEL)
    chunk_rows = []
    for layer in range(N_A_LAYERS):
        args = (cm_w_in[layer], cm_b_in[layer], cm_ln_v_g[layer], cm_ln_v_b[layer], cm_w_s[layer], cm_b_s[layer],
                cm_w_out[layer], cm_b_out[layer], ln_g[layer], ln_b[layer])
        (xp,) = _chunk_mlp_layer(xp, mod_p[layer], *args, single_row_chunks=False)
        xs, vn_s = _chunk_mlp_layer(xs, mod_s[layer], *args, single_row_chunks=True)
        chunk_rows.append(vn_s)

    kt_p, vt_p, lft_p, kb_p, vb_p, ct_p = _kv_prompt(xp, kv_w, kv_b, batch, seq)
    kt_s, vt_s, lft_s, k_s, v_s, lf_s = _kv_sample(xs, kv_w, kv_b)

    n_phys = cache_k.shape[0]
    cache_kt = jnp.transpose(cache_k, (0, 2, 3, 1)).reshape(n_phys, FA_WIDTH, PAGE_SIZE)
    cache_vt = jnp.transpose(cache_v, (0, 2, 3, 1)).reshape(n_phys, FA_WIDTH, PAGE_SIZE)
    cache_lft = jnp.swapaxes(cache_logf, 1, 2)

    for j in range(DEPTH - N_A_LAYERS):
        layer = N_A_LAYERS + j
        q_pm, gz_p = _q_proj(xp, mod_p[layer], fa_w_in[j], fa_b_in[j], batch=batch, seq=seq)
        q_s, gz_s = _q_proj(xs, mod_s[layer], fa_w_in[j], fa_b_in[j])
        o_p, o_s = _attention_layer(q_pm, kb_p, vb_p, ct_p, batch, seq,
                                    q_s, k_s, v_s, lf_s, cache_kt, cache_vt, cache_lft, page_table)
        xp = _out_proj(o_p.reshape(batch * seq, FA_WIDTH), gz_p, xp, mod_p[layer], fa_w_out[j], fa_b_out[j],
                       ln_g[layer], ln_b[layer], tm=ROW_TILE)
        xs = _out_proj(o_s, gz_s, xs, mod_s[layer], fa_w_out[j], fa_b_out[j], ln_g[layer], ln_b[layer], tm=n_dec)

    y_prompt = xp.reshape(batch, seq, D_MODEL)
    y_sample = xs.reshape(n_dec, 1, D_MODEL)
    k_prompt = jnp.transpose(kt_p.reshape(batch, FA_HEADS, FA_HEAD_DIM, seq), (0, 3, 1, 2))
    v_prompt = jnp.transpose(vt_p.reshape(batch, FA_HEADS, FA_HEAD_DIM, seq), (0, 3, 1, 2))
    logf_prompt = jnp.transpose(lft_p, (0, 2, 1))
    k_sample = jnp.transpose(kt_s.reshape(1, FA_HEADS, FA_HEAD_DIM, n_dec), (3, 0, 1, 2))
    v_sample = jnp.transpose(vt_s.reshape(1, FA_HEADS, FA_HEAD_DIM, n_dec), (3, 0, 1, 2))
    logf_sample = jnp.transpose(lft_s, (2, 0, 1))
    chunk_v_sample = jnp.stack(chunk_rows).reshape(N_A_LAYERS, n_dec, 1, CM_WIDTH)
    return (y_prompt, y_sample, k_prompt, v_prompt, logf_prompt, k_sample, v_sample, logf_sample, chunk_v_sample)
```

```python
import functools

import jax
import jax.numpy as jnp
from jax import lax
from jax.experimental import pallas as pl
from jax.experimental.pallas import tpu as pltpu

F32 = jnp.float32
BF16 = jnp.bfloat16

D_MODEL = 1024
DEPTH = 4
N_A_LAYERS = 2
CHUNK = 128
CM_WIDTH = 2048
CM_GROUPS = 8
CM_GROUP_DIM = CM_WIDTH // CM_GROUPS
FA_HEADS = 16
FA_HEAD_DIM = 64
FA_WIDTH = FA_HEADS * FA_HEAD_DIM
HEAD_PAIRS = FA_HEADS // 2
PAGE_SIZE = 128
DEEPNORM_ALPHA = (2.0 * DEPTH) ** 0.25
LN_EPS = 1e-5
LANES = 128
NEG_BIG = -1e30
VMEM_LIMIT = 56 * 1024 * 1024

ROW_TILE = 512
ATT_TILE = 512
FLASH_ROWS = 32
Q_TILES_PER_STEP = 2
LOG2E = 1.4426950408889634


def _params(n_axes, vmem=VMEM_LIMIT):
    return pltpu.CompilerParams(dimension_semantics=("arbitrary",) * n_axes, vmem_limit_bytes=vmem)


def _dot(a, b):
    return jnp.dot(a, b, preferred_element_type=F32)


def _dot_nt(a, b):
    return lax.dot_general(a, b, (((1,), (1,)), ((), ())), preferred_element_type=F32)


def _layer_norm(x, g, b):
    mu = jnp.mean(x, axis=-1, keepdims=True)
    d = x - mu
    var = jnp.mean(d * d, axis=-1, keepdims=True)
    return d * lax.rsqrt(var + LN_EPS) * g + b


def _log_sigmoid(x):
    return -(jnp.maximum(-x, 0.0) + jnp.log1p(jnp.exp(-jnp.abs(x))))


def _split3_dot(x, w):
    hi = x.astype(BF16)
    r1 = x - hi.astype(F32)
    mid = r1.astype(BF16)
    lo = (r1 - mid.astype(F32)).astype(BF16)
    return _dot(hi, w) + _dot(mid, w) + _dot(lo, w)


def _const_spec(shape):
    nd = len(shape)
    return pl.BlockSpec(shape, lambda *_: (0,) * nd, pipeline_mode=pl.Buffered(1))


def _ada_kernel(c_ref, w_ref, b_ref, o_ref):
    a = jax.nn.silu(c_ref[...]).astype(BF16)
    o_ref[0] = _dot(a, w_ref[0].astype(BF16)) + b_ref[0]


def _ada_modulation(c_all, w_ada, b_ada):
    n = c_all.shape[0]
    nt = 3 * D_MODEL // D_MODEL
    return pl.pallas_call(
        _ada_kernel,
        grid=(DEPTH, nt),
        in_specs=[pl.BlockSpec((n, D_MODEL), lambda l, j: (0, 0)),
                  pl.BlockSpec((1, D_MODEL, D_MODEL), lambda l, j: (l, 0, j)),
                  pl.BlockSpec((1, 1, D_MODEL), lambda l, j: (l, 0, j))],
        out_specs=pl.BlockSpec((1, n, D_MODEL), lambda l, j: (l, 0, j)),
        out_shape=jax.ShapeDtypeStruct((DEPTH, n, 3 * D_MODEL), F32),
        compiler_params=_params(2),
        name="ada_modulation",
    )(c_all, w_ada, b_ada.reshape(DEPTH, 1, 3 * D_MODEL))


def _chunk_mlp_kernel(x_ref, mod_ref, win_ref, bin_ref, lvg_ref, lvb_ref, ws_ref, bs_ref,
                      wout_ref, bout_ref, lng_ref, lnb_ref, *rest, single_row_chunks):
    if single_row_chunks:
        o_ref, vn_ref, t_scr = rest
    else:
        o_ref, t_scr = rest
    rows = x_ref.shape[0]
    x = x_ref[...]
    m = mod_ref[0]
    shift, scale, gate = m[:, :D_MODEL], m[:, D_MODEL:2 * D_MODEL], m[:, 2 * D_MODEL:]
    h = (x * (1.0 + scale) + shift).astype(BF16)

    v = jax.nn.gelu(_dot(h, win_ref[:, CM_WIDTH:2 * CM_WIDTH]) + bin_ref[:, CM_WIDTH:2 * CM_WIDTH])
    vn = _layer_norm(v, lvg_ref[...], lvb_ref[...])
    if single_row_chunks:
        vn_ref[...] = vn
    else:
        r_i = lax.broadcasted_iota(jnp.int32, (CHUNK, CHUNK), 0)
        c_i = lax.broadcasted_iota(jnp.int32, (CHUNK, CHUNK), 1)
        causal = c_i <= r_i

    for g in range(CM_GROUPS):
        lo, hi = g * CM_GROUP_DIM, (g + 1) * CM_GROUP_DIM
        vn_g = vn[:, lo:hi]
        if single_row_chunks:
            sg = vn_g * ws_ref[:, lo:hi] + bs_ref[:, lo:hi]
        else:
            w_g = jnp.where(causal, ws_ref[g], 0.0).astype(BF16)
            b_g = bs_ref[:, g:g + 1]
            vb = vn_g.astype(BF16)
            sg = jnp.concatenate(
                [_dot(w_g, vb[c * CHUNK:(c + 1) * CHUNK]) + b_g for c in range(rows // CHUNK)], axis=0)
        u_g = jax.nn.gelu(_dot(h, win_ref[:, lo:hi]) + bin_ref[:, lo:hi])
        z_g = _dot(h, win_ref[:, 2 * CM_WIDTH + lo:2 * CM_WIDTH + hi]) + bin_ref[:, 2 * CM_WIDTH + lo:2 * CM_WIDTH + hi]
        t_scr[:, lo:hi] = (u_g * sg * jax.nn.silu(z_g)).astype(BF16)

    y = _dot(t_scr[...], wout_ref[...]) + bout_ref[...]
    r = DEEPNORM_ALPHA * x + (1.0 + gate) * y
    o_ref[...] = _layer_norm(r, lng_ref[...], lnb_ref[...])


def _chunk_mlp_layer(x, mod, w_in, b_in, lvg, lvb, w_s, b_s, w_out, b_out, ln_g, ln_b, *, single_row_chunks):
    rows = x.shape[0]
    nb, mrows = mod.shape[0], mod.shape[1]
    tm = rows if single_row_chunks else ROW_TILE
    tiles_per_batch = rows // nb // tm
    if single_row_chunks:
        ws_arg = jnp.repeat(w_s[:, 0, 0], CM_GROUP_DIM)[None]
        bs_arg = jnp.repeat(b_s[:, 0], CM_GROUP_DIM)[None]
    else:
        ws_arg, bs_arg = w_s, b_s.T
    out_shape = [jax.ShapeDtypeStruct((rows, D_MODEL), F32)]
    out_specs = [pl.BlockSpec((tm, D_MODEL), lambda i: (i, 0))]
    if single_row_chunks:
        out_shape.append(jax.ShapeDtypeStruct((rows, CM_WIDTH), F32))
        out_specs.append(pl.BlockSpec((tm, CM_WIDTH), lambda i: (i, 0)))
    res = pl.pallas_call(
        functools.partial(_chunk_mlp_kernel, single_row_chunks=single_row_chunks),
        grid=(rows // tm,),
        in_specs=[pl.BlockSpec((tm, D_MODEL), lambda i: (i, 0)),
                  pl.BlockSpec((1, mrows, 3 * D_MODEL), lambda i: (i // tiles_per_batch, 0, 0)),
                  _const_spec((D_MODEL, 3 * CM_WIDTH)), _const_spec((1, 3 * CM_WIDTH)),
                  _const_spec((1, CM_WIDTH)), _const_spec((1, CM_WIDTH)),
                  _const_spec(ws_arg.shape), _const_spec(bs_arg.shape),
                  _const_spec((CM_WIDTH, D_MODEL)), _const_spec((1, D_MODEL)),
                  _const_spec((1, D_MODEL)), _const_spec((1, D_MODEL))],
        out_specs=out_specs,
        out_shape=out_shape,
        scratch_shapes=[pltpu.VMEM((tm, CM_WIDTH), BF16)],
        compiler_params=_params(1),
        name="chunk_mlp_sample" if single_row_chunks else "chunk_mlp_prompt",
    )(x, mod, w_in.astype(BF16), b_in[None], lvg[None], lvb[None], ws_arg, bs_arg,
      w_out.astype(BF16), b_out[None], ln_g[None], ln_b[None])
    return res


def _kv_kernel(x_ref, wkv_ref, bkv_ref, wkvt_ref, bkvt_ref, wft_ref, bft_ref, *rest, prompt):
    if prompt:
        kt_ref, vt_ref, lft_ref, kb_ref, vb_ref, ct_ref, carry_scr = rest
    else:
        wf_ref, bf_ref, kt_ref, vt_ref, lft_ref, k_ref, v_ref, lf_ref = rest
    rows = x_ref.shape[0]
    xb = x_ref[...].astype(BF16)
    kvt = _dot_nt(wkvt_ref[...], xb) + bkvt_ref[...]
    kt_ref[0] = kvt[:FA_WIDTH]
    vt_ref[0] = kvt[FA_WIDTH:]
    lft = _log_sigmoid(_dot_nt(wft_ref[...], xb) + bft_ref[...])
    lft_ref[0] = lft
    if not prompt:
        kv = _dot(xb, wkv_ref[...]) + bkv_ref[...]
        k_ref[...] = kv[:, :FA_WIDTH]
        v_ref[...] = kv[:, FA_WIDTH:]
        lf_ref[...] = _log_sigmoid(_dot(xb, wf_ref[...]) + bf_ref[...])[:, :FA_HEADS]
    else:
        ktb = kvt[:FA_WIDTH].astype(BF16)
        vb = (_dot(xb, wkv_ref[...]) + bkv_ref[...]).astype(BF16)
        for j in range(HEAD_PAIRS):
            kb_ref[0, j, 0] = ktb[j * LANES:(j + 1) * LANES, :]
            vb_ref[0, j] = vb[:, j * LANES:(j + 1) * LANES]
        r_i = lax.broadcasted_iota(jnp.int32, (rows, rows), 0)
        c_i = lax.broadcasted_iota(jnp.int32, (rows, rows), 1)
        upper = jnp.where(r_i <= c_i, 1.0, 0.0).astype(BF16)

        @pl.when(pl.program_id(1) == 0)
        def _():
            carry_scr[...] = jnp.zeros_like(carry_scr)

        ct = _split3_dot(lft, upper) + carry_scr[:, 0:1]
        carry_scr[...] = jnp.broadcast_to(ct[:, rows - 1:rows], carry_scr.shape)
        ct2 = ct * LOG2E
        for j in range(HEAD_PAIRS):
            ct_ref[0, j, 0] = ct2[2 * j:2 * j + 2, :]


def _kv_weights(kv_w, kv_b):
    wkv = kv_w[:, :2 * FA_WIDTH].astype(BF16)
    bkv = kv_b[None, :2 * FA_WIDTH]
    wkvt = kv_w[:, :2 * FA_WIDTH].T.astype(BF16)
    bkvt = kv_b[:2 * FA_WIDTH, None]
    wft = kv_w[:, 2 * FA_WIDTH:].T.astype(BF16)
    bft = kv_b[2 * FA_WIDTH:, None]
    return wkv, bkv, wkvt, bkvt, wft, bft


def _kv_prompt(x, kv_w, kv_b, batch, seq):
    tm = ATT_TILE
    nt = seq // tm
    wkv, bkv, wkvt, bkvt, wft, bft = _kv_weights(kv_w, kv_b)
    weights = (wkv[:, FA_WIDTH:], bkv[:, FA_WIDTH:], wkvt, bkvt, wft, bft)
    return pl.pallas_call(
        functools.partial(_kv_kernel, prompt=True),
        grid=(batch, nt),
        in_specs=[pl.BlockSpec((tm, D_MODEL), lambda b, i: (b * nt + i, 0))] + [_const_spec(w.shape) for w in weights],
        out_specs=[pl.BlockSpec((1, FA_WIDTH, tm), lambda b, i: (b, 0, i)),
                   pl.BlockSpec((1, FA_WIDTH, tm), lambda b, i: (b, 0, i)),
                   pl.BlockSpec((1, FA_HEADS, tm), lambda b, i: (b, 0, i)),
                   pl.BlockSpec((1, HEAD_PAIRS, 1, LANES, tm), lambda b, i: (b, 0, i, 0, 0)),
                   pl.BlockSpec((1, HEAD_PAIRS, tm, LANES), lambda b, i: (b, 0, i, 0)),
                   pl.BlockSpec((1, HEAD_PAIRS, 1, 2, tm), lambda b, i: (b, 0, i, 0, 0))],
        out_shape=[jax.ShapeDtypeStruct((batch, FA_WIDTH, seq), F32),
                   jax.ShapeDtypeStruct((batch, FA_WIDTH, seq), F32),
                   jax.ShapeDtypeStruct((batch, FA_HEADS, seq), F32),
                   jax.ShapeDtypeStruct((batch, HEAD_PAIRS, nt, LANES, tm), BF16),
                   jax.ShapeDtypeStruct((batch, HEAD_PAIRS, seq, LANES), BF16),
                   jax.ShapeDtypeStruct((batch, HEAD_PAIRS, nt, 2, tm), F32)],
        scratch_shapes=[pltpu.VMEM((FA_HEADS, LANES), F32)],
        compiler_params=_params(2),
        name="kv_prompt",
    )(x, *weights)


def _kv_sample(x, kv_w, kv_b):
    rows = x.shape[0]
    weights = _kv_weights(kv_w, kv_b)
    wf = jnp.pad(kv_w[:, 2 * FA_WIDTH:], ((0, 0), (0, LANES - FA_HEADS))).astype(BF16)
    bf = jnp.pad(kv_b[2 * FA_WIDTH:], (0, LANES - FA_HEADS))[None]
    weights = weights + (wf, bf)
    full = lambda i: (0, 0)
    full3 = lambda i: (0, 0, 0)
    return pl.pallas_call(
        functools.partial(_kv_kernel, prompt=False),
        grid=(1,),
        in_specs=[pl.BlockSpec((rows, D_MODEL), full)] + [_const_spec(w.shape) for w in weights],
        out_specs=[pl.BlockSpec((1, FA_WIDTH, rows), full3), pl.BlockSpec((1, FA_WIDTH, rows), full3),
                   pl.BlockSpec((1, FA_HEADS, rows), full3),
                   pl.BlockSpec((rows, FA_WIDTH), full), pl.BlockSpec((rows, FA_WIDTH), full),
                   pl.BlockSpec((rows, FA_HEADS), full)],
        out_shape=[jax.ShapeDtypeStruct((1, FA_WIDTH, rows), F32), jax.ShapeDtypeStruct((1, FA_WIDTH, rows), F32),
                   jax.ShapeDtypeStruct((1, FA_HEADS, rows), F32),
                   jax.ShapeDtypeStruct((rows, FA_WIDTH), F32), jax.ShapeDtypeStruct((rows, FA_WIDTH), F32),
                   jax.ShapeDtypeStruct((rows, FA_HEADS), F32)],
        compiler_params=_params(1),
        name="kv_sample",
    )(x, *weights)


def _q_proj_kernel(x_ref, mod_ref, w_ref, b_ref, q_ref, gz_ref, *, pair_major):
    x = x_ref[...]
    m = mod_ref[0]
    h = (x * (1.0 + m[:, D_MODEL:2 * D_MODEL]) + m[:, :D_MODEL]).astype(BF16)
    p = _dot(h, w_ref[...]) + b_ref[...]
    if pair_major:
        q = (p[:, :FA_WIDTH] * (LOG2E * FA_HEAD_DIM ** -0.5)).astype(BF16)
        for j in range(HEAD_PAIRS):
            q_ref[0, j] = q[:, j * LANES:(j + 1) * LANES]
    else:
        q_ref[...] = (p[:, :FA_WIDTH] * (FA_HEAD_DIM ** -0.5)).astype(BF16)
    gz_ref[...] = jax.nn.silu(p[:, FA_WIDTH:]).astype(gz_ref.dtype)


def _q_proj(x, mod, w_in, b_in, *, batch=None, seq=None):
    rows = x.shape[0]
    nb, mrows = mod.shape[0], mod.shape[1]
    pair_major = batch is not None
    tm = ROW_TILE if pair_major else rows
    tpb = rows // nb // tm
    if pair_major:
        q_spec = pl.BlockSpec((1, HEAD_PAIRS, tm, LANES), lambda i: (i // tpb, 0, i % tpb, 0))
        q_shape = jax.ShapeDtypeStruct((batch, HEAD_PAIRS, seq, LANES), BF16)
    else:
        q_spec = pl.BlockSpec((tm, FA_WIDTH), lambda i: (i, 0))
        q_shape = jax.ShapeDtypeStruct((rows, FA_WIDTH), BF16)
    return pl.pallas_call(
        functools.partial(_q_proj_kernel, pair_major=pair_major),
        grid=(rows // tm,),
        in_specs=[pl.BlockSpec((tm, D_MODEL), lambda i: (i, 0)),
                  pl.BlockSpec((1, mrows, 3 * D_MODEL), lambda i: (i // tpb, 0, 0)),
                  _const_spec((D_MODEL, 2 * FA_WIDTH)), _const_spec((1, 2 * FA_WIDTH))],
        out_specs=[q_spec, pl.BlockSpec((tm, FA_WIDTH), lambda i: (i, 0))],
        out_shape=[q_shape, jax.ShapeDtypeStruct((rows, FA_WIDTH), BF16)],
        compiler_params=_params(1),
        name="q_proj_prompt" if pair_major else "q_proj_sample",
    )(x, mod, w_in.astype(BF16), b_in[None])


def _out_proj_kernel(o_ref, gz_ref, x_ref, mod_ref, w_ref, b_ref, g_ref, bb_ref, y_ref):
    t = (o_ref[...].astype(F32) * gz_ref[...].astype(F32)).astype(BF16)
    y = _dot(t, w_ref[...]) + b_ref[...]
    gate = mod_ref[0][:, 2 * D_MODEL:]
    r = DEEPNORM_ALPHA * x_ref[...] + (1.0 + gate) * y
    y_ref[...] = _layer_norm(r, g_ref[...], bb_ref[...])


def _out_proj(o, gz, x, mod, w_out, b_out, ln_g, ln_b, *, tm):
    rows = x.shape[0]
    nb, mrows = mod.shape[0], mod.shape[1]
    tpb = rows // nb // tm
    row = lambda i: (i, 0)
    return pl.pallas_call(
        _out_proj_kernel,
        grid=(rows // tm,),
        in_specs=[pl.BlockSpec((tm, FA_WIDTH), row), pl.BlockSpec((tm, FA_WIDTH), row),
                  pl.BlockSpec((tm, D_MODEL), row),
                  pl.BlockSpec((1, mrows, 3 * D_MODEL), lambda i: (i // tpb, 0, 0)),
                  _const_spec((FA_WIDTH, D_MODEL)), _const_spec((1, D_MODEL)),
                  _const_spec((1, D_MODEL)), _const_spec((1, D_MODEL))],
        out_specs=pl.BlockSpec((tm, D_MODEL), row),
        out_shape=jax.ShapeDtypeStruct((rows, D_MODEL), F32),
        compiler_params=_params(1),
        name="out_proj",
    )(o, gz, x, mod, w_out.astype(BF16), b_out[None], ln_g[None], ln_b[None])


def _flash_q_tile(i, q2, k_ref, v_ref, c_ref, scr, tile):
    s_scr = (scr[0:2], scr[2:4])
    m_scr = (scr[4:6], scr[6:8])
    p_scr, alpha_scr, acc_scr = scr[8:10], scr[10:12], scr[12:14]
    lane = lax.broadcasted_iota(jnp.int32, (1, LANES), 1)
    first = lane < FA_HEAD_DIM
    sub8 = lax.broadcasted_iota(jnp.int32, (8, tile), 0)

    def logits(t, q_heads, buf):
        c2 = c_ref[0, 0, t]
        hi = c2.astype(BF16).astype(F32)
        mid = (c2 - hi).astype(BF16).astype(F32)
        lo = c2 - hi - mid
        pieces = (hi[0:1], mid[0:1], lo[0:1], hi[1:2], mid[1:2], lo[1:2])
        c_rows = jnp.zeros((8, tile), F32)
        for r, piece in enumerate(pieces):
            c_rows = jnp.where(sub8 == r, piece, c_rows)
        c_block = jnp.concatenate([c_rows, jnp.zeros((LANES - 8, tile), F32)], axis=0).astype(BF16)
        k_aug = jnp.concatenate([k_ref[0, 0, t], c_block], axis=0)
        for a in range(2):
            s_scr[buf][a][...] = _dot(q_heads[a], k_aug)

    def softmax_blocks(a, buf, masked):
        for r0 in range(0, tile, FLASH_ROWS):
            rows = slice(r0, r0 + FLASH_ROWS)
            ncols = min(tile, -(-(r0 + FLASH_ROWS) // LANES) * LANES) if masked else tile
            s = s_scr[buf][a][rows, :ncols]
            if masked:
                r_i = r0 + lax.broadcasted_iota(jnp.int32, (FLASH_ROWS, ncols), 0)
                c_i = lax.broadcasted_iota(jnp.int32, (FLASH_ROWS, ncols), 1)
                s = jnp.where(c_i <= r_i, s, NEG_BIG)
            m_old = m_scr[buf][a][rows, :]
            m_new = jnp.maximum(m_old, jnp.max(s, axis=-1, keepdims=True))
            m_scr[1 - buf][a][rows, :] = m_new
            alpha_scr[a][rows, :] = jnp.exp2(m_old - m_new)
            m_wide = jnp.concatenate([m_new] * (ncols // LANES), axis=1)
            p_scr[a][rows, :ncols] = jnp.exp2((s - m_wide).astype(BF16))
            if ncols < tile:
                p_scr[a][rows, ncols:] = jnp.zeros((FLASH_ROWS, tile - ncols), BF16)

    def consume(t, buf, masked):
        v2 = v_ref[0, 0, pl.ds(pl.multiple_of(t * tile, tile), tile), :]
        one = jnp.ones_like(v2)
        v_heads = (jnp.where(first, v2, one), jnp.where(first, one, v2))
        for a in range(2):
            softmax_blocks(a, buf, masked)
            acc_scr[a][...] = alpha_scr[a][...] * acc_scr[a][...] + _dot(p_scr[a][...], v_heads[a])

    col_q = lax.broadcasted_iota(jnp.int32, (tile, LANES), 1)
    zero = jnp.zeros_like(q2)
    q_heads = (jnp.concatenate([jnp.where(first, q2, zero),
                                jnp.where(col_q < 3, -1.0, 0.0).astype(BF16)], axis=1),
               jnp.concatenate([jnp.where(first, zero, q2),
                                jnp.where((col_q >= 3) & (col_q < 6), -1.0, 0.0).astype(BF16)], axis=1))
    for a in range(2):
        m_scr[0][a][...] = jnp.full((tile, LANES), NEG_BIG, F32)
        acc_scr[a][...] = jnp.zeros((tile, LANES), F32)

    logits(0, q_heads, 0)

    def tile_pair(u, _):
        logits(2 * u + 1, q_heads, 1)
        consume(2 * u, 0, False)
        logits(2 * u + 2, q_heads, 0)
        consume(2 * u + 1, 1, False)
        return 0

    lax.fori_loop(0, i // 2, tile_pair, 0)

    @pl.when(i % 2 == 1)
    def _():
        logits(i, q_heads, 1)
        consume(i - 1, 0, False)
        consume(i, 1, True)

    @pl.when(i % 2 == 0)
    def _():
        consume(i, 0, True)

    acc_a, acc_b = acc_scr[0][...], acc_scr[1][...]
    o_a = acc_a / pltpu.roll(acc_a, FA_HEAD_DIM, axis=1)
    o_b = acc_b / pltpu.roll(acc_b, FA_HEAD_DIM, axis=1)
    return jnp.where(first, o_a, o_b)


def _decode_group(g, n_groups, q_ref, kn_ref, vn_ref, lfn_ref, k_refs, v_refs, lf_refs, o_ref,
                  m_scr, l_scr, acc_scr, off_scr):
    pg = len(k_refs)

    @pl.when(g == 0)
    def _():
        m_scr[...] = jnp.full(m_scr.shape, NEG_BIG, F32)
        l_scr[...] = jnp.zeros(l_scr.shape, F32)
        acc_scr[...] = jnp.zeros(acc_scr.shape, F32)
        off_scr[...] = jnp.zeros(off_scr.shape, F32)

    head = lax.broadcasted_iota(jnp.int32, (FA_HEADS, FA_WIDTH), 0)
    col = lax.broadcasted_iota(jnp.int32, (FA_HEADS, FA_WIDTH), 1)
    own = (col >= head * FA_HEAD_DIM) & (col < (head + 1) * FA_HEAD_DIM)
    q_rows = jnp.where(own, jnp.broadcast_to(q_ref[0].astype(F32), (FA_HEADS, FA_WIDTH)), 0.0)
    q_bf = q_rows.astype(BF16)

    r_i = lax.broadcasted_iota(jnp.int32, (PAGE_SIZE, PAGE_SIZE), 0)
    c_i = lax.broadcasted_iota(jnp.int32, (PAGE_SIZE, PAGE_SIZE), 1)
    upper = jnp.where(r_i <= c_i, 1.0, 0.0).astype(BF16)
    lf_all = jnp.concatenate([lf_refs[p][0] for p in range(pg)], axis=0)
    cum_all = _split3_dot(lf_all, upper)
    off = off_scr[:, 0:1]
    s_parts = []
    for p in range(pg):
        cum_p = cum_all[p * FA_HEADS:(p + 1) * FA_HEADS]
        s_parts.append(_dot(q_bf, k_refs[p][0].astype(BF16)) - (off + cum_p))
        off = off + cum_p[:, PAGE_SIZE - 1:PAGE_SIZE]
    off_scr[...] = jnp.broadcast_to(off, off_scr.shape)
    s = jnp.concatenate(s_parts, axis=1)
    m_old = m_scr[:, 0:1]
    m_new = jnp.maximum(m_old, jnp.max(s, axis=1, keepdims=True))
    alpha = jnp.exp(m_old - m_new)
    prob = jnp.exp(s - m_new)
    l_new = alpha * l_scr[:, 0:1] + jnp.sum(prob, axis=1, keepdims=True)
    pb = prob.astype(BF16)
    acc = alpha * acc_scr[...]
    for p in range(pg):
        acc = acc + _dot_nt(pb[:, p * PAGE_SIZE:(p + 1) * PAGE_SIZE], v_refs[p][0].astype(BF16))
    m_scr[...] = jnp.broadcast_to(m_new, m_scr.shape)
    l_scr[...] = jnp.broadcast_to(l_new, l_scr.shape)
    acc_scr[...] = acc

    @pl.when(g == n_groups - 1)
    def _():
        c_query = off + lfn_ref[0]
        s_new = jnp.sum(q_rows * kn_ref[0], axis=1, keepdims=True)
        m_past = m_new + c_query
        m_fin = jnp.maximum(m_past, s_new)
        w_past = jnp.exp(m_past - m_fin)
        w_new = jnp.exp(s_new - m_fin)
        o_full = (acc * w_past + w_new * vn_ref[0]) / (l_new * w_past + w_new)
        o_ref[0] = jnp.sum(jnp.where(own, o_full, 0.0), axis=0, keepdims=True)


def _attention_kernel(pt_ref, q_ref, k_ref, v_ref, c_ref, qs_ref, kn_ref, vn_ref, lfn_ref, *refs,
                      tile, pg, n_groups):
    del pt_ref
    k_refs, v_refs, lf_refs = refs[:pg], refs[pg:2 * pg], refs[2 * pg:3 * pg]
    o_ref, os_ref = refs[3 * pg:3 * pg + 2]
    scr = refs[3 * pg + 2:]
    i2 = pl.program_id(2)

    def q_tile(h, _):
        rows = pl.ds(pl.multiple_of(h * tile, tile), tile)
        o = _flash_q_tile(Q_TILES_PER_STEP * i2 + h, q_ref[0, 0, rows, :], k_ref, v_ref, c_ref, scr[:14], tile)
        o_ref[0, rows, :] = o.astype(o_ref.dtype)
        return 0

    lax.fori_loop(0, Q_TILES_PER_STEP, q_tile, 0)
    step = (pl.program_id(0) * pl.num_programs(1) + pl.program_id(1)) * pl.num_programs(2) + i2
    _decode_group(step % n_groups, n_groups, qs_ref, kn_ref, vn_ref, lfn_ref, k_refs, v_refs, lf_refs, os_ref,
                  *scr[14:])


def _attention_layer(q_pm, k_pm, v_pm, c_t, batch, seq,
                     q_s, k_new, v_new, lf_new, cache_kt, cache_vt, cache_lft, page_table):
    tile = ATT_TILE
    nt = seq // tile
    qt = Q_TILES_PER_STEP
    nq = nt // qt
    n, n_pages = page_table.shape
    steps = batch * HEAD_PAIRS * nq
    n_groups = steps // n
    pg = n_pages // n_groups
    assert nq * qt == nt and n_groups * n == steps and pg * n_groups == n_pages

    def sample_of(b, j, i):
        return ((b * HEAD_PAIRS + j) * nq + i) // n_groups

    def page_map(p):
        def index(b, j, i, pt):
            step = (b * HEAD_PAIRS + j) * nq + i
            return (pt[step // n_groups, (step % n_groups) * pg + p], 0, 0)
        return index

    row = lambda b, j, i, pt: (sample_of(b, j, i), 0, 0)
    in_specs = [pl.BlockSpec((1, 1, qt * tile, LANES), lambda b, j, i, pt: (b, j, i, 0)),
                pl.BlockSpec((1, 1, nt, LANES, tile), lambda b, j, i, pt: (b, j, 0, 0, 0)),
                pl.BlockSpec((1, 1, seq, LANES), lambda b, j, i, pt: (b, j, 0, 0)),
                pl.BlockSpec((1, 1, nt, 2, tile), lambda b, j, i, pt: (b, j, 0, 0, 0)),
                pl.BlockSpec((1, 1, FA_WIDTH), row), pl.BlockSpec((1, 1, FA_WIDTH), row),
                pl.BlockSpec((1, 1, FA_WIDTH), row), pl.BlockSpec((1, FA_HEADS, 1), row)]
    in_specs += [pl.BlockSpec((1, FA_WIDTH, PAGE_SIZE), page_map(p)) for p in range(pg)]
    in_specs += [pl.BlockSpec((1, FA_WIDTH, PAGE_SIZE), page_map(p)) for p in range(pg)]
    in_specs += [pl.BlockSpec((1, FA_HEADS, PAGE_SIZE), page_map(p)) for p in range(pg)]
    o_p, o_s = pl.pallas_call(
        functools.partial(_attention_kernel, tile=tile, pg=pg, n_groups=n_groups),
        grid_spec=pltpu.PrefetchScalarGridSpec(
            num_scalar_prefetch=1, grid=(batch, HEAD_PAIRS, nq), in_specs=in_specs,
            out_specs=[pl.BlockSpec((1, qt * tile, LANES), lambda b, j, i, pt: (b, i, j)),
                       pl.BlockSpec((1, 1, FA_WIDTH), row)],
            scratch_shapes=([pltpu.VMEM((tile, tile), F32)] * 4
                            + [pltpu.VMEM((tile, LANES), F32)] * 4
                            + [pltpu.VMEM((tile, tile), BF16)] * 2
                            + [pltpu.VMEM((tile, LANES), F32)] * 2
                            + [pltpu.VMEM((tile, LANES), F32)] * 2
                            + [pltpu.VMEM((FA_HEADS, LANES), F32)] * 2
                            + [pltpu.VMEM((FA_HEADS, FA_WIDTH), F32)]
                            + [pltpu.VMEM((FA_HEADS, LANES), F32)])),
        out_shape=[jax.ShapeDtypeStruct((batch, seq, FA_WIDTH), BF16),
                   jax.ShapeDtypeStruct((n, 1, FA_WIDTH), F32)],
        compiler_params=_params(3),
        name="attention_layer",
    )(page_table, q_pm, k_pm, v_pm, c_t,
      q_s.reshape(n, 1, FA_WIDTH), k_new.reshape(n, 1, FA_WIDTH), v_new.reshape(n, 1, FA_WIDTH),
      lf_new.reshape(n, FA_HEADS, 1), *([cache_kt] * pg), *([cache_vt] * pg), *([cache_lft] * pg))
    return o_p, o_s.reshape(n, FA_WIDTH)


def kernel(x_prompt, x_sample, c_prompt, c_sample, cache_k, cache_v, cache_logf, page_table, w_ada, b_ada, ln_g, ln_b, cm_w_in, cm_b_in, cm_ln_v_g, cm_ln_v_b, cm_w_s, cm_b_s, cm_w_out, cm_b_out, kv_w, kv_b, fa_w_in, fa_b_in, fa_w_out, fa_b_out):
    batch, seq, _ = x_prompt.shape
    n_dec = x_sample.shape[0]

    n_c = batch + n_dec
    c_all = jnp.pad(jnp.concatenate([c_prompt, c_sample], axis=0), ((0, (-n_c) % 8), (0, 0)))
    mod = _ada_modulation(c_all, w_ada, b_ada)
    mod_p = mod[:, :batch].reshape(DEPTH, batch, 1, 3 * D_MODEL)
    mod_s = mod[:, batch:n_c].reshape(DEPTH, 1, n_dec, 3 * D_MODEL)

    xp = x_prompt.reshape(batch * seq, D_MODEL)
    xs = x_sample.reshape(n_dec, D_MODEL)
    chunk_rows = []
    for layer in range(N_A_LAYERS):
        args = (cm_w_in[layer], cm_b_in[layer], cm_ln_v_g[layer], cm_ln_v_b[layer], cm_w_s[layer], cm_b_s[layer],
                cm_w_out[layer], cm_b_out[layer], ln_g[layer], ln_b[layer])
        (xp,) = _chunk_mlp_layer(xp, mod_p[layer], *args, single_row_chunks=False)
        xs, vn_s = _chunk_mlp_layer(xs, mod_s[layer], *args, single_row_chunks=True)
        chunk_rows.append(vn_s)

    kt_p, vt_p, lft_p, kb_p, vb_p, ct_p = _kv_prompt(xp, kv_w, kv_b, batch, seq)
    kt_s, vt_s, lft_s, k_s, v_s, lf_s = _kv_sample(xs, kv_w, kv_b)

    n_phys = cache_k.shape[0]
    cache_kt = jnp.transpose(cache_k, (0, 2, 3, 1)).reshape(n_phys, FA_WIDTH, PAGE_SIZE)
    cache_vt = jnp.transpose(cache_v, (0, 2, 3, 1)).reshape(n_phys, FA_WIDTH, PAGE_SIZE)
    cache_lft = jnp.swapaxes(cache_logf, 1, 2)

    for j in range(DEPTH - N_A_LAYERS):
        layer = N_A_LAYERS + j
        q_pm, gz_p = _q_proj(xp, mod_p[layer], fa_w_in[j], fa_b_in[j], batch=batch, seq=seq)
        q_s, gz_s = _q_proj(xs, mod_s[layer], fa_w_in[j], fa_b_in[j])
        o_p, o_s = _attention_layer(q_pm, kb_p, vb_p, ct_p, batch, seq,
                                    q_s, k_s, v_s, lf_s, cache_kt, cache_vt, cache_lft, page_table)
        xp = _out_proj(o_p.reshape(batch * seq, FA_WIDTH), gz_p, xp, mod_p[layer], fa_w_out[j], fa_b_out[j],
                       ln_g[layer], ln_b[layer], tm=ROW_TILE)
        xs = _out_proj(o_s, gz_s, xs, mod_s[layer], fa_w_out[j], fa_b_out[j], ln_g[layer], ln_b[layer], tm=n_dec)

    y_prompt = xp.reshape(batch, seq, D_MODEL)
    y_sample = xs.reshape(n_dec, 1, D_MODEL)
    k_prompt = jnp.transpose(kt_p.reshape(batch, FA_HEADS, FA_HEAD_DIM, seq), (0, 3, 1, 2))
    v_prompt = jnp.transpose(vt_p.reshape(batch, FA_HEADS, FA_HEAD_DIM, seq), (0, 3, 1, 2))
    logf_prompt = jnp.transpose(lft_p, (0, 2, 1))
    k_sample = jnp.transpose(kt_s.reshape(1, FA_HEADS, FA_HEAD_DIM, n_dec), (3, 0, 1, 2))
    v_sample = jnp.transpose(vt_s.reshape(1, FA_HEADS, FA_HEAD_DIM, n_dec), (3, 0, 1, 2))
    logf_sample = jnp.transpose(lft_s, (2, 0, 1))
    chunk_v_sample = jnp.stack(chunk_rows).reshape(N_A_LAYERS, n_dec, 1, CM_WIDTH)
    return (y_prompt, y_sample, k_prompt, v_prompt, logf_prompt, k_sample, v_sample, logf_sample, chunk_v_sample)
```

```python
import functools

import jax
import jax.numpy as jnp
from jax import lax
from jax.experimental import pallas as pl
from jax.experimental.pallas import tpu as pltpu

F32 = jnp.float32
BF16 = jnp.bfloat16

D_MODEL = 1024
DEPTH = 4
N_A_LAYERS = 2
CHUNK = 128
CM_WIDTH = 2048
CM_GROUPS = 8
CM_GROUP_DIM = CM_WIDTH // CM_GROUPS
FA_HEADS = 16
FA_HEAD_DIM = 64
FA_WIDTH = FA_HEADS * FA_HEAD_DIM
HEAD_PAIRS = FA_HEADS // 2
PAGE_SIZE = 128
DEEPNORM_ALPHA = (2.0 * DEPTH) ** 0.25
LN_EPS = 1e-5
LANES = 128
NEG_BIG = -1e30
VMEM_LIMIT = 56 * 1024 * 1024

ROW_TILE = 512
ATT_TILE = 512
FLASH_ROWS = 32
Q_TILES_PER_STEP = 2
LOG2E = 1.4426950408889634


def _params(n_axes, vmem=VMEM_LIMIT):
    return pltpu.CompilerParams(dimension_semantics=("arbitrary",) * n_axes, vmem_limit_bytes=vmem)


def _dot(a, b):
    return jnp.dot(a, b, preferred_element_type=F32)


def _dot_nt(a, b):
    return lax.dot_general(a, b, (((1,), (1,)), ((), ())), preferred_element_type=F32)


def _layer_norm(x, g, b):
    mu = jnp.mean(x, axis=-1, keepdims=True)
    d = x - mu
    var = jnp.mean(d * d, axis=-1, keepdims=True)
    return d * lax.rsqrt(var + LN_EPS) * g + b


def _log_sigmoid(x):
    return -(jnp.maximum(-x, 0.0) + jnp.log1p(jnp.exp(-jnp.abs(x))))


def _split3_dot(x, w):
    hi = x.astype(BF16)
    r1 = x - hi.astype(F32)
    mid = r1.astype(BF16)
    lo = (r1 - mid.astype(F32)).astype(BF16)
    return _dot(hi, w) + _dot(mid, w) + _dot(lo, w)


def _const_spec(shape):
    nd = len(shape)
    return pl.BlockSpec(shape, lambda *_: (0,) * nd, pipeline_mode=pl.Buffered(1))


def _ada_kernel(c_ref, w_ref, b_ref, o_ref):
    a = jax.nn.silu(c_ref[...]).astype(BF16)
    o_ref[0] = _dot(a, w_ref[0].astype(BF16)) + b_ref[0]


def _ada_modulation(c_all, w_ada, b_ada):
    n = c_all.shape[0]
    nt = 3 * D_MODEL // D_MODEL
    return pl.pallas_call(
        _ada_kernel,
        grid=(DEPTH, nt),
        in_specs=[pl.BlockSpec((n, D_MODEL), lambda l, j: (0, 0)),
                  pl.BlockSpec((1, D_MODEL, D_MODEL), lambda l, j: (l, 0, j)),
                  pl.BlockSpec((1, 1, D_MODEL), lambda l, j: (l, 0, j))],
        out_specs=pl.BlockSpec((1, n, D_MODEL), lambda l, j: (l, 0, j)),
        out_shape=jax.ShapeDtypeStruct((DEPTH, n, 3 * D_MODEL), F32),
        compiler_params=_params(2),
        name="ada_modulation",
    )(c_all, w_ada, b_ada.reshape(DEPTH, 1, 3 * D_MODEL))


def _chunk_mlp_kernel(x_ref, mod_ref, win_ref, bin_ref, lvg_ref, lvb_ref, ws_ref, bs_ref,
                      wout_ref, bout_ref, lng_ref, lnb_ref, *rest, single_row_chunks):
    if single_row_chunks:
        o_ref, vn_ref, t_scr = rest
    else:
        o_ref, t_scr = rest
    rows = x_ref.shape[0]
    x = x_ref[...]
    m = mod_ref[0]
    shift, scale, gate = m[:, :D_MODEL], m[:, D_MODEL:2 * D_MODEL], m[:, 2 * D_MODEL:]
    h = (x * (1.0 + scale) + shift).astype(BF16)

    v = jax.nn.gelu(_dot(h, win_ref[:, CM_WIDTH:2 * CM_WIDTH]) + bin_ref[:, CM_WIDTH:2 * CM_WIDTH])
    vn = _layer_norm(v, lvg_ref[...], lvb_ref[...])
    if single_row_chunks:
        vn_ref[...] = vn
    else:
        r_i = lax.broadcasted_iota(jnp.int32, (CHUNK, CHUNK), 0)
        c_i = lax.broadcasted_iota(jnp.int32, (CHUNK, CHUNK), 1)
        causal = c_i <= r_i

    for g in range(CM_GROUPS):
        lo, hi = g * CM_GROUP_DIM, (g + 1) * CM_GROUP_DIM
        vn_g = vn[:, lo:hi]
        if single_row_chunks:
            sg = vn_g * ws_ref[:, lo:hi] + bs_ref[:, lo:hi]
        else:
            w_g = jnp.where(causal, ws_ref[g], 0.0).astype(BF16)
            b_g = bs_ref[:, g:g + 1]
            vb = vn_g.astype(BF16)
            sg = jnp.concatenate(
                [_dot(w_g, vb[c * CHUNK:(c + 1) * CHUNK]) + b_g for c in range(rows // CHUNK)], axis=0)
        u_g = jax.nn.gelu(_dot(h, win_ref[:, lo:hi]) + bin_ref[:, lo:hi])
        z_g = _dot(h, win_ref[:, 2 * CM_WIDTH + lo:2 * CM_WIDTH + hi]) + bin_ref[:, 2 * CM_WIDTH + lo:2 * CM_WIDTH + hi]
        t_scr[:, lo:hi] = (u_g * sg * jax.nn.silu(z_g)).astype(BF16)

    y = _dot(t_scr[...], wout_ref[...]) + bout_ref[...]
    r = DEEPNORM_ALPHA * x + (1.0 + gate) * y
    o_ref[...] = _layer_norm(r, lng_ref[...], lnb_ref[...])


def _chunk_mlp_layer(x, mod, w_in, b_in, lvg, lvb, w_s, b_s, w_out, b_out, ln_g, ln_b, *, single_row_chunks):
    rows = x.shape[0]
    nb, mrows = mod.shape[0], mod.shape[1]
    tm = rows if single_row_chunks else ROW_TILE
    tiles_per_batch = rows // nb // tm
    if single_row_chunks:
        ws_arg = jnp.repeat(w_s[:, 0, 0], CM_GROUP_DIM)[None]
        bs_arg = jnp.repeat(b_s[:, 0], CM_GROUP_DIM)[None]
    else:
        ws_arg, bs_arg = w_s, b_s.T
    out_shape = [jax.ShapeDtypeStruct((rows, D_MODEL), F32)]
    out_specs = [pl.BlockSpec((tm, D_MODEL), lambda i: (i, 0))]
    if single_row_chunks:
        out_shape.append(jax.ShapeDtypeStruct((rows, CM_WIDTH), F32))
        out_specs.append(pl.BlockSpec((tm, CM_WIDTH), lambda i: (i, 0)))
    res = pl.pallas_call(
        functools.partial(_chunk_mlp_kernel, single_row_chunks=single_row_chunks),
        grid=(rows // tm,),
        in_specs=[pl.BlockSpec((tm, D_MODEL), lambda i: (i, 0)),
                  pl.BlockSpec((1, mrows, 3 * D_MODEL), lambda i: (i // tiles_per_batch, 0, 0)),
                  _const_spec((D_MODEL, 3 * CM_WIDTH)), _const_spec((1, 3 * CM_WIDTH)),
                  _const_spec((1, CM_WIDTH)), _const_spec((1, CM_WIDTH)),
                  _const_spec(ws_arg.shape), _const_spec(bs_arg.shape),
                  _const_spec((CM_WIDTH, D_MODEL)), _const_spec((1, D_MODEL)),
                  _const_spec((1, D_MODEL)), _const_spec((1, D_MODEL))],
        out_specs=out_specs,
        out_shape=out_shape,
        scratch_shapes=[pltpu.VMEM((tm, CM_WIDTH), BF16)],
        compiler_params=_params(1),
        name="chunk_mlp_sample" if single_row_chunks else "chunk_mlp_prompt",
    )(x, mod, w_in.astype(BF16), b_in[None], lvg[None], lvb[None], ws_arg, bs_arg,
      w_out.astype(BF16), b_out[None], ln_g[None], ln_b[None])
    return res


def _kv_kernel(x_ref, wkv_ref, bkv_ref, wkvt_ref, bkvt_ref, wft_ref, bft_ref, *rest, prompt):
    if prompt:
        mod_ref, wq_ref, bq_ref, kt_ref, vt_ref, lft_ref, kb_ref, vb_ref, ct_ref, q_ref, gz_ref, carry_scr = rest
    else:
        wf_ref, bf_ref, kt_ref, vt_ref, lft_ref, k_ref, v_ref, lf_ref = rest
    rows = x_ref.shape[0]
    xb = x_ref[...].astype(BF16)
    kvt = _dot_nt(wkvt_ref[...], xb) + bkvt_ref[...]
    kt_ref[0] = kvt[:FA_WIDTH]
    vt_ref[0] = kvt[FA_WIDTH:]
    lft = _log_sigmoid(_dot_nt(wft_ref[...], xb) + bft_ref[...])
    lft_ref[0] = lft
    if not prompt:
        kv = _dot(xb, wkv_ref[...]) + bkv_ref[...]
        k_ref[...] = kv[:, :FA_WIDTH]
        v_ref[...] = kv[:, FA_WIDTH:]
        lf_ref[...] = _log_sigmoid(_dot(xb, wf_ref[...]) + bf_ref[...])[:, :FA_HEADS]
    else:
        ktb = kvt[:FA_WIDTH].astype(BF16)
        vb = (_dot(xb, wkv_ref[...]) + bkv_ref[...]).astype(BF16)
        for j in range(HEAD_PAIRS):
            kb_ref[0, j, 0] = ktb[j * LANES:(j + 1) * LANES, :]
            vb_ref[0, j] = vb[:, j * LANES:(j + 1) * LANES]
        r_i = lax.broadcasted_iota(jnp.int32, (rows, rows), 0)
        c_i = lax.broadcasted_iota(jnp.int32, (rows, rows), 1)
        upper = jnp.where(r_i <= c_i, 1.0, 0.0).astype(BF16)

        @pl.when(pl.program_id(1) == 0)
        def _():
            carry_scr[...] = jnp.zeros_like(carry_scr)

        ct = _split3_dot(lft, upper) + carry_scr[:, 0:1]
        carry_scr[...] = jnp.broadcast_to(ct[:, rows - 1:rows], carry_scr.shape)
        ct2 = ct * LOG2E
        for j in range(HEAD_PAIRS):
            ct_ref[0, j, 0] = ct2[2 * j:2 * j + 2, :]
        _q_proj_store(x_ref[...], mod_ref[0], wq_ref, bq_ref, q_ref, gz_ref, True)


def _kv_weights(kv_w, kv_b):
    wkv = kv_w[:, :2 * FA_WIDTH].astype(BF16)
    bkv = kv_b[None, :2 * FA_WIDTH]
    wkvt = kv_w[:, :2 * FA_WIDTH].T.astype(BF16)
    bkvt = kv_b[:2 * FA_WIDTH, None]
    wft = kv_w[:, 2 * FA_WIDTH:].T.astype(BF16)
    bft = kv_b[2 * FA_WIDTH:, None]
    return wkv, bkv, wkvt, bkvt, wft, bft


def _kv_prompt(x, kv_w, kv_b, batch, seq, mod, w_in, b_in):
    tm = ATT_TILE
    nt = seq // tm
    wkv, bkv, wkvt, bkvt, wft, bft = _kv_weights(kv_w, kv_b)
    weights = (wkv[:, FA_WIDTH:], bkv[:, FA_WIDTH:], wkvt, bkvt, wft, bft)
    q_weights = (w_in.astype(BF16), b_in[None])
    row = lambda b, i: (b * nt + i, 0)
    return pl.pallas_call(
        functools.partial(_kv_kernel, prompt=True),
        grid=(batch, nt),
        in_specs=([pl.BlockSpec((tm, D_MODEL), row)] + [_const_spec(w.shape) for w in weights]
                  + [pl.BlockSpec((1, 1, 3 * D_MODEL), lambda b, i: (b, 0, 0))]
                  + [_const_spec(w.shape) for w in q_weights]),
        out_specs=[pl.BlockSpec((1, FA_WIDTH, tm), lambda b, i: (b, 0, i)),
                   pl.BlockSpec((1, FA_WIDTH, tm), lambda b, i: (b, 0, i)),
                   pl.BlockSpec((1, FA_HEADS, tm), lambda b, i: (b, 0, i)),
                   pl.BlockSpec((1, HEAD_PAIRS, 1, LANES, tm), lambda b, i: (b, 0, i, 0, 0)),
                   pl.BlockSpec((1, HEAD_PAIRS, tm, LANES), lambda b, i: (b, 0, i, 0)),
                   pl.BlockSpec((1, HEAD_PAIRS, 1, 2, tm), lambda b, i: (b, 0, i, 0, 0)),
                   pl.BlockSpec((1, HEAD_PAIRS, tm, LANES), lambda b, i: (b, 0, i, 0)),
                   pl.BlockSpec((tm, FA_WIDTH), row)],
        out_shape=[jax.ShapeDtypeStruct((batch, FA_WIDTH, seq), F32),
                   jax.ShapeDtypeStruct((batch, FA_WIDTH, seq), F32),
                   jax.ShapeDtypeStruct((batch, FA_HEADS, seq), F32),
                   jax.ShapeDtypeStruct((batch, HEAD_PAIRS, nt, LANES, tm), BF16),
                   jax.ShapeDtypeStruct((batch, HEAD_PAIRS, seq, LANES), BF16),
                   jax.ShapeDtypeStruct((batch, HEAD_PAIRS, nt, 2, tm), F32),
                   jax.ShapeDtypeStruct((batch, HEAD_PAIRS, seq, LANES), BF16),
                   jax.ShapeDtypeStruct((batch * seq, FA_WIDTH), BF16)],
        scratch_shapes=[pltpu.VMEM((FA_HEADS, LANES), F32)],
        compiler_params=_params(2),
        name="kv_prompt",
    )(x, *weights, mod, *q_weights)


def _kv_sample(x, kv_w, kv_b):
    rows = x.shape[0]
    weights = _kv_weights(kv_w, kv_b)
    wf = jnp.pad(kv_w[:, 2 * FA_WIDTH:], ((0, 0), (0, LANES - FA_HEADS))).astype(BF16)
    bf = jnp.pad(kv_b[2 * FA_WIDTH:], (0, LANES - FA_HEADS))[None]
    weights = weights + (wf, bf)
    full = lambda i: (0, 0)
    full3 = lambda i: (0, 0, 0)
    return pl.pallas_call(
        functools.partial(_kv_kernel, prompt=False),
        grid=(1,),
        in_specs=[pl.BlockSpec((rows, D_MODEL), full)] + [_const_spec(w.shape) for w in weights],
        out_specs=[pl.BlockSpec((1, FA_WIDTH, rows), full3), pl.BlockSpec((1, FA_WIDTH, rows), full3),
                   pl.BlockSpec((1, FA_HEADS, rows), full3),
                   pl.BlockSpec((rows, FA_WIDTH), full), pl.BlockSpec((rows, FA_WIDTH), full),
                   pl.BlockSpec((rows, FA_HEADS), full)],
        out_shape=[jax.ShapeDtypeStruct((1, FA_WIDTH, rows), F32), jax.ShapeDtypeStruct((1, FA_WIDTH, rows), F32),
                   jax.ShapeDtypeStruct((1, FA_HEADS, rows), F32),
                   jax.ShapeDtypeStruct((rows, FA_WIDTH), F32), jax.ShapeDtypeStruct((rows, FA_WIDTH), F32),
                   jax.ShapeDtypeStruct((rows, FA_HEADS), F32)],
        compiler_params=_params(1),
        name="kv_sample",
    )(x, *weights)


def _q_proj_kernel(x_ref, mod_ref, w_ref, b_ref, q_ref, gz_ref, *, pair_major):
    _q_proj_store(x_ref[...], mod_ref[0], w_ref, b_ref, q_ref, gz_ref, pair_major)


def _q_proj_store(x, m, w_ref, b_ref, q_ref, gz_ref, pair_major):
    h = (x * (1.0 + m[:, D_MODEL:2 * D_MODEL]) + m[:, :D_MODEL]).astype(BF16)
    p = _dot(h, w_ref[...]) + b_ref[...]
    if pair_major:
        q = (p[:, :FA_WIDTH] * (LOG2E * FA_HEAD_DIM ** -0.5)).astype(BF16)
        for j in range(HEAD_PAIRS):
            q_ref[0, j] = q[:, j * LANES:(j + 1) * LANES]
    else:
        q_ref[...] = (p[:, :FA_WIDTH] * (FA_HEAD_DIM ** -0.5)).astype(BF16)
    gz_ref[...] = jax.nn.silu(p[:, FA_WIDTH:]).astype(gz_ref.dtype)


def _q_proj(x, mod, w_in, b_in, *, batch=None, seq=None):
    rows = x.shape[0]
    nb, mrows = mod.shape[0], mod.shape[1]
    pair_major = batch is not None
    tm = ROW_TILE if pair_major else rows
    tpb = rows // nb // tm
    if pair_major:
        q_spec = pl.BlockSpec((1, HEAD_PAIRS, tm, LANES), lambda i: (i // tpb, 0, i % tpb, 0))
        q_shape = jax.ShapeDtypeStruct((batch, HEAD_PAIRS, seq, LANES), BF16)
    else:
        q_spec = pl.BlockSpec((tm, FA_WIDTH), lambda i: (i, 0))
        q_shape = jax.ShapeDtypeStruct((rows, FA_WIDTH), BF16)
    return pl.pallas_call(
        functools.partial(_q_proj_kernel, pair_major=pair_major),
        grid=(rows // tm,),
        in_specs=[pl.BlockSpec((tm, D_MODEL), lambda i: (i, 0)),
                  pl.BlockSpec((1, mrows, 3 * D_MODEL), lambda i: (i // tpb, 0, 0)),
                  _const_spec((D_MODEL, 2 * FA_WIDTH)), _const_spec((1, 2 * FA_WIDTH))],
        out_specs=[q_spec, pl.BlockSpec((tm, FA_WIDTH), lambda i: (i, 0))],
        out_shape=[q_shape, jax.ShapeDtypeStruct((rows, FA_WIDTH), BF16)],
        compiler_params=_params(1),
        name="q_proj_prompt" if pair_major else "q_proj_sample",
    )(x, mod, w_in.astype(BF16), b_in[None])


def _out_proj_kernel(o_ref, gz_ref, x_ref, mod_ref, w_ref, b_ref, g_ref, bb_ref, *rest, next_q):
    if next_q:
        modn_ref, wq_ref, bq_ref, y_ref, q_ref, gzn_ref = rest
    else:
        (y_ref,) = rest
    t = (o_ref[...].astype(F32) * gz_ref[...].astype(F32)).astype(BF16)
    y = _dot(t, w_ref[...]) + b_ref[...]
    gate = mod_ref[0][:, 2 * D_MODEL:]
    r = DEEPNORM_ALPHA * x_ref[...] + (1.0 + gate) * y
    x_new = _layer_norm(r, g_ref[...], bb_ref[...])
    y_ref[...] = x_new
    if next_q:
        _q_proj_store(x_new, modn_ref[0], wq_ref, bq_ref, q_ref, gzn_ref, True)


def _out_proj(o, gz, x, mod, w_out, b_out, ln_g, ln_b, *, tm, next_q=None):
    rows = x.shape[0]
    nb, mrows = mod.shape[0], mod.shape[1]
    tpb = rows // nb // tm
    row = lambda i: (i, 0)
    mod_spec = pl.BlockSpec((1, mrows, 3 * D_MODEL), lambda i: (i // tpb, 0, 0))
    in_specs = [pl.BlockSpec((tm, FA_WIDTH), row), pl.BlockSpec((tm, FA_WIDTH), row),
                pl.BlockSpec((tm, D_MODEL), row), mod_spec,
                _const_spec((FA_WIDTH, D_MODEL)), _const_spec((1, D_MODEL)),
                _const_spec((1, D_MODEL)), _const_spec((1, D_MODEL))]
    args = [o, gz, x, mod, w_out.astype(BF16), b_out[None], ln_g[None], ln_b[None]]
    out_specs = [pl.BlockSpec((tm, D_MODEL), row)]
    out_shape = [jax.ShapeDtypeStruct((rows, D_MODEL), F32)]
    if next_q is not None:
        mod_n, w_in, b_in, batch, seq = next_q
        in_specs += [mod_spec, _const_spec((D_MODEL, 2 * FA_WIDTH)), _const_spec((1, 2 * FA_WIDTH))]
        args += [mod_n, w_in.astype(BF16), b_in[None]]
        out_specs += [pl.BlockSpec((1, HEAD_PAIRS, tm, LANES), lambda i: (i // tpb, 0, i % tpb, 0)),
                      pl.BlockSpec((tm, FA_WIDTH), row)]
        out_shape += [jax.ShapeDtypeStruct((batch, HEAD_PAIRS, seq, LANES), BF16),
                      jax.ShapeDtypeStruct((rows, FA_WIDTH), BF16)]
    res = pl.pallas_call(
        functools.partial(_out_proj_kernel, next_q=next_q is not None),
        grid=(rows // tm,),
        in_specs=in_specs,
        out_specs=out_specs,
        out_shape=out_shape,
        compiler_params=_params(1),
        name="out_proj",
    )(*args)
    return res if next_q is not None else res[0]


def _flash_q_tile(i, q2, k_ref, v_ref, c_ref, scr, tile):
    s_scr = (scr[0:2], scr[2:4])
    m_scr = (scr[4:6], scr[6:8])
    p_scr, alpha_scr, acc_scr = scr[8:10], scr[10:12], scr[12:14]
    lane = lax.broadcasted_iota(jnp.int32, (1, LANES), 1)
    first = lane < FA_HEAD_DIM
    sub8 = lax.broadcasted_iota(jnp.int32, (8, tile), 0)

    def logits(t, q_heads, buf):
        c2 = c_ref[0, 0, t]
        hi = c2.astype(BF16).astype(F32)
        mid = (c2 - hi).astype(BF16).astype(F32)
        lo = c2 - hi - mid
        pieces = (hi[0:1], mid[0:1], lo[0:1], hi[1:2], mid[1:2], lo[1:2])
        c_rows = jnp.zeros((8, tile), F32)
        for r, piece in enumerate(pieces):
            c_rows = jnp.where(sub8 == r, piece, c_rows)
        c_block = jnp.concatenate([c_rows, jnp.zeros((LANES - 8, tile), F32)], axis=0).astype(BF16)
        k_aug = jnp.concatenate([k_ref[0, 0, t], c_block], axis=0)
        for a in range(2):
            s_scr[buf][a][...] = _dot(q_heads[a], k_aug)

    def softmax_blocks(a, buf, masked):
        for r0 in range(0, tile, FLASH_ROWS):
            rows = slice(r0, r0 + FLASH_ROWS)
            ncols = min(tile, -(-(r0 + FLASH_ROWS) // LANES) * LANES) if masked else tile
            s = s_scr[buf][a][rows, :ncols]
            if masked:
                r_i = r0 + lax.broadcasted_iota(jnp.int32, (FLASH_ROWS, ncols), 0)
                c_i = lax.broadcasted_iota(jnp.int32, (FLASH_ROWS, ncols), 1)
                s = jnp.where(c_i <= r_i, s, NEG_BIG)
            m_old = m_scr[buf][a][rows, :]
            m_new = jnp.maximum(m_old, jnp.max(s, axis=-1, keepdims=True))
            m_scr[1 - buf][a][rows, :] = m_new
            alpha_scr[a][rows, :] = jnp.exp2(m_old - m_new)
            m_wide = jnp.concatenate([m_new] * (ncols // LANES), axis=1)
            p_scr[a][rows, :ncols] = jnp.exp2((s - m_wide).astype(BF16))
            if ncols < tile:
                p_scr[a][rows, ncols:] = jnp.zeros((FLASH_ROWS, tile - ncols), BF16)

    def consume(t, buf, masked):
        v2 = v_ref[0, 0, pl.ds(pl.multiple_of(t * tile, tile), tile), :]
        one = jnp.ones_like(v2)
        v_heads = (jnp.where(first, v2, one), jnp.where(first, one, v2))
        for a in range(2):
            softmax_blocks(a, buf, masked)
            acc_scr[a][...] = alpha_scr[a][...] * acc_scr[a][...] + _dot(p_scr[a][...], v_heads[a])

    col_q = lax.broadcasted_iota(jnp.int32, (tile, LANES), 1)
    zero = jnp.zeros_like(q2)
    q_heads = (jnp.concatenate([jnp.where(first, q2, zero),
                                jnp.where(col_q < 3, -1.0, 0.0).astype(BF16)], axis=1),
               jnp.concatenate([jnp.where(first, zero, q2),
                                jnp.where((col_q >= 3) & (col_q < 6), -1.0, 0.0).astype(BF16)], axis=1))
    for a in range(2):
        m_scr[0][a][...] = jnp.full((tile, LANES), NEG_BIG, F32)
        acc_scr[a][...] = jnp.zeros((tile, LANES), F32)

    logits(0, q_heads, 0)

    def tile_pair(u, _):
        logits(2 * u + 1, q_heads, 1)
        consume(2 * u, 0, False)
        logits(2 * u + 2, q_heads, 0)
        consume(2 * u + 1, 1, False)
        return 0

    lax.fori_loop(0, i // 2, tile_pair, 0)

    @pl.when(i % 2 == 1)
    def _():
        logits(i, q_heads, 1)
        consume(i - 1, 0, False)
        consume(i, 1, True)

    @pl.when(i % 2 == 0)
    def _():
        consume(i, 0, True)

    acc_a, acc_b = acc_scr[0][...], acc_scr[1][...]
    o_a = acc_a / pltpu.roll(acc_a, FA_HEAD_DIM, axis=1)
    o_b = acc_b / pltpu.roll(acc_b, FA_HEAD_DIM, axis=1)
    return jnp.where(first, o_a, o_b)


def _decode_group(g, n_groups, q_ref, kn_ref, vn_ref, lfn_ref, k_refs, v_refs, lf_refs, o_ref,
                  m_scr, l_scr, acc_scr, off_scr):
    pg = len(k_refs)

    @pl.when(g == 0)
    def _():
        m_scr[...] = jnp.full(m_scr.shape, NEG_BIG, F32)
        l_scr[...] = jnp.zeros(l_scr.shape, F32)
        acc_scr[...] = jnp.zeros(acc_scr.shape, F32)
        off_scr[...] = jnp.zeros(off_scr.shape, F32)

    head = lax.broadcasted_iota(jnp.int32, (FA_HEADS, FA_WIDTH), 0)
    col = lax.broadcasted_iota(jnp.int32, (FA_HEADS, FA_WIDTH), 1)
    own = (col >= head * FA_HEAD_DIM) & (col < (head + 1) * FA_HEAD_DIM)
    q_rows = jnp.where(own, jnp.broadcast_to(q_ref[0].astype(F32), (FA_HEADS, FA_WIDTH)), 0.0)
    q_bf = q_rows.astype(BF16)

    r_i = lax.broadcasted_iota(jnp.int32, (PAGE_SIZE, PAGE_SIZE), 0)
    c_i = lax.broadcasted_iota(jnp.int32, (PAGE_SIZE, PAGE_SIZE), 1)
    upper = jnp.where(r_i <= c_i, 1.0, 0.0).astype(BF16)
    lf_all = jnp.concatenate([lf_refs[p][0] for p in range(pg)], axis=0)
    cum_all = _split3_dot(lf_all, upper)
    off = off_scr[:, 0:1]
    s_parts = []
    for p in range(pg):
        cum_p = cum_all[p * FA_HEADS:(p + 1) * FA_HEADS]
        s_parts.append(_dot(q_bf, k_refs[p][0].astype(BF16)) - (off + cum_p))
        off = off + cum_p[:, PAGE_SIZE - 1:PAGE_SIZE]
    off_scr[...] = jnp.broadcast_to(off, off_scr.shape)
    s = jnp.concatenate(s_parts, axis=1)
    m_old = m_scr[:, 0:1]
    m_new = jnp.maximum(m_old, jnp.max(s, axis=1, keepdims=True))
    alpha = jnp.exp(m_old - m_new)
    prob = jnp.exp(s - m_new)
    l_new = alpha * l_scr[:, 0:1] + jnp.sum(prob, axis=1, keepdims=True)
    pb = prob.astype(BF16)
    acc = alpha * acc_scr[...]
    for p in range(pg):
        acc = acc + _dot_nt(pb[:, p * PAGE_SIZE:(p + 1) * PAGE_SIZE], v_refs[p][0].astype(BF16))
    m_scr[...] = jnp.broadcast_to(m_new, m_scr.shape)
    l_scr[...] = jnp.broadcast_to(l_new, l_scr.shape)
    acc_scr[...] = acc

    @pl.when(g == n_groups - 1)
    def _():
        c_query = off + lfn_ref[0]
        s_new = jnp.sum(q_rows * kn_ref[0], axis=1, keepdims=True)
        m_past = m_new + c_query
        m_fin = jnp.maximum(m_past, s_new)
        w_past = jnp.exp(m_past - m_fin)
        w_new = jnp.exp(s_new - m_fin)
        o_full = (acc * w_past + w_new * vn_ref[0]) / (l_new * w_past + w_new)
        o_ref[0] = jnp.sum(jnp.where(own, o_full, 0.0), axis=0, keepdims=True)


def _attention_kernel(pt_ref, q_ref, k_ref, v_ref, c_ref, qs_ref, kn_ref, vn_ref, lfn_ref, *refs,
                      tile, pg, n_groups):
    del pt_ref
    k_refs, v_refs, lf_refs = refs[:pg], refs[pg:2 * pg], refs[2 * pg:3 * pg]
    o_ref, os_ref = refs[3 * pg:3 * pg + 2]
    scr = refs[3 * pg + 2:]
    i2 = pl.program_id(2)

    def q_tile(h, _):
        rows = pl.ds(pl.multiple_of(h * tile, tile), tile)
        o = _flash_q_tile(Q_TILES_PER_STEP * i2 + h, q_ref[0, 0, rows, :], k_ref, v_ref, c_ref, scr[:14], tile)
        o_ref[0, rows, :] = o.astype(o_ref.dtype)
        return 0

    lax.fori_loop(0, Q_TILES_PER_STEP, q_tile, 0)
    step = (pl.program_id(0) * pl.num_programs(1) + pl.program_id(1)) * pl.num_programs(2) + i2
    _decode_group(step % n_groups, n_groups, qs_ref, kn_ref, vn_ref, lfn_ref, k_refs, v_refs, lf_refs, os_ref,
                  *scr[14:])


def _attention_layer(q_pm, k_pm, v_pm, c_t, batch, seq,
                     q_s, k_new, v_new, lf_new, cache_kt, cache_vt, cache_lft, page_table):
    tile = ATT_TILE
    nt = seq // tile
    qt = Q_TILES_PER_STEP
    nq = nt // qt
    n, n_pages = page_table.shape
    steps = batch * HEAD_PAIRS * nq
    n_groups = steps // n
    pg = n_pages // n_groups
    assert nq * qt == nt and n_groups * n == steps and pg * n_groups == n_pages

    def sample_of(b, j, i):
        return ((b * HEAD_PAIRS + j) * nq + i) // n_groups

    def page_map(p):
        def index(b, j, i, pt):
            step = (b * HEAD_PAIRS + j) * nq + i
            return (pt[step // n_groups, (step % n_groups) * pg + p], 0, 0)
        return index

    row = lambda b, j, i, pt: (sample_of(b, j, i), 0, 0)
    in_specs = [pl.BlockSpec((1, 1, qt * tile, LANES), lambda b, j, i, pt: (b, j, i, 0)),
                pl.BlockSpec((1, 1, nt, LANES, tile), lambda b, j, i, pt: (b, j, 0, 0, 0)),
                pl.BlockSpec((1, 1, seq, LANES), lambda b, j, i, pt: (b, j, 0, 0)),
                pl.BlockSpec((1, 1, nt, 2, tile), lambda b, j, i, pt: (b, j, 0, 0, 0)),
                pl.BlockSpec((1, 1, FA_WIDTH), row), pl.BlockSpec((1, 1, FA_WIDTH), row),
                pl.BlockSpec((1, 1, FA_WIDTH), row), pl.BlockSpec((1, FA_HEADS, 1), row)]
    in_specs += [pl.BlockSpec((1, FA_WIDTH, PAGE_SIZE), page_map(p)) for p in range(pg)]
    in_specs += [pl.BlockSpec((1, FA_WIDTH, PAGE_SIZE), page_map(p)) for p in range(pg)]
    in_specs += [pl.BlockSpec((1, FA_HEADS, PAGE_SIZE), page_map(p)) for p in range(pg)]
    o_p, o_s = pl.pallas_call(
        functools.partial(_attention_kernel, tile=tile, pg=pg, n_groups=n_groups),
        grid_spec=pltpu.PrefetchScalarGridSpec(
            num_scalar_prefetch=1, grid=(batch, HEAD_PAIRS, nq), in_specs=in_specs,
            out_specs=[pl.BlockSpec((1, qt * tile, LANES), lambda b, j, i, pt: (b, i, j)),
                       pl.BlockSpec((1, 1, FA_WIDTH), row)],
            scratch_shapes=([pltpu.VMEM((tile, tile), F32)] * 4
                            + [pltpu.VMEM((tile, LANES), F32)] * 4
                            + [pltpu.VMEM((tile, tile), BF16)] * 2
                            + [pltpu.VMEM((tile, LANES), F32)] * 2
                            + [pltpu.VMEM((tile, LANES), F32)] * 2
                            + [pltpu.VMEM((FA_HEADS, LANES), F32)] * 2
                            + [pltpu.VMEM((FA_HEADS, FA_WIDTH), F32)]
                            + [pltpu.VMEM((FA_HEADS, LANES), F32)])),
        out_shape=[jax.ShapeDtypeStruct((batch, seq, FA_WIDTH), BF16),
                   jax.ShapeDtypeStruct((n, 1, FA_WIDTH), F32)],
        compiler_params=_params(3),
        name="attention_layer",
    )(page_table, q_pm, k_pm, v_pm, c_t,
      q_s.reshape(n, 1, FA_WIDTH), k_new.reshape(n, 1, FA_WIDTH), v_new.reshape(n, 1, FA_WIDTH),
      lf_new.reshape(n, FA_HEADS, 1), *([cache_kt] * pg), *([cache_vt] * pg), *([cache_lft] * pg))
    return o_p, o_s.reshape(n, FA_WIDTH)


def kernel(x_prompt, x_sample, c_prompt, c_sample, cache_k, cache_v, cache_logf, page_table, w_ada, b_ada, ln_g, ln_b, cm_w_in, cm_b_in, cm_ln_v_g, cm_ln_v_b, cm_w_s, cm_b_s, cm_w_out, cm_b_out, kv_w, kv_b, fa_w_in, fa_b_in, fa_w_out, fa_b_out):
    batch, seq, _ = x_prompt.shape
    n_dec = x_sample.shape[0]

    n_c = batch + n_dec
    c_all = jnp.pad(jnp.concatenate([c_prompt, c_sample], axis=0), ((0, (-n_c) % 8), (0, 0)))
    mod = _ada_modulation(c_all, w_ada, b_ada)
    mod_p = mod[:, :batch].reshape(DEPTH, batch, 1, 3 * D_MODEL)
    mod_s = mod[:, batch:n_c].reshape(DEPTH, 1, n_dec, 3 * D_MODEL)

    xp = x_prompt.reshape(batch * seq, D_MODEL)
    xs = x_sample.reshape(n_dec, D_MODEL)
    chunk_rows = []
    for layer in range(N_A_LAYERS):
        args = (cm_w_in[layer], cm_b_in[layer], cm_ln_v_g[layer], cm_ln_v_b[layer], cm_w_s[layer], cm_b_s[layer],
                cm_w_out[layer], cm_b_out[layer], ln_g[layer], ln_b[layer])
        (xp,) = _chunk_mlp_layer(xp, mod_p[layer], *args, single_row_chunks=False)
        xs, vn_s = _chunk_mlp_layer(xs, mod_s[layer], *args, single_row_chunks=True)
        chunk_rows.append(vn_s)

    kt_p, vt_p, lft_p, kb_p, vb_p, ct_p, q_pm, gz_p = _kv_prompt(
        xp, kv_w, kv_b, batch, seq, mod_p[N_A_LAYERS], fa_w_in[0], fa_b_in[0])
    kt_s, vt_s, lft_s, k_s, v_s, lf_s = _kv_sample(xs, kv_w, kv_b)

    n_phys = cache_k.shape[0]
    cache_kt = jnp.transpose(cache_k, (0, 2, 3, 1)).reshape(n_phys, FA_WIDTH, PAGE_SIZE)
    cache_vt = jnp.transpose(cache_v, (0, 2, 3, 1)).reshape(n_phys, FA_WIDTH, PAGE_SIZE)
    cache_lft = jnp.swapaxes(cache_logf, 1, 2)

    for j in range(DEPTH - N_A_LAYERS):
        layer = N_A_LAYERS + j
        q_s, gz_s = _q_proj(xs, mod_s[layer], fa_w_in[j], fa_b_in[j])
        o_p, o_s = _attention_layer(q_pm, kb_p, vb_p, ct_p, batch, seq,
                                    q_s, k_s, v_s, lf_s, cache_kt, cache_vt, cache_lft, page_table)
        next_q = ((mod_p[layer + 1], fa_w_in[j + 1], fa_b_in[j + 1], batch, seq) if layer + 1 < DEPTH else None)
        res = _out_proj(o_p.reshape(batch * seq, FA_WIDTH), gz_p, xp, mod_p[layer], fa_w_out[j], fa_b_out[j],
                        ln_g[layer], ln_b[layer], tm=ROW_TILE, next_q=next_q)
        xp, q_pm, gz_p = res if next_q is not None else (res, None, None)
        xs = _out_proj(o_s, gz_s, xs, mod_s[layer], fa_w_out[j], fa_b_out[j], ln_g[layer], ln_b[layer], tm=n_dec)

    y_prompt = xp.reshape(batch, seq, D_MODEL)
    y_sample = xs.reshape(n_dec, 1, D_MODEL)
    k_prompt = jnp.transpose(kt_p.reshape(batch, FA_HEADS, FA_HEAD_DIM, seq), (0, 3, 1, 2))
    v_prompt = jnp.transpose(vt_p.reshape(batch, FA_HEADS, FA_HEAD_DIM, seq), (0, 3, 1, 2))
    logf_prompt = jnp.transpose(lft_p, (0, 2, 1))
    k_sample = jnp.transpose(kt_s.reshape(1, FA_HEADS, FA_HEAD_DIM, n_dec), (3, 0, 1, 2))
    v_sample = jnp.transpose(vt_s.reshape(1, FA_HEADS, FA_HEAD_DIM, n_dec), (3, 0, 1, 2))
    logf_sample = jnp.transpose(lft_s, (2, 0, 1))
    chunk_v_sample = jnp.stack(chunk_rows).reshape(N_A_LAYERS, n_dec, 1, CM_WIDTH)
    return (y_prompt, y_sample, k_prompt, v_prompt, logf_prompt, k_sample, v_sample, logf_sample, chunk_v_sample)
```

```python
import functools

import jax
import jax.numpy as jnp
from jax import lax
from jax.experimental import pallas as pl
from jax.experimental.pallas import tpu as pltpu

F32 = jnp.float32
BF16 = jnp.bfloat16

D_MODEL = 1024
DEPTH = 4
N_A_LAYERS = 2
CHUNK = 128
CM_WIDTH = 2048
CM_GROUPS = 8
CM_GROUP_DIM = CM_WIDTH // CM_GROUPS
FA_HEADS = 16
FA_HEAD_DIM = 64
FA_WIDTH = FA_HEADS * FA_HEAD_DIM
HEAD_PAIRS = FA_HEADS // 2
PAGE_SIZE = 128
DEEPNORM_ALPHA = (2.0 * DEPTH) ** 0.25
LN_EPS = 1e-5
LANES = 128
NEG_BIG = -1e30
VMEM_LIMIT = 56 * 1024 * 1024

ROW_TILE = 512
ATT_TILE = 512
FLASH_ROWS = 32
Q_TILES_PER_STEP = 2
LOG2E = 1.4426950408889634


def _params(n_axes, vmem=VMEM_LIMIT):
    return pltpu.CompilerParams(dimension_semantics=("arbitrary",) * n_axes, vmem_limit_bytes=vmem)


def _dot(a, b):
    return jnp.dot(a, b, preferred_element_type=F32)


def _dot_nt(a, b):
    return lax.dot_general(a, b, (((1,), (1,)), ((), ())), preferred_element_type=F32)


def _layer_norm(x, g, b):
    mu = jnp.mean(x, axis=-1, keepdims=True)
    d = x - mu
    var = jnp.mean(d * d, axis=-1, keepdims=True)
    return d * lax.rsqrt(var + LN_EPS) * g + b


def _log_sigmoid(x):
    return -(jnp.maximum(-x, 0.0) + jnp.log1p(jnp.exp(-jnp.abs(x))))


def _split3_dot(x, w):
    hi = x.astype(BF16)
    r1 = x - hi.astype(F32)
    mid = r1.astype(BF16)
    lo = (r1 - mid.astype(F32)).astype(BF16)
    return _dot(hi, w) + _dot(mid, w) + _dot(lo, w)


def _const_spec(shape):
    nd = len(shape)
    return pl.BlockSpec(shape, lambda *_: (0,) * nd, pipeline_mode=pl.Buffered(1))


def _ada_kernel(c_ref, w_ref, b_ref, o_ref):
    a = jax.nn.silu(c_ref[...]).astype(BF16)
    o_ref[0] = _dot(a, w_ref[0].astype(BF16)) + b_ref[0]


def _ada_modulation(c_all, w_ada, b_ada):
    n = c_all.shape[0]
    nt = 3 * D_MODEL // D_MODEL
    return pl.pallas_call(
        _ada_kernel,
        grid=(DEPTH, nt),
        in_specs=[pl.BlockSpec((n, D_MODEL), lambda l, j: (0, 0)),
                  pl.BlockSpec((1, D_MODEL, D_MODEL), lambda l, j: (l, 0, j)),
                  pl.BlockSpec((1, 1, D_MODEL), lambda l, j: (l, 0, j))],
        out_specs=pl.BlockSpec((1, n, D_MODEL), lambda l, j: (l, 0, j)),
        out_shape=jax.ShapeDtypeStruct((DEPTH, n, 3 * D_MODEL), F32),
        compiler_params=_params(2),
        name="ada_modulation",
    )(c_all, w_ada, b_ada.reshape(DEPTH, 1, 3 * D_MODEL))


def _chunk_mlp_kernel(x_ref, mod_ref, win_ref, bin_ref, lvg_ref, lvb_ref, ws_ref, bs_ref,
                      wout_ref, bout_ref, lng_ref, lnb_ref, *rest, single_row_chunks):
    if single_row_chunks:
        o_ref, vn_ref, t_scr = rest
    else:
        o_ref, t_scr = rest
    rows = x_ref.shape[0]
    x = x_ref[...]
    m = mod_ref[0]
    shift, scale, gate = m[:, :D_MODEL], m[:, D_MODEL:2 * D_MODEL], m[:, 2 * D_MODEL:]
    h = (x * (1.0 + scale) + shift).astype(BF16)

    v = jax.nn.gelu(_dot(h, win_ref[:, CM_WIDTH:2 * CM_WIDTH]) + bin_ref[:, CM_WIDTH:2 * CM_WIDTH])
    vn = _layer_norm(v, lvg_ref[...], lvb_ref[...])
    if single_row_chunks:
        vn_ref[...] = vn
    else:
        r_i = lax.broadcasted_iota(jnp.int32, (CHUNK, CHUNK), 0)
        c_i = lax.broadcasted_iota(jnp.int32, (CHUNK, CHUNK), 1)
        causal = c_i <= r_i

    for g in range(CM_GROUPS):
        lo, hi = g * CM_GROUP_DIM, (g + 1) * CM_GROUP_DIM
        vn_g = vn[:, lo:hi]
        if single_row_chunks:
            sg = vn_g * ws_ref[:, lo:hi] + bs_ref[:, lo:hi]
        else:
            w_g = jnp.where(causal, ws_ref[g], 0.0).astype(BF16)
            b_g = bs_ref[:, g:g + 1]
            vb = vn_g.astype(BF16)
            sg = jnp.concatenate(
                [_dot(w_g, vb[c * CHUNK:(c + 1) * CHUNK]) + b_g for c in range(rows // CHUNK)], axis=0)
        u_g = jax.nn.gelu(_dot(h, win_ref[:, lo:hi]) + bin_ref[:, lo:hi])
        z_g = _dot(h, win_ref[:, 2 * CM_WIDTH + lo:2 * CM_WIDTH + hi]) + bin_ref[:, 2 * CM_WIDTH + lo:2 * CM_WIDTH + hi]
        t_scr[:, lo:hi] = (u_g * sg * jax.nn.silu(z_g)).astype(BF16)

    y = _dot(t_scr[...], wout_ref[...]) + bout_ref[...]
    r = DEEPNORM_ALPHA * x + (1.0 + gate) * y
    o_ref[...] = _layer_norm(r, lng_ref[...], lnb_ref[...])


def _chunk_mlp_layer(x, mod, w_in, b_in, lvg, lvb, w_s, b_s, w_out, b_out, ln_g, ln_b, *, single_row_chunks):
    rows = x.shape[0]
    nb, mrows = mod.shape[0], mod.shape[1]
    tm = rows if single_row_chunks else ROW_TILE
    tiles_per_batch = rows // nb // tm
    if single_row_chunks:
        ws_arg = jnp.repeat(w_s[:, 0, 0], CM_GROUP_DIM)[None]
        bs_arg = jnp.repeat(b_s[:, 0], CM_GROUP_DIM)[None]
    else:
        ws_arg, bs_arg = w_s, b_s.T
    out_shape = [jax.ShapeDtypeStruct((rows, D_MODEL), F32)]
    out_specs = [pl.BlockSpec((tm, D_MODEL), lambda i: (i, 0))]
    if single_row_chunks:
        out_shape.append(jax.ShapeDtypeStruct((rows, CM_WIDTH), F32))
        out_specs.append(pl.BlockSpec((tm, CM_WIDTH), lambda i: (i, 0)))
    res = pl.pallas_call(
        functools.partial(_chunk_mlp_kernel, single_row_chunks=single_row_chunks),
        grid=(rows // tm,),
        in_specs=[pl.BlockSpec((tm, D_MODEL), lambda i: (i, 0)),
                  pl.BlockSpec((1, mrows, 3 * D_MODEL), lambda i: (i // tiles_per_batch, 0, 0)),
                  _const_spec((D_MODEL, 3 * CM_WIDTH)), _const_spec((1, 3 * CM_WIDTH)),
                  _const_spec((1, CM_WIDTH)), _const_spec((1, CM_WIDTH)),
                  _const_spec(ws_arg.shape), _const_spec(bs_arg.shape),
                  _const_spec((CM_WIDTH, D_MODEL)), _const_spec((1, D_MODEL)),
                  _const_spec((1, D_MODEL)), _const_spec((1, D_MODEL))],
        out_specs=out_specs,
        out_shape=out_shape,
        scratch_shapes=[pltpu.VMEM((tm, CM_WIDTH), BF16)],
        compiler_params=_params(1),
        name="chunk_mlp_sample" if single_row_chunks else "chunk_mlp_prompt",
    )(x, mod, w_in.astype(BF16), b_in[None], lvg[None], lvb[None], ws_arg, bs_arg,
      w_out.astype(BF16), b_out[None], ln_g[None], ln_b[None])
    return res


def _kv_kernel(x_ref, wkv_ref, bkv_ref, wkvt_ref, bkvt_ref, wft_ref, bft_ref, *rest, prompt):
    if prompt:
        mod_ref, wq_ref, bq_ref, kt_ref, vt_ref, lft_ref, kb_ref, vb_ref, ct_ref, q_ref, gz_ref, carry_scr = rest
    else:
        wf_ref, bf_ref, kt_ref, vt_ref, lft_ref, k_ref, v_ref, lf_ref = rest
    rows = x_ref.shape[0]
    xb = x_ref[...].astype(BF16)
    kvt = _dot_nt(wkvt_ref[...], xb) + bkvt_ref[...]
    kt_ref[0] = kvt[:FA_WIDTH]
    vt_ref[0] = kvt[FA_WIDTH:]
    lft = _log_sigmoid(_dot_nt(wft_ref[...], xb) + bft_ref[...])
    lft_ref[0] = lft
    if not prompt:
        kv = _dot(xb, wkv_ref[...]) + bkv_ref[...]
        k_ref[...] = kv[:, :FA_WIDTH]
        v_ref[...] = kv[:, FA_WIDTH:]
        lf_ref[...] = _log_sigmoid(_dot(xb, wf_ref[...]) + bf_ref[...])[:, :FA_HEADS]
    else:
        ktb = kvt[:FA_WIDTH].astype(BF16)
        vb = (_dot(xb, wkv_ref[...]) + bkv_ref[...]).astype(BF16)
        for j in range(HEAD_PAIRS):
            kb_ref[0, j, 0] = ktb[j * LANES:(j + 1) * LANES, :]
            vb_ref[0, j] = vb[:, j * LANES:(j + 1) * LANES]
        r_i = lax.broadcasted_iota(jnp.int32, (rows, rows), 0)
        c_i = lax.broadcasted_iota(jnp.int32, (rows, rows), 1)
        upper = jnp.where(r_i <= c_i, 1.0, 0.0).astype(BF16)

        @pl.when(pl.program_id(1) == 0)
        def _():
            carry_scr[...] = jnp.zeros_like(carry_scr)

        ct = _split3_dot(lft, upper) + carry_scr[:, 0:1]
        carry_scr[...] = jnp.broadcast_to(ct[:, rows - 1:rows], carry_scr.shape)
        ct2 = ct * LOG2E
        for j in range(HEAD_PAIRS):
            ct_ref[0, j, 0] = ct2[2 * j:2 * j + 2, :]
        _q_proj_store(x_ref[...], mod_ref[0], wq_ref, bq_ref, q_ref, gz_ref, True)


def _kv_weights(kv_w, kv_b):
    wkv = kv_w[:, :2 * FA_WIDTH].astype(BF16)
    bkv = kv_b[None, :2 * FA_WIDTH]
    wkvt = kv_w[:, :2 * FA_WIDTH].T.astype(BF16)
    bkvt = kv_b[:2 * FA_WIDTH, None]
    wft = kv_w[:, 2 * FA_WIDTH:].T.astype(BF16)
    bft = kv_b[2 * FA_WIDTH:, None]
    return wkv, bkv, wkvt, bkvt, wft, bft


def _kv_prompt(x, kv_w, kv_b, batch, seq, mod, w_in, b_in):
    tm = ATT_TILE
    nt = seq // tm
    wkv, bkv, wkvt, bkvt, wft, bft = _kv_weights(kv_w, kv_b)
    weights = (wkv[:, FA_WIDTH:], bkv[:, FA_WIDTH:], wkvt, bkvt, wft, bft)
    q_weights = (w_in.astype(BF16), b_in[None])
    row = lambda b, i: (b * nt + i, 0)
    return pl.pallas_call(
        functools.partial(_kv_kernel, prompt=True),
        grid=(batch, nt),
        in_specs=([pl.BlockSpec((tm, D_MODEL), row)] + [_const_spec(w.shape) for w in weights]
                  + [pl.BlockSpec((1, 1, 3 * D_MODEL), lambda b, i: (b, 0, 0))]
                  + [_const_spec(w.shape) for w in q_weights]),
        out_specs=[pl.BlockSpec((1, FA_WIDTH, tm), lambda b, i: (b, 0, i)),
                   pl.BlockSpec((1, FA_WIDTH, tm), lambda b, i: (b, 0, i)),
                   pl.BlockSpec((1, FA_HEADS, tm), lambda b, i: (b, 0, i)),
                   pl.BlockSpec((1, HEAD_PAIRS, 1, LANES, tm), lambda b, i: (b, 0, i, 0, 0)),
                   pl.BlockSpec((1, HEAD_PAIRS, tm, LANES), lambda b, i: (b, 0, i, 0)),
                   pl.BlockSpec((1, HEAD_PAIRS, 1, 2, tm), lambda b, i: (b, 0, i, 0, 0)),
                   pl.BlockSpec((1, HEAD_PAIRS, tm, LANES), lambda b, i: (b, 0, i, 0)),
                   pl.BlockSpec((tm, FA_WIDTH), row)],
        out_shape=[jax.ShapeDtypeStruct((batch, FA_WIDTH, seq), F32),
                   jax.ShapeDtypeStruct((batch, FA_WIDTH, seq), F32),
                   jax.ShapeDtypeStruct((batch, FA_HEADS, seq), F32),
                   jax.ShapeDtypeStruct((batch, HEAD_PAIRS, nt, LANES, tm), BF16),
                   jax.ShapeDtypeStruct((batch, HEAD_PAIRS, seq, LANES), BF16),
                   jax.ShapeDtypeStruct((batch, HEAD_PAIRS, nt, 2, tm), F32),
                   jax.ShapeDtypeStruct((batch, HEAD_PAIRS, seq, LANES), BF16),
                   jax.ShapeDtypeStruct((batch * seq, FA_WIDTH), BF16)],
        scratch_shapes=[pltpu.VMEM((FA_HEADS, LANES), F32)],
        compiler_params=_params(2),
        name="kv_prompt",
    )(x, *weights, mod, *q_weights)


def _kv_sample(x, kv_w, kv_b):
    rows = x.shape[0]
    weights = _kv_weights(kv_w, kv_b)
    wf = jnp.pad(kv_w[:, 2 * FA_WIDTH:], ((0, 0), (0, LANES - FA_HEADS))).astype(BF16)
    bf = jnp.pad(kv_b[2 * FA_WIDTH:], (0, LANES - FA_HEADS))[None]
    weights = weights + (wf, bf)
    full = lambda i: (0, 0)
    full3 = lambda i: (0, 0, 0)
    return pl.pallas_call(
        functools.partial(_kv_kernel, prompt=False),
        grid=(1,),
        in_specs=[pl.BlockSpec((rows, D_MODEL), full)] + [_const_spec(w.shape) for w in weights],
        out_specs=[pl.BlockSpec((1, FA_WIDTH, rows), full3), pl.BlockSpec((1, FA_WIDTH, rows), full3),
                   pl.BlockSpec((1, FA_HEADS, rows), full3),
                   pl.BlockSpec((rows, FA_WIDTH), full), pl.BlockSpec((rows, FA_WIDTH), full),
                   pl.BlockSpec((rows, FA_HEADS), full)],
        out_shape=[jax.ShapeDtypeStruct((1, FA_WIDTH, rows), F32), jax.ShapeDtypeStruct((1, FA_WIDTH, rows), F32),
                   jax.ShapeDtypeStruct((1, FA_HEADS, rows), F32),
                   jax.ShapeDtypeStruct((rows, FA_WIDTH), F32), jax.ShapeDtypeStruct((rows, FA_WIDTH), F32),
                   jax.ShapeDtypeStruct((rows, FA_HEADS), F32)],
        compiler_params=_params(1),
        name="kv_sample",
    )(x, *weights)


def _q_proj_kernel(x_ref, mod_ref, w_ref, b_ref, q_ref, gz_ref, *, pair_major):
    _q_proj_store(x_ref[...], mod_ref[0], w_ref, b_ref, q_ref, gz_ref, pair_major)


def _q_proj_store(x, m, w_ref, b_ref, q_ref, gz_ref, pair_major):
    h = (x * (1.0 + m[:, D_MODEL:2 * D_MODEL]) + m[:, :D_MODEL]).astype(BF16)
    p = _dot(h, w_ref[...]) + b_ref[...]
    if pair_major:
        q = (p[:, :FA_WIDTH] * (LOG2E * FA_HEAD_DIM ** -0.5)).astype(BF16)
        for j in range(HEAD_PAIRS):
            q_ref[0, j] = q[:, j * LANES:(j + 1) * LANES]
    else:
        q_ref[...] = (p[:, :FA_WIDTH] * (FA_HEAD_DIM ** -0.5)).astype(BF16)
    gz_ref[...] = jax.nn.silu(p[:, FA_WIDTH:]).astype(gz_ref.dtype)


def _q_proj(x, mod, w_in, b_in, *, batch=None, seq=None):
    rows = x.shape[0]
    nb, mrows = mod.shape[0], mod.shape[1]
    pair_major = batch is not None
    tm = ROW_TILE if pair_major else rows
    tpb = rows // nb // tm
    if pair_major:
        q_spec = pl.BlockSpec((1, HEAD_PAIRS, tm, LANES), lambda i: (i // tpb, 0, i % tpb, 0))
        q_shape = jax.ShapeDtypeStruct((batch, HEAD_PAIRS, seq, LANES), BF16)
    else:
        q_spec = pl.BlockSpec((tm, FA_WIDTH), lambda i: (i, 0))
        q_shape = jax.ShapeDtypeStruct((rows, FA_WIDTH), BF16)
    return pl.pallas_call(
        functools.partial(_q_proj_kernel, pair_major=pair_major),
        grid=(rows // tm,),
        in_specs=[pl.BlockSpec((tm, D_MODEL), lambda i: (i, 0)),
                  pl.BlockSpec((1, mrows, 3 * D_MODEL), lambda i: (i // tpb, 0, 0)),
                  _const_spec((D_MODEL, 2 * FA_WIDTH)), _const_spec((1, 2 * FA_WIDTH))],
        out_specs=[q_spec, pl.BlockSpec((tm, FA_WIDTH), lambda i: (i, 0))],
        out_shape=[q_shape, jax.ShapeDtypeStruct((rows, FA_WIDTH), BF16)],
        compiler_params=_params(1),
        name="q_proj_prompt" if pair_major else "q_proj_sample",
    )(x, mod, w_in.astype(BF16), b_in[None])


def _out_proj_kernel(o_ref, gz_ref, x_ref, mod_ref, w_ref, b_ref, g_ref, bb_ref, *rest, next_q):
    if next_q:
        modn_ref, wq_ref, bq_ref, y_ref, q_ref, gzn_ref = rest
    else:
        (y_ref,) = rest
    t = (o_ref[...].astype(F32) * gz_ref[...].astype(F32)).astype(BF16)
    y = _dot(t, w_ref[...]) + b_ref[...]
    gate = mod_ref[0][:, 2 * D_MODEL:]
    r = DEEPNORM_ALPHA * x_ref[...] + (1.0 + gate) * y
    x_new = _layer_norm(r, g_ref[...], bb_ref[...])
    y_ref[...] = x_new
    if next_q:
        _q_proj_store(x_new, modn_ref[0], wq_ref, bq_ref, q_ref, gzn_ref, True)


def _out_proj(o, gz, x, mod, w_out, b_out, ln_g, ln_b, *, tm, next_q=None):
    rows = x.shape[0]
    nb, mrows = mod.shape[0], mod.shape[1]
    tpb = rows // nb // tm
    row = lambda i: (i, 0)
    mod_spec = pl.BlockSpec((1, mrows, 3 * D_MODEL), lambda i: (i // tpb, 0, 0))
    in_specs = [pl.BlockSpec((tm, FA_WIDTH), row), pl.BlockSpec((tm, FA_WIDTH), row),
                pl.BlockSpec((tm, D_MODEL), row), mod_spec,
                _const_spec((FA_WIDTH, D_MODEL)), _const_spec((1, D_MODEL)),
                _const_spec((1, D_MODEL)), _const_spec((1, D_MODEL))]
    args = [o, gz, x, mod, w_out.astype(BF16), b_out[None], ln_g[None], ln_b[None]]
    out_specs = [pl.BlockSpec((tm, D_MODEL), row)]
    out_shape = [jax.ShapeDtypeStruct((rows, D_MODEL), F32)]
    if next_q is not None:
        mod_n, w_in, b_in, batch, seq = next_q
        in_specs += [mod_spec, _const_spec((D_MODEL, 2 * FA_WIDTH)), _const_spec((1, 2 * FA_WIDTH))]
        args += [mod_n, w_in.astype(BF16), b_in[None]]
        out_specs += [pl.BlockSpec((1, HEAD_PAIRS, tm, LANES), lambda i: (i // tpb, 0, i % tpb, 0)),
                      pl.BlockSpec((tm, FA_WIDTH), row)]
        out_shape += [jax.ShapeDtypeStruct((batch, HEAD_PAIRS, seq, LANES), BF16),
                      jax.ShapeDtypeStruct((rows, FA_WIDTH), BF16)]
    res = pl.pallas_call(
        functools.partial(_out_proj_kernel, next_q=next_q is not None),
        grid=(rows // tm,),
        in_specs=in_specs,
        out_specs=out_specs,
        out_shape=out_shape,
        compiler_params=_params(1),
        name="out_proj",
    )(*args)
    return res if next_q is not None else res[0]


def _flash_q_tile(i, q2, k_ref, v_ref, c_ref, scr, tile):
    s_scr = (scr[0:2], scr[2:4])
    m_scr = (scr[4:6], scr[6:8])
    p_scr, alpha_scr, acc_scr = scr[8:10], scr[10:12], scr[12:14]
    lane = lax.broadcasted_iota(jnp.int32, (1, LANES), 1)
    first = lane < FA_HEAD_DIM
    sub8 = lax.broadcasted_iota(jnp.int32, (8, tile), 0)

    half = tile // 2

    def logits(t, q_heads, buf, diagonal=False):
        c2 = c_ref[0, 0, t]
        hi = c2.astype(BF16).astype(F32)
        mid = (c2 - hi).astype(BF16).astype(F32)
        lo = c2 - hi - mid
        pieces = (hi[0:1], mid[0:1], lo[0:1], hi[1:2], mid[1:2], lo[1:2])
        c_rows = jnp.zeros((8, tile), F32)
        for r, piece in enumerate(pieces):
            c_rows = jnp.where(sub8 == r, piece, c_rows)
        c_block = jnp.concatenate([c_rows, jnp.zeros((LANES - 8, tile), F32)], axis=0).astype(BF16)
        k_aug = jnp.concatenate([k_ref[0, 0, t], c_block], axis=0)
        for a in range(2):
            if diagonal:
                s_scr[buf][a][:half, :half] = _dot(q_heads[a][:half], k_aug[:, :half])
                s_scr[buf][a][half:, :] = _dot(q_heads[a][half:], k_aug)
            else:
                s_scr[buf][a][...] = _dot(q_heads[a], k_aug)

    def softmax_blocks(a, buf, masked):
        for r0 in range(0, tile, FLASH_ROWS):
            rows = slice(r0, r0 + FLASH_ROWS)
            ncols = min(tile, -(-(r0 + FLASH_ROWS) // LANES) * LANES) if masked else tile
            s = s_scr[buf][a][rows, :ncols]
            if masked:
                r_i = r0 + lax.broadcasted_iota(jnp.int32, (FLASH_ROWS, ncols), 0)
                c_i = lax.broadcasted_iota(jnp.int32, (FLASH_ROWS, ncols), 1)
                s = jnp.where(c_i <= r_i, s, NEG_BIG)
            m_old = m_scr[buf][a][rows, :]
            m_new = jnp.maximum(m_old, jnp.max(s, axis=-1, keepdims=True))
            m_scr[1 - buf][a][rows, :] = m_new
            alpha_scr[a][rows, :] = jnp.exp2(m_old - m_new)
            m_wide = jnp.concatenate([m_new] * (ncols // LANES), axis=1)
            p_scr[a][rows, :ncols] = jnp.exp2((s - m_wide).astype(BF16))
            fill_to = half if r0 < half else tile
            if masked and ncols < fill_to:
                p_scr[a][rows, ncols:fill_to] = jnp.zeros((FLASH_ROWS, fill_to - ncols), BF16)

    def consume(t, buf, masked):
        v2 = v_ref[0, 0, pl.ds(pl.multiple_of(t * tile, tile), tile), :]
        one = jnp.ones_like(v2)
        v_heads = (jnp.where(first, v2, one), jnp.where(first, one, v2))
        for a in range(2):
            softmax_blocks(a, buf, masked)
            if masked:
                pv = jnp.concatenate([_dot(p_scr[a][:half, :half], v_heads[a][:half]),
                                      _dot(p_scr[a][half:, :], v_heads[a])], axis=0)
            else:
                pv = _dot(p_scr[a][...], v_heads[a])
            acc_scr[a][...] = alpha_scr[a][...] * acc_scr[a][...] + pv

    col_q = lax.broadcasted_iota(jnp.int32, (tile, LANES), 1)
    zero = jnp.zeros_like(q2)
    q_heads = (jnp.concatenate([jnp.where(first, q2, zero),
                                jnp.where(col_q < 3, -1.0, 0.0).astype(BF16)], axis=1),
               jnp.concatenate([jnp.where(first, zero, q2),
                                jnp.where((col_q >= 3) & (col_q < 6), -1.0, 0.0).astype(BF16)], axis=1))
    for a in range(2):
        m_scr[0][a][...] = jnp.full((tile, LANES), NEG_BIG, F32)
        acc_scr[a][...] = jnp.zeros((tile, LANES), F32)

    logits(0, q_heads, 0)

    def tile_pair(u, _):
        logits(2 * u + 1, q_heads, 1)
        consume(2 * u, 0, False)
        logits(2 * u + 2, q_heads, 0)
        consume(2 * u + 1, 1, False)
        return 0

    lax.fori_loop(0, i // 2, tile_pair, 0)

    @pl.when(i % 2 == 1)
    def _():
        logits(i, q_heads, 1, diagonal=True)
        consume(i - 1, 0, False)
        consume(i, 1, True)

    @pl.when(i % 2 == 0)
    def _():
        consume(i, 0, True)

    acc_a, acc_b = acc_scr[0][...], acc_scr[1][...]
    o_a = acc_a / pltpu.roll(acc_a, FA_HEAD_DIM, axis=1)
    o_b = acc_b / pltpu.roll(acc_b, FA_HEAD_DIM, axis=1)
    return jnp.where(first, o_a, o_b)


def _decode_group(g, n_groups, q_ref, kn_ref, vn_ref, lfn_ref, k_refs, v_refs, lf_refs, o_ref,
                  m_scr, l_scr, acc_scr, off_scr):
    pg = len(k_refs)

    @pl.when(g == 0)
    def _():
        m_scr[...] = jnp.full(m_scr.shape, NEG_BIG, F32)
        l_scr[...] = jnp.zeros(l_scr.shape, F32)
        acc_scr[...] = jnp.zeros(acc_scr.shape, F32)
        off_scr[...] = jnp.zeros(off_scr.shape, F32)

    head = lax.broadcasted_iota(jnp.int32, (FA_HEADS, FA_WIDTH), 0)
    col = lax.broadcasted_iota(jnp.int32, (FA_HEADS, FA_WIDTH), 1)
    own = (col >= head * FA_HEAD_DIM) & (col < (head + 1) * FA_HEAD_DIM)
    q_rows = jnp.where(own, jnp.broadcast_to(q_ref[0].astype(F32), (FA_HEADS, FA_WIDTH)), 0.0)
    q_bf = q_rows.astype(BF16)

    r_i = lax.broadcasted_iota(jnp.int32, (PAGE_SIZE, PAGE_SIZE), 0)
    c_i = lax.broadcasted_iota(jnp.int32, (PAGE_SIZE, PAGE_SIZE), 1)
    upper = jnp.where(r_i <= c_i, 1.0, 0.0).astype(BF16)
    lf_all = jnp.concatenate([lf_refs[p][0] for p in range(pg)], axis=0)
    cum_all = _split3_dot(lf_all, upper)
    off = off_scr[:, 0:1]
    s_parts = []
    for p in range(pg):
        cum_p = cum_all[p * FA_HEADS:(p + 1) * FA_HEADS]
        s_parts.append(_dot(q_bf, k_refs[p][0].astype(BF16)) - (off + cum_p))
        off = off + cum_p[:, PAGE_SIZE - 1:PAGE_SIZE]
    off_scr[...] = jnp.broadcast_to(off, off_scr.shape)
    s = jnp.concatenate(s_parts, axis=1)
    m_old = m_scr[:, 0:1]
    m_new = jnp.maximum(m_old, jnp.max(s, axis=1, keepdims=True))
    alpha = jnp.exp(m_old - m_new)
    prob = jnp.exp(s - m_new)
    l_new = alpha * l_scr[:, 0:1] + jnp.sum(prob, axis=1, keepdims=True)
    pb = prob.astype(BF16)
    acc = alpha * acc_scr[...]
    for p in range(pg):
        acc = acc + _dot_nt(pb[:, p * PAGE_SIZE:(p + 1) * PAGE_SIZE], v_refs[p][0].astype(BF16))
    m_scr[...] = jnp.broadcast_to(m_new, m_scr.shape)
    l_scr[...] = jnp.broadcast_to(l_new, l_scr.shape)
    acc_scr[...] = acc

    @pl.when(g == n_groups - 1)
    def _():
        c_query = off + lfn_ref[0]
        s_new = jnp.sum(q_rows * kn_ref[0], axis=1, keepdims=True)
        m_past = m_new + c_query
        m_fin = jnp.maximum(m_past, s_new)
        w_past = jnp.exp(m_past - m_fin)
        w_new = jnp.exp(s_new - m_fin)
        o_full = (acc * w_past + w_new * vn_ref[0]) / (l_new * w_past + w_new)
        o_ref[0] = jnp.sum(jnp.where(own, o_full, 0.0), axis=0, keepdims=True)


def _attention_kernel(pt_ref, q_ref, k_ref, v_ref, c_ref, qs_ref, kn_ref, vn_ref, lfn_ref, *refs,
                      tile, pg, n_groups):
    del pt_ref
    k_refs, v_refs, lf_refs = refs[:pg], refs[pg:2 * pg], refs[2 * pg:3 * pg]
    o_ref, os_ref = refs[3 * pg:3 * pg + 2]
    scr = refs[3 * pg + 2:]
    i2 = pl.program_id(2)

    def q_tile(h, _):
        rows = pl.ds(pl.multiple_of(h * tile, tile), tile)
        o = _flash_q_tile(Q_TILES_PER_STEP * i2 + h, q_ref[0, 0, rows, :], k_ref, v_ref, c_ref, scr[:14], tile)
        o_ref[0, rows, :] = o.astype(o_ref.dtype)
        return 0

    lax.fori_loop(0, Q_TILES_PER_STEP, q_tile, 0)
    step = (pl.program_id(0) * pl.num_programs(1) + pl.program_id(1)) * pl.num_programs(2) + i2
    _decode_group(step % n_groups, n_groups, qs_ref, kn_ref, vn_ref, lfn_ref, k_refs, v_refs, lf_refs, os_ref,
                  *scr[14:])


def _attention_layer(q_pm, k_pm, v_pm, c_t, batch, seq,
                     q_s, k_new, v_new, lf_new, cache_kt, cache_vt, cache_lft, page_table):
    tile = ATT_TILE
    nt = seq // tile
    qt = Q_TILES_PER_STEP
    nq = nt // qt
    n, n_pages = page_table.shape
    steps = batch * HEAD_PAIRS * nq
    n_groups = steps // n
    pg = n_pages // n_groups
    assert nq * qt == nt and n_groups * n == steps and pg * n_groups == n_pages

    def sample_of(b, j, i):
        return ((b * HEAD_PAIRS + j) * nq + i) // n_groups

    def page_map(p):
        def index(b, j, i, pt):
            step = (b * HEAD_PAIRS + j) * nq + i
            return (pt[step // n_groups, (step % n_groups) * pg + p], 0, 0)
        return index

    row = lambda b, j, i, pt: (sample_of(b, j, i), 0, 0)
    in_specs = [pl.BlockSpec((1, 1, qt * tile, LANES), lambda b, j, i, pt: (b, j, i, 0)),
                pl.BlockSpec((1, 1, nt, LANES, tile), lambda b, j, i, pt: (b, j, 0, 0, 0)),
                pl.BlockSpec((1, 1, seq, LANES), lambda b, j, i, pt: (b, j, 0, 0)),
                pl.BlockSpec((1, 1, nt, 2, tile), lambda b, j, i, pt: (b, j, 0, 0, 0)),
                pl.BlockSpec((1, 1, FA_WIDTH), row), pl.BlockSpec((1, 1, FA_WIDTH), row),
                pl.BlockSpec((1, 1, FA_WIDTH), row), pl.BlockSpec((1, FA_HEADS, 1), row)]
    in_specs += [pl.BlockSpec((1, FA_WIDTH, PAGE_SIZE), page_map(p)) for p in range(pg)]
    in_specs += [pl.BlockSpec((1, FA_WIDTH, PAGE_SIZE), page_map(p)) for p in range(pg)]
    in_specs += [pl.BlockSpec((1, FA_HEADS, PAGE_SIZE), page_map(p)) for p in range(pg)]
    o_p, o_s = pl.pallas_call(
        functools.partial(_attention_kernel, tile=tile, pg=pg, n_groups=n_groups),
        grid_spec=pltpu.PrefetchScalarGridSpec(
            num_scalar_prefetch=1, grid=(batch, HEAD_PAIRS, nq), in_specs=in_specs,
            out_specs=[pl.BlockSpec((1, qt * tile, LANES), lambda b, j, i, pt: (b, i, j)),
                       pl.BlockSpec((1, 1, FA_WIDTH), row)],
            scratch_shapes=([pltpu.VMEM((tile, tile), F32)] * 4
                            + [pltpu.VMEM((tile, LANES), F32)] * 4
                            + [pltpu.VMEM((tile, tile), BF16)] * 2
                            + [pltpu.VMEM((tile, LANES), F32)] * 2
                            + [pltpu.VMEM((tile, LANES), F32)] * 2
                            + [pltpu.VMEM((FA_HEADS, LANES), F32)] * 2
                            + [pltpu.VMEM((FA_HEADS, FA_WIDTH), F32)]
                            + [pltpu.VMEM((FA_HEADS, LANES), F32)])),
        out_shape=[jax.ShapeDtypeStruct((batch, seq, FA_WIDTH), BF16),
                   jax.ShapeDtypeStruct((n, 1, FA_WIDTH), F32)],
        compiler_params=_params(3),
        name="attention_layer",
    )(page_table, q_pm, k_pm, v_pm, c_t,
      q_s.reshape(n, 1, FA_WIDTH), k_new.reshape(n, 1, FA_WIDTH), v_new.reshape(n, 1, FA_WIDTH),
      lf_new.reshape(n, FA_HEADS, 1), *([cache_kt] * pg), *([cache_vt] * pg), *([cache_lft] * pg))
    return o_p, o_s.reshape(n, FA_WIDTH)


def kernel(x_prompt, x_sample, c_prompt, c_sample, cache_k, cache_v, cache_logf, page_table, w_ada, b_ada, ln_g, ln_b, cm_w_in, cm_b_in, cm_ln_v_g, cm_ln_v_b, cm_w_s, cm_b_s, cm_w_out, cm_b_out, kv_w, kv_b, fa_w_in, fa_b_in, fa_w_out, fa_b_out):
    batch, seq, _ = x_prompt.shape
    n_dec = x_sample.shape[0]

    n_c = batch + n_dec
    c_all = jnp.pad(jnp.concatenate([c_prompt, c_sample], axis=0), ((0, (-n_c) % 8), (0, 0)))
    mod = _ada_modulation(c_all, w_ada, b_ada)
    mod_p = mod[:, :batch].reshape(DEPTH, batch, 1, 3 * D_MODEL)
    mod_s = mod[:, batch:n_c].reshape(DEPTH, 1, n_dec, 3 * D_MODEL)

    xp = x_prompt.reshape(batch * seq, D_MODEL)
    xs = x_sample.reshape(n_dec, D_MODEL)
    chunk_rows = []
    for layer in range(N_A_LAYERS):
        args = (cm_w_in[layer], cm_b_in[layer], cm_ln_v_g[layer], cm_ln_v_b[layer], cm_w_s[layer], cm_b_s[layer],
                cm_w_out[layer], cm_b_out[layer], ln_g[layer], ln_b[layer])
        (xp,) = _chunk_mlp_layer(xp, mod_p[layer], *args, single_row_chunks=False)
        xs, vn_s = _chunk_mlp_layer(xs, mod_s[layer], *args, single_row_chunks=True)
        chunk_rows.append(vn_s)

    kt_p, vt_p, lft_p, kb_p, vb_p, ct_p, q_pm, gz_p = _kv_prompt(
        xp, kv_w, kv_b, batch, seq, mod_p[N_A_LAYERS], fa_w_in[0], fa_b_in[0])
    kt_s, vt_s, lft_s, k_s, v_s, lf_s = _kv_sample(xs, kv_w, kv_b)

    n_phys = cache_k.shape[0]
    cache_kt = jnp.transpose(cache_k, (0, 2, 3, 1)).reshape(n_phys, FA_WIDTH, PAGE_SIZE)
    cache_vt = jnp.transpose(cache_v, (0, 2, 3, 1)).reshape(n_phys, FA_WIDTH, PAGE_SIZE)
    cache_lft = jnp.swapaxes(cache_logf, 1, 2)

    for j in range(DEPTH - N_A_LAYERS):
        layer = N_A_LAYERS + j
        q_s, gz_s = _q_proj(xs, mod_s[layer], fa_w_in[j], fa_b_in[j])
        o_p, o_s = _attention_layer(q_pm, kb_p, vb_p, ct_p, batch, seq,
                                    q_s, k_s, v_s, lf_s, cache_kt, cache_vt, cache_lft, page_table)
        next_q = ((mod_p[layer + 1], fa_w_in[j + 1], fa_b_in[j + 1], batch, seq) if layer + 1 < DEPTH else None)
        res = _out_proj(o_p.reshape(batch * seq, FA_WIDTH), gz_p, xp, mod_p[layer], fa_w_out[j], fa_b_out[j],
                        ln_g[layer], ln_b[layer], tm=ROW_TILE, next_q=next_q)
        xp, q_pm, gz_p = res if next_q is not None else (res, None, None)
        xs = _out_proj(o_s, gz_s, xs, mod_s[layer], fa_w_out[j], fa_b_out[j], ln_g[layer], ln_b[layer], tm=n_dec)

    y_prompt = xp.reshape(batch, seq, D_MODEL)
    y_sample = xs.reshape(n_dec, 1, D_MODEL)
    k_prompt = jnp.transpose(kt_p.reshape(batch, FA_HEADS, FA_HEAD_DIM, seq), (0, 3, 1, 2))
    v_prompt = jnp.transpose(vt_p.reshape(batch, FA_HEADS, FA_HEAD_DIM, seq), (0, 3, 1, 2))
    logf_prompt = jnp.transpose(lft_p, (0, 2, 1))
    k_sample = jnp.transpose(kt_s.reshape(1, FA_HEADS, FA_HEAD_DIM, n_dec), (3, 0, 1, 2))
    v_sample = jnp.transpose(vt_s.reshape(1, FA_HEADS, FA_HEAD_DIM, n_dec), (3, 0, 1, 2))
    logf_sample = jnp.transpose(lft_s, (2, 0, 1))
    chunk_v_sample = jnp.stack(chunk_rows).reshape(N_A_LAYERS, n_dec, 1, CM_WIDTH)
    return (y_prompt, y_sample, k_prompt, v_prompt, logf_prompt, k_sample, v_sample, logf_sample, chunk_v_sample)
```

```python
import functools

import jax
import jax.numpy as jnp
from jax import lax
from jax.experimental import pallas as pl
from jax.experimental.pallas import tpu as pltpu

F32 = jnp.float32
BF16 = jnp.bfloat16

D_MODEL = 1024
DEPTH = 4
N_A_LAYERS = 2
CHUNK = 128
CM_WIDTH = 2048
CM_GROUPS = 8
CM_GROUP_DIM = CM_WIDTH // CM_GROUPS
FA_HEADS = 16
FA_HEAD_DIM = 64
FA_WIDTH = FA_HEADS * FA_HEAD_DIM
HEAD_PAIRS = FA_HEADS // 2
PAGE_SIZE = 128
DEEPNORM_ALPHA = (2.0 * DEPTH) ** 0.25
LN_EPS = 1e-5
LANES = 128
NEG_BIG = -1e30
VMEM_LIMIT = 56 * 1024 * 1024

ROW_TILE = 512
ATT_TILE = 512
FLASH_ROWS = 32
Q_TILES_PER_STEP = 2
LOG2E = 1.4426950408889634


def _params(n_axes, vmem=VMEM_LIMIT):
    return pltpu.CompilerParams(dimension_semantics=("arbitrary",) * n_axes, vmem_limit_bytes=vmem)


def _dot(a, b):
    return jnp.dot(a, b, preferred_element_type=F32)


def _dot_nt(a, b):
    return lax.dot_general(a, b, (((1,), (1,)), ((), ())), preferred_element_type=F32)


def _layer_norm(x, g, b):
    mu = jnp.mean(x, axis=-1, keepdims=True)
    d = x - mu
    var = jnp.mean(d * d, axis=-1, keepdims=True)
    return d * lax.rsqrt(var + LN_EPS) * g + b


def _log_sigmoid(x):
    return -(jnp.maximum(-x, 0.0) + jnp.log1p(jnp.exp(-jnp.abs(x))))


def _split3_dot(x, w):
    hi = x.astype(BF16)
    r1 = x - hi.astype(F32)
    mid = r1.astype(BF16)
    lo = (r1 - mid.astype(F32)).astype(BF16)
    return _dot(hi, w) + _dot(mid, w) + _dot(lo, w)


def _const_spec(shape):
    nd = len(shape)
    return pl.BlockSpec(shape, lambda *_: (0,) * nd, pipeline_mode=pl.Buffered(1))


def _ada_kernel(c_ref, w_ref, b_ref, o_ref):
    a = jax.nn.silu(c_ref[...]).astype(BF16)
    o_ref[0] = _dot(a, w_ref[0].astype(BF16)) + b_ref[0]


def _ada_modulation(c_all, w_ada, b_ada):
    n = c_all.shape[0]
    nt = 3 * D_MODEL // D_MODEL
    return pl.pallas_call(
        _ada_kernel,
        grid=(DEPTH, nt),
        in_specs=[pl.BlockSpec((n, D_MODEL), lambda l, j: (0, 0)),
                  pl.BlockSpec((1, D_MODEL, D_MODEL), lambda l, j: (l, 0, j)),
                  pl.BlockSpec((1, 1, D_MODEL), lambda l, j: (l, 0, j))],
        out_specs=pl.BlockSpec((1, n, D_MODEL), lambda l, j: (l, 0, j)),
        out_shape=jax.ShapeDtypeStruct((DEPTH, n, 3 * D_MODEL), F32),
        compiler_params=_params(2),
        name="ada_modulation",
    )(c_all, w_ada, b_ada.reshape(DEPTH, 1, 3 * D_MODEL))


def _chunk_mlp_kernel(x_ref, mod_ref, win_ref, bin_ref, lvg_ref, lvb_ref, ws_ref, bs_ref,
                      wout_ref, bout_ref, lng_ref, lnb_ref, *rest, single_row_chunks):
    if single_row_chunks:
        o_ref, vn_ref, t_scr = rest
    else:
        o_ref, t_scr = rest
    rows = x_ref.shape[0]
    x = x_ref[...]
    m = mod_ref[0]
    shift, scale, gate = m[:, :D_MODEL], m[:, D_MODEL:2 * D_MODEL], m[:, 2 * D_MODEL:]
    h = (x * (1.0 + scale) + shift).astype(BF16)

    v = jax.nn.gelu(_dot(h, win_ref[:, CM_WIDTH:2 * CM_WIDTH]) + bin_ref[:, CM_WIDTH:2 * CM_WIDTH])
    vn = _layer_norm(v, lvg_ref[...], lvb_ref[...])
    if single_row_chunks:
        vn_ref[...] = vn
    else:
        r_i = lax.broadcasted_iota(jnp.int32, (CHUNK, CHUNK), 0)
        c_i = lax.broadcasted_iota(jnp.int32, (CHUNK, CHUNK), 1)
        causal = c_i <= r_i

    for g in range(CM_GROUPS):
        lo, hi = g * CM_GROUP_DIM, (g + 1) * CM_GROUP_DIM
        vn_g = vn[:, lo:hi]
        if single_row_chunks:
            sg = vn_g * ws_ref[:, lo:hi] + bs_ref[:, lo:hi]
        else:
            w_g = jnp.where(causal, ws_ref[g], 0.0).astype(BF16)
            b_g = bs_ref[:, g:g + 1]
            vb = vn_g.astype(BF16)
            sg = jnp.concatenate(
                [_dot(w_g, vb[c * CHUNK:(c + 1) * CHUNK]) + b_g for c in range(rows // CHUNK)], axis=0)
        u_g = jax.nn.gelu(_dot(h, win_ref[:, lo:hi]) + bin_ref[:, lo:hi])
        z_g = _dot(h, win_ref[:, 2 * CM_WIDTH + lo:2 * CM_WIDTH + hi]) + bin_ref[:, 2 * CM_WIDTH + lo:2 * CM_WIDTH + hi]
        t_scr[:, lo:hi] = (u_g * sg * jax.nn.silu(z_g)).astype(BF16)

    y = _dot(t_scr[...], wout_ref[...]) + bout_ref[...]
    r = DEEPNORM_ALPHA * x + (1.0 + gate) * y
    o_ref[...] = _layer_norm(r, lng_ref[...], lnb_ref[...])


def _chunk_mlp_layer(x, mod, w_in, b_in, lvg, lvb, w_s, b_s, w_out, b_out, ln_g, ln_b, *, single_row_chunks):
    rows = x.shape[0]
    nb, mrows = mod.shape[0], mod.shape[1]
    tm = rows if single_row_chunks else ROW_TILE
    tiles_per_batch = rows // nb // tm
    if single_row_chunks:
        ws_arg = jnp.repeat(w_s[:, 0, 0], CM_GROUP_DIM)[None]
        bs_arg = jnp.repeat(b_s[:, 0], CM_GROUP_DIM)[None]
    else:
        ws_arg, bs_arg = w_s, b_s.T
    out_shape = [jax.ShapeDtypeStruct((rows, D_MODEL), F32)]
    out_specs = [pl.BlockSpec((tm, D_MODEL), lambda i: (i, 0))]
    if single_row_chunks:
        out_shape.append(jax.ShapeDtypeStruct((rows, CM_WIDTH), F32))
        out_specs.append(pl.BlockSpec((tm, CM_WIDTH), lambda i: (i, 0)))
    res = pl.pallas_call(
        functools.partial(_chunk_mlp_kernel, single_row_chunks=single_row_chunks),
        grid=(rows // tm,),
        in_specs=[pl.BlockSpec((tm, D_MODEL), lambda i: (i, 0)),
                  pl.BlockSpec((1, mrows, 3 * D_MODEL), lambda i: (i // tiles_per_batch, 0, 0)),
                  _const_spec((D_MODEL, 3 * CM_WIDTH)), _const_spec((1, 3 * CM_WIDTH)),
                  _const_spec((1, CM_WIDTH)), _const_spec((1, CM_WIDTH)),
                  _const_spec(ws_arg.shape), _const_spec(bs_arg.shape),
                  _const_spec((CM_WIDTH, D_MODEL)), _const_spec((1, D_MODEL)),
                  _const_spec((1, D_MODEL)), _const_spec((1, D_MODEL))],
        out_specs=out_specs,
        out_shape=out_shape,
        scratch_shapes=[pltpu.VMEM((tm, CM_WIDTH), BF16)],
        compiler_params=_params(1),
        name="chunk_mlp_sample" if single_row_chunks else "chunk_mlp_prompt",
    )(x, mod, w_in.astype(BF16), b_in[None], lvg[None], lvb[None], ws_arg, bs_arg,
      w_out.astype(BF16), b_out[None], ln_g[None], ln_b[None])
    return res


def _kv_kernel(x_ref, wkv_ref, bkv_ref, wkvt_ref, bkvt_ref, wft_ref, bft_ref, *rest, prompt):
    if prompt:
        mod_ref, wq_ref, bq_ref, kt_ref, vt_ref, lft_ref, kb_ref, vb_ref, ct_ref, q_ref, gz_ref, carry_scr = rest
    else:
        wf_ref, bf_ref, kt_ref, vt_ref, lft_ref, k_ref, v_ref, lf_ref = rest
    rows = x_ref.shape[0]
    xb = x_ref[...].astype(BF16)
    kvt = _dot_nt(wkvt_ref[...], xb) + bkvt_ref[...]
    kt_ref[0] = kvt[:FA_WIDTH]
    vt_ref[0] = kvt[FA_WIDTH:]
    lft = _log_sigmoid(_dot_nt(wft_ref[...], xb) + bft_ref[...])
    lft_ref[0] = lft
    if not prompt:
        kv = _dot(xb, wkv_ref[...]) + bkv_ref[...]
        k_ref[...] = kv[:, :FA_WIDTH]
        v_ref[...] = kv[:, FA_WIDTH:]
        lf_ref[...] = _log_sigmoid(_dot(xb, wf_ref[...]) + bf_ref[...])[:, :FA_HEADS]
    else:
        ktb = kvt[:FA_WIDTH].astype(BF16)
        vb = kvt[FA_WIDTH:].T.astype(BF16)
        for j in range(HEAD_PAIRS):
            kb_ref[0, j, 0] = ktb[j * LANES:(j + 1) * LANES, :]
            vb_ref[0, j] = vb[:, j * LANES:(j + 1) * LANES]
        r_i = lax.broadcasted_iota(jnp.int32, (rows, rows), 0)
        c_i = lax.broadcasted_iota(jnp.int32, (rows, rows), 1)
        upper = jnp.where(r_i <= c_i, 1.0, 0.0).astype(BF16)

        @pl.when(pl.program_id(1) == 0)
        def _():
            carry_scr[...] = jnp.zeros_like(carry_scr)

        ct = _split3_dot(lft, upper) + carry_scr[:, 0:1]
        carry_scr[...] = jnp.broadcast_to(ct[:, rows - 1:rows], carry_scr.shape)
        ct2 = ct * LOG2E
        for j in range(HEAD_PAIRS):
            ct_ref[0, j, 0] = ct2[2 * j:2 * j + 2, :]
        _q_proj_store(x_ref[...], mod_ref[0], wq_ref, bq_ref, q_ref, gz_ref, True)


def _kv_weights(kv_w, kv_b):
    wkv = kv_w[:, :2 * FA_WIDTH].astype(BF16)
    bkv = kv_b[None, :2 * FA_WIDTH]
    wkvt = kv_w[:, :2 * FA_WIDTH].T.astype(BF16)
    bkvt = kv_b[:2 * FA_WIDTH, None]
    wft = kv_w[:, 2 * FA_WIDTH:].T.astype(BF16)
    bft = kv_b[2 * FA_WIDTH:, None]
    return wkv, bkv, wkvt, bkvt, wft, bft


def _kv_prompt(x, kv_w, kv_b, batch, seq, mod, w_in, b_in):
    tm = ATT_TILE
    nt = seq // tm
    wkv, bkv, wkvt, bkvt, wft, bft = _kv_weights(kv_w, kv_b)
    weights = (wkv[:, FA_WIDTH:], bkv[:, FA_WIDTH:], wkvt, bkvt, wft, bft)
    q_weights = (w_in.astype(BF16), b_in[None])
    row = lambda b, i: (b * nt + i, 0)
    return pl.pallas_call(
        functools.partial(_kv_kernel, prompt=True),
        grid=(batch, nt),
        in_specs=([pl.BlockSpec((tm, D_MODEL), row)] + [_const_spec(w.shape) for w in weights]
                  + [pl.BlockSpec((1, 1, 3 * D_MODEL), lambda b, i: (b, 0, 0))]
                  + [_const_spec(w.shape) for w in q_weights]),
        out_specs=[pl.BlockSpec((1, FA_WIDTH, tm), lambda b, i: (b, 0, i)),
                   pl.BlockSpec((1, FA_WIDTH, tm), lambda b, i: (b, 0, i)),
                   pl.BlockSpec((1, FA_HEADS, tm), lambda b, i: (b, 0, i)),
                   pl.BlockSpec((1, HEAD_PAIRS, 1, LANES, tm), lambda b, i: (b, 0, i, 0, 0)),
                   pl.BlockSpec((1, HEAD_PAIRS, tm, LANES), lambda b, i: (b, 0, i, 0)),
                   pl.BlockSpec((1, HEAD_PAIRS, 1, 2, tm), lambda b, i: (b, 0, i, 0, 0)),
                   pl.BlockSpec((1, HEAD_PAIRS, tm, LANES), lambda b, i: (b, 0, i, 0)),
                   pl.BlockSpec((tm, FA_WIDTH), row)],
        out_shape=[jax.ShapeDtypeStruct((batch, FA_WIDTH, seq), F32),
                   jax.ShapeDtypeStruct((batch, FA_WIDTH, seq), F32),
                   jax.ShapeDtypeStruct((batch, FA_HEADS, seq), F32),
                   jax.ShapeDtypeStruct((batch, HEAD_PAIRS, nt, LANES, tm), BF16),
                   jax.ShapeDtypeStruct((batch, HEAD_PAIRS, seq, LANES), BF16),
                   jax.ShapeDtypeStruct((batch, HEAD_PAIRS, nt, 2, tm), F32),
                   jax.ShapeDtypeStruct((batch, HEAD_PAIRS, seq, LANES), BF16),
                   jax.ShapeDtypeStruct((batch * seq, FA_WIDTH), BF16)],
        scratch_shapes=[pltpu.VMEM((FA_HEADS, LANES), F32)],
        compiler_params=_params(2),
        name="kv_prompt",
    )(x, *weights, mod, *q_weights)


def _kv_sample(x, kv_w, kv_b):
    rows = x.shape[0]
    weights = _kv_weights(kv_w, kv_b)
    wf = jnp.pad(kv_w[:, 2 * FA_WIDTH:], ((0, 0), (0, LANES - FA_HEADS))).astype(BF16)
    bf = jnp.pad(kv_b[2 * FA_WIDTH:], (0, LANES - FA_HEADS))[None]
    weights = weights + (wf, bf)
    full = lambda i: (0, 0)
    full3 = lambda i: (0, 0, 0)
    return pl.pallas_call(
        functools.partial(_kv_kernel, prompt=False),
        grid=(1,),
        in_specs=[pl.BlockSpec((rows, D_MODEL), full)] + [_const_spec(w.shape) for w in weights],
        out_specs=[pl.BlockSpec((1, FA_WIDTH, rows), full3), pl.BlockSpec((1, FA_WIDTH, rows), full3),
                   pl.BlockSpec((1, FA_HEADS, rows), full3),
                   pl.BlockSpec((rows, FA_WIDTH), full), pl.BlockSpec((rows, FA_WIDTH), full),
                   pl.BlockSpec((rows, FA_HEADS), full)],
        out_shape=[jax.ShapeDtypeStruct((1, FA_WIDTH, rows), F32), jax.ShapeDtypeStruct((1, FA_WIDTH, rows), F32),
                   jax.ShapeDtypeStruct((1, FA_HEADS, rows), F32),
                   jax.ShapeDtypeStruct((rows, FA_WIDTH), F32), jax.ShapeDtypeStruct((rows, FA_WIDTH), F32),
                   jax.ShapeDtypeStruct((rows, FA_HEADS), F32)],
        compiler_params=_params(1),
        name="kv_sample",
    )(x, *weights)


def _q_proj_kernel(x_ref, mod_ref, w_ref, b_ref, q_ref, gz_ref, *, pair_major):
    _q_proj_store(x_ref[...], mod_ref[0], w_ref, b_ref, q_ref, gz_ref, pair_major)


def _q_proj_store(x, m, w_ref, b_ref, q_ref, gz_ref, pair_major):
    h = (x * (1.0 + m[:, D_MODEL:2 * D_MODEL]) + m[:, :D_MODEL]).astype(BF16)
    p = _dot(h, w_ref[...]) + b_ref[...]
    if pair_major:
        q = (p[:, :FA_WIDTH] * (LOG2E * FA_HEAD_DIM ** -0.5)).astype(BF16)
        for j in range(HEAD_PAIRS):
            q_ref[0, j] = q[:, j * LANES:(j + 1) * LANES]
    else:
        q_ref[...] = (p[:, :FA_WIDTH] * (FA_HEAD_DIM ** -0.5)).astype(BF16)
    gz_ref[...] = jax.nn.silu(p[:, FA_WIDTH:]).astype(gz_ref.dtype)


def _q_proj(x, mod, w_in, b_in, *, batch=None, seq=None):
    rows = x.shape[0]
    nb, mrows = mod.shape[0], mod.shape[1]
    pair_major = batch is not None
    tm = ROW_TILE if pair_major else rows
    tpb = rows // nb // tm
    if pair_major:
        q_spec = pl.BlockSpec((1, HEAD_PAIRS, tm, LANES), lambda i: (i // tpb, 0, i % tpb, 0))
        q_shape = jax.ShapeDtypeStruct((batch, HEAD_PAIRS, seq, LANES), BF16)
    else:
        q_spec = pl.BlockSpec((tm, FA_WIDTH), lambda i: (i, 0))
        q_shape = jax.ShapeDtypeStruct((rows, FA_WIDTH), BF16)
    return pl.pallas_call(
        functools.partial(_q_proj_kernel, pair_major=pair_major),
        grid=(rows // tm,),
        in_specs=[pl.BlockSpec((tm, D_MODEL), lambda i: (i, 0)),
                  pl.BlockSpec((1, mrows, 3 * D_MODEL), lambda i: (i // tpb, 0, 0)),
                  _const_spec((D_MODEL, 2 * FA_WIDTH)), _const_spec((1, 2 * FA_WIDTH))],
        out_specs=[q_spec, pl.BlockSpec((tm, FA_WIDTH), lambda i: (i, 0))],
        out_shape=[q_shape, jax.ShapeDtypeStruct((rows, FA_WIDTH), BF16)],
        compiler_params=_params(1),
        name="q_proj_prompt" if pair_major else "q_proj_sample",
    )(x, mod, w_in.astype(BF16), b_in[None])


def _out_proj_kernel(o_ref, gz_ref, x_ref, mod_ref, w_ref, b_ref, g_ref, bb_ref, *rest, next_q):
    if next_q:
        modn_ref, wq_ref, bq_ref, y_ref, q_ref, gzn_ref = rest
    else:
        (y_ref,) = rest
    t = (o_ref[...].astype(F32) * gz_ref[...].astype(F32)).astype(BF16)
    y = _dot(t, w_ref[...]) + b_ref[...]
    gate = mod_ref[0][:, 2 * D_MODEL:]
    r = DEEPNORM_ALPHA * x_ref[...] + (1.0 + gate) * y
    x_new = _layer_norm(r, g_ref[...], bb_ref[...])
    y_ref[...] = x_new
    if next_q:
        _q_proj_store(x_new, modn_ref[0], wq_ref, bq_ref, q_ref, gzn_ref, True)


def _out_proj(o, gz, x, mod, w_out, b_out, ln_g, ln_b, *, tm, next_q=None):
    rows = x.shape[0]
    nb, mrows = mod.shape[0], mod.shape[1]
    tpb = rows // nb // tm
    row = lambda i: (i, 0)
    mod_spec = pl.BlockSpec((1, mrows, 3 * D_MODEL), lambda i: (i // tpb, 0, 0))
    in_specs = [pl.BlockSpec((tm, FA_WIDTH), row), pl.BlockSpec((tm, FA_WIDTH), row),
                pl.BlockSpec((tm, D_MODEL), row), mod_spec,
                _const_spec((FA_WIDTH, D_MODEL)), _const_spec((1, D_MODEL)),
                _const_spec((1, D_MODEL)), _const_spec((1, D_MODEL))]
    args = [o, gz, x, mod, w_out.astype(BF16), b_out[None], ln_g[None], ln_b[None]]
    out_specs = [pl.BlockSpec((tm, D_MODEL), row)]
    out_shape = [jax.ShapeDtypeStruct((rows, D_MODEL), F32)]
    if next_q is not None:
        mod_n, w_in, b_in, batch, seq = next_q
        in_specs += [mod_spec, _const_spec((D_MODEL, 2 * FA_WIDTH)), _const_spec((1, 2 * FA_WIDTH))]
        args += [mod_n, w_in.astype(BF16), b_in[None]]
        out_specs += [pl.BlockSpec((1, HEAD_PAIRS, tm, LANES), lambda i: (i // tpb, 0, i % tpb, 0)),
                      pl.BlockSpec((tm, FA_WIDTH), row)]
        out_shape += [jax.ShapeDtypeStruct((batch, HEAD_PAIRS, seq, LANES), BF16),
                      jax.ShapeDtypeStruct((rows, FA_WIDTH), BF16)]
    res = pl.pallas_call(
        functools.partial(_out_proj_kernel, next_q=next_q is not None),
        grid=(rows // tm,),
        in_specs=in_specs,
        out_specs=out_specs,
        out_shape=out_shape,
        compiler_params=_params(1),
        name="out_proj",
    )(*args)
    return res if next_q is not None else res[0]


def _flash_q_tile(i, q2, k_ref, v_ref, c_ref, scr, tile):
    s_scr = (scr[0:2], scr[2:4])
    m_scr = (scr[4:6], scr[6:8])
    p_scr, alpha_scr, acc_scr = scr[8:10], scr[10:12], scr[12:14]
    lane = lax.broadcasted_iota(jnp.int32, (1, LANES), 1)
    first = lane < FA_HEAD_DIM
    sub8 = lax.broadcasted_iota(jnp.int32, (8, tile), 0)

    half = tile // 2

    def logits(t, q_heads, buf, diagonal=False):
        c2 = c_ref[0, 0, t]
        hi = c2.astype(BF16).astype(F32)
        mid = (c2 - hi).astype(BF16).astype(F32)
        lo = c2 - hi - mid
        pieces = (hi[0:1], mid[0:1], lo[0:1], hi[1:2], mid[1:2], lo[1:2])
        c_rows = jnp.zeros((8, tile), F32)
        for r, piece in enumerate(pieces):
            c_rows = jnp.where(sub8 == r, piece, c_rows)
        c_block = jnp.concatenate([c_rows, jnp.zeros((LANES - 8, tile), F32)], axis=0).astype(BF16)
        k_aug = jnp.concatenate([k_ref[0, 0, t], c_block], axis=0)
        for a in range(2):
            if diagonal:
                s_scr[buf][a][:half, :half] = _dot(q_heads[a][:half], k_aug[:, :half])
                s_scr[buf][a][half:, :] = _dot(q_heads[a][half:], k_aug)
            else:
                s_scr[buf][a][...] = _dot(q_heads[a], k_aug)

    def softmax_blocks(a, buf, masked):
        for r0 in range(0, tile, FLASH_ROWS):
            rows = slice(r0, r0 + FLASH_ROWS)
            ncols = min(tile, -(-(r0 + FLASH_ROWS) // LANES) * LANES) if masked else tile
            s = s_scr[buf][a][rows, :ncols]
            if masked:
                r_i = r0 + lax.broadcasted_iota(jnp.int32, (FLASH_ROWS, ncols), 0)
                c_i = lax.broadcasted_iota(jnp.int32, (FLASH_ROWS, ncols), 1)
                s = jnp.where(c_i <= r_i, s, NEG_BIG)
            m_old = m_scr[buf][a][rows, :]
            m_new = jnp.maximum(m_old, jnp.max(s, axis=-1, keepdims=True))
            m_scr[1 - buf][a][rows, :] = m_new
            alpha_scr[a][rows, :] = jnp.exp2(m_old - m_new)
            m_wide = jnp.concatenate([m_new] * (ncols // LANES), axis=1)
            p_scr[a][rows, :ncols] = jnp.exp2((s - m_wide).astype(BF16))
            fill_to = half if r0 < half else tile
            if masked and ncols < fill_to:
                p_scr[a][rows, ncols:fill_to] = jnp.zeros((FLASH_ROWS, fill_to - ncols), BF16)

    def consume(t, buf, masked):
        v2 = v_ref[0, 0, pl.ds(pl.multiple_of(t * tile, tile), tile), :]
        one = jnp.ones_like(v2)
        v_heads = (jnp.where(first, v2, one), jnp.where(first, one, v2))
        for a in range(2):
            softmax_blocks(a, buf, masked)
            if masked:
                pv = jnp.concatenate([_dot(p_scr[a][:half, :half], v_heads[a][:half]),
                                      _dot(p_scr[a][half:, :], v_heads[a])], axis=0)
            else:
                pv = _dot(p_scr[a][...], v_heads[a])
            acc_scr[a][...] = alpha_scr[a][...] * acc_scr[a][...] + pv

    col_q = lax.broadcasted_iota(jnp.int32, (tile, LANES), 1)
    zero = jnp.zeros_like(q2)
    q_heads = (jnp.concatenate([jnp.where(first, q2, zero),
                                jnp.where(col_q < 3, -1.0, 0.0).astype(BF16)], axis=1),
               jnp.concatenate([jnp.where(first, zero, q2),
                                jnp.where((col_q >= 3) & (col_q < 6), -1.0, 0.0).astype(BF16)], axis=1))
    for a in range(2):
        m_scr[0][a][...] = jnp.full((tile, LANES), NEG_BIG, F32)
        acc_scr[a][...] = jnp.zeros((tile, LANES), F32)

    logits(0, q_heads, 0)

    def tile_pair(u, _):
        logits(2 * u + 1, q_heads, 1)
        consume(2 * u, 0, False)
        logits(2 * u + 2, q_heads, 0)
        consume(2 * u + 1, 1, False)
        return 0

    lax.fori_loop(0, i // 2, tile_pair, 0)

    @pl.when(i % 2 == 1)
    def _():
        logits(i, q_heads, 1, diagonal=True)
        consume(i - 1, 0, False)
        consume(i, 1, True)

    @pl.when(i % 2 == 0)
    def _():
        consume(i, 0, True)

    acc_a, acc_b = acc_scr[0][...], acc_scr[1][...]
    o_a = acc_a / pltpu.roll(acc_a, FA_HEAD_DIM, axis=1)
    o_b = acc_b / pltpu.roll(acc_b, FA_HEAD_DIM, axis=1)
    return jnp.where(first, o_a, o_b)


def _decode_group(g, n_groups, q_ref, kn_ref, vn_ref, lfn_ref, k_refs, v_refs, lf_refs, o_ref,
                  m_scr, l_scr, acc_scr, off_scr):
    pg = len(k_refs)

    @pl.when(g == 0)
    def _():
        m_scr[...] = jnp.full(m_scr.shape, NEG_BIG, F32)
        l_scr[...] = jnp.zeros(l_scr.shape, F32)
        acc_scr[...] = jnp.zeros(acc_scr.shape, F32)
        off_scr[...] = jnp.zeros(off_scr.shape, F32)

    head = lax.broadcasted_iota(jnp.int32, (FA_HEADS, FA_WIDTH), 0)
    col = lax.broadcasted_iota(jnp.int32, (FA_HEADS, FA_WIDTH), 1)
    own = (col >= head * FA_HEAD_DIM) & (col < (head + 1) * FA_HEAD_DIM)
    q_rows = jnp.where(own, jnp.broadcast_to(q_ref[0].astype(F32), (FA_HEADS, FA_WIDTH)), 0.0)
    q_bf = q_rows.astype(BF16)

    r_i = lax.broadcasted_iota(jnp.int32, (PAGE_SIZE, PAGE_SIZE), 0)
    c_i = lax.broadcasted_iota(jnp.int32, (PAGE_SIZE, PAGE_SIZE), 1)
    upper = jnp.where(r_i <= c_i, 1.0, 0.0).astype(BF16)
    lf_all = jnp.concatenate([lf_refs[p][0] for p in range(pg)], axis=0)
    cum_all = _split3_dot(lf_all, upper)
    off = off_scr[:, 0:1]
    s_parts = []
    for p in range(pg):
        cum_p = cum_all[p * FA_HEADS:(p + 1) * FA_HEADS]
        s_parts.append(_dot(q_bf, k_refs[p][0].astype(BF16)) - (off + cum_p))
        off = off + cum_p[:, PAGE_SIZE - 1:PAGE_SIZE]
    off_scr[...] = jnp.broadcast_to(off, off_scr.shape)
    s = jnp.concatenate(s_parts, axis=1)
    m_old = m_scr[:, 0:1]
    m_new = jnp.maximum(m_old, jnp.max(s, axis=1, keepdims=True))
    alpha = jnp.exp(m_old - m_new)
    prob = jnp.exp(s - m_new)
    l_new = alpha * l_scr[:, 0:1] + jnp.sum(prob, axis=1, keepdims=True)
    pb = prob.astype(BF16)
    acc = alpha * acc_scr[...]
    for p in range(pg):
        acc = acc + _dot_nt(pb[:, p * PAGE_SIZE:(p + 1) * PAGE_SIZE], v_refs[p][0].astype(BF16))
    m_scr[...] = jnp.broadcast_to(m_new, m_scr.shape)
    l_scr[...] = jnp.broadcast_to(l_new, l_scr.shape)
    acc_scr[...] = acc

    @pl.when(g == n_groups - 1)
    def _():
        c_query = off + lfn_ref[0]
        s_new = jnp.sum(q_rows * kn_ref[0], axis=1, keepdims=True)
        m_past = m_new + c_query
        m_fin = jnp.maximum(m_past, s_new)
        w_past = jnp.exp(m_past - m_fin)
        w_new = jnp.exp(s_new - m_fin)
        o_full = (acc * w_past + w_new * vn_ref[0]) / (l_new * w_past + w_new)
        o_ref[0] = jnp.sum(jnp.where(own, o_full, 0.0), axis=0, keepdims=True)


def _attention_kernel(pt_ref, q_ref, k_ref, v_ref, c_ref, qs_ref, kn_ref, vn_ref, lfn_ref, *refs,
                      tile, pg, n_groups):
    del pt_ref
    k_refs, v_refs, lf_refs = refs[:pg], refs[pg:2 * pg], refs[2 * pg:3 * pg]
    o_ref, os_ref = refs[3 * pg:3 * pg + 2]
    scr = refs[3 * pg + 2:]
    i2 = pl.program_id(2)

    def q_tile(h, _):
        rows = pl.ds(pl.multiple_of(h * tile, tile), tile)
        o = _flash_q_tile(Q_TILES_PER_STEP * i2 + h, q_ref[0, 0, rows, :], k_ref, v_ref, c_ref, scr[:14], tile)
        o_ref[0, rows, :] = o.astype(o_ref.dtype)
        return 0

    lax.fori_loop(0, Q_TILES_PER_STEP, q_tile, 0)
    step = (pl.program_id(0) * pl.num_programs(1) + pl.program_id(1)) * pl.num_programs(2) + i2
    _decode_group(step % n_groups, n_groups, qs_ref, kn_ref, vn_ref, lfn_ref, k_refs, v_refs, lf_refs, os_ref,
                  *scr[14:])


def _attention_layer(q_pm, k_pm, v_pm, c_t, batch, seq,
                     q_s, k_new, v_new, lf_new, cache_kt, cache_vt, cache_lft, page_table):
    tile = ATT_TILE
    nt = seq // tile
    qt = Q_TILES_PER_STEP
    nq = nt // qt
    n, n_pages = page_table.shape
    steps = batch * HEAD_PAIRS * nq
    n_groups = steps // n
    pg = n_pages // n_groups
    assert nq * qt == nt and n_groups * n == steps and pg * n_groups == n_pages

    def sample_of(b, j, i):
        return ((b * HEAD_PAIRS + j) * nq + i) // n_groups

    def page_map(p):
        def index(b, j, i, pt):
            step = (b * HEAD_PAIRS + j) * nq + i
            return (pt[step // n_groups, (step % n_groups) * pg + p], 0, 0)
        return index

    row = lambda b, j, i, pt: (sample_of(b, j, i), 0, 0)
    in_specs = [pl.BlockSpec((1, 1, qt * tile, LANES), lambda b, j, i, pt: (b, j, i, 0)),
                pl.BlockSpec((1, 1, nt, LANES, tile), lambda b, j, i, pt: (b, j, 0, 0, 0)),
                pl.BlockSpec((1, 1, seq, LANES), lambda b, j, i, pt: (b, j, 0, 0)),
                pl.BlockSpec((1, 1, nt, 2, tile), lambda b, j, i, pt: (b, j, 0, 0, 0)),
                pl.BlockSpec((1, 1, FA_WIDTH), row), pl.BlockSpec((1, 1, FA_WIDTH), row),
                pl.BlockSpec((1, 1, FA_WIDTH), row), pl.BlockSpec((1, FA_HEADS, 1), row)]
    in_specs += [pl.BlockSpec((1, FA_WIDTH, PAGE_SIZE), page_map(p)) for p in range(pg)]
    in_specs += [pl.BlockSpec((1, FA_WIDTH, PAGE_SIZE), page_map(p)) for p in range(pg)]
    in_specs += [pl.BlockSpec((1, FA_HEADS, PAGE_SIZE), page_map(p)) for p in range(pg)]
    o_p, o_s = pl.pallas_call(
        functools.partial(_attention_kernel, tile=tile, pg=pg, n_groups=n_groups),
        grid_spec=pltpu.PrefetchScalarGridSpec(
            num_scalar_prefetch=1, grid=(batch, HEAD_PAIRS, nq), in_specs=in_specs,
            out_specs=[pl.BlockSpec((1, qt * tile, LANES), lambda b, j, i, pt: (b, i, j)),
                       pl.BlockSpec((1, 1, FA_WIDTH), row)],
            scratch_shapes=([pltpu.VMEM((tile, tile), F32)] * 4
                            + [pltpu.VMEM((tile, LANES), F32)] * 4
                            + [pltpu.VMEM((tile, tile), BF16)] * 2
                            + [pltpu.VMEM((tile, LANES), F32)] * 2
                            + [pltpu.VMEM((tile, LANES), F32)] * 2
                            + [pltpu.VMEM((FA_HEADS, LANES), F32)] * 2
                            + [pltpu.VMEM((FA_HEADS, FA_WIDTH), F32)]
                            + [pltpu.VMEM((FA_HEADS, LANES), F32)])),
        out_shape=[jax.ShapeDtypeStruct((batch, seq, FA_WIDTH), BF16),
                   jax.ShapeDtypeStruct((n, 1, FA_WIDTH), F32)],
        compiler_params=_params(3),
        name="attention_layer",
    )(page_table, q_pm, k_pm, v_pm, c_t,
      q_s.reshape(n, 1, FA_WIDTH), k_new.reshape(n, 1, FA_WIDTH), v_new.reshape(n, 1, FA_WIDTH),
      lf_new.reshape(n, FA_HEADS, 1), *([cache_kt] * pg), *([cache_vt] * pg), *([cache_lft] * pg))
    return o_p, o_s.reshape(n, FA_WIDTH)


def kernel(x_prompt, x_sample, c_prompt, c_sample, cache_k, cache_v, cache_logf, page_table, w_ada, b_ada, ln_g, ln_b, cm_w_in, cm_b_in, cm_ln_v_g, cm_ln_v_b, cm_w_s, cm_b_s, cm_w_out, cm_b_out, kv_w, kv_b, fa_w_in, fa_b_in, fa_w_out, fa_b_out):
    batch, seq, _ = x_prompt.shape
    n_dec = x_sample.shape[0]

    n_c = batch + n_dec
    c_all = jnp.pad(jnp.concatenate([c_prompt, c_sample], axis=0), ((0, (-n_c) % 8), (0, 0)))
    mod = _ada_modulation(c_all, w_ada, b_ada)
    mod_p = mod[:, :batch].reshape(DEPTH, batch, 1, 3 * D_MODEL)
    mod_s = mod[:, batch:n_c].reshape(DEPTH, 1, n_dec, 3 * D_MODEL)

    xp = x_prompt.reshape(batch * seq, D_MODEL)
    xs = x_sample.reshape(n_dec, D_MODEL)
    chunk_rows = []
    for layer in range(N_A_LAYERS):
        args = (cm_w_in[layer], cm_b_in[layer], cm_ln_v_g[layer], cm_ln_v_b[layer], cm_w_s[layer], cm_b_s[layer],
                cm_w_out[layer], cm_b_out[layer], ln_g[layer], ln_b[layer])
        (xp,) = _chunk_mlp_layer(xp, mod_p[layer], *args, single_row_chunks=False)
        xs, vn_s = _chunk_mlp_layer(xs, mod_s[layer], *args, single_row_chunks=True)
        chunk_rows.append(vn_s)

    kt_p, vt_p, lft_p, kb_p, vb_p, ct_p, q_pm, gz_p = _kv_prompt(
        xp, kv_w, kv_b, batch, seq, mod_p[N_A_LAYERS], fa_w_in[0], fa_b_in[0])
    kt_s, vt_s, lft_s, k_s, v_s, lf_s = _kv_sample(xs, kv_w, kv_b)

    n_phys = cache_k.shape[0]
    cache_kt = jnp.transpose(cache_k, (0, 2, 3, 1)).reshape(n_phys, FA_WIDTH, PAGE_SIZE)
    cache_vt = jnp.transpose(cache_v, (0, 2, 3, 1)).reshape(n_phys, FA_WIDTH, PAGE_SIZE)
    cache_lft = jnp.swapaxes(cache_logf, 1, 2)

    for j in range(DEPTH - N_A_LAYERS):
        layer = N_A_LAYERS + j
        q_s, gz_s = _q_proj(xs, mod_s[layer], fa_w_in[j], fa_b_in[j])
        o_p, o_s = _attention_layer(q_pm, kb_p, vb_p, ct_p, batch, seq,
                                    q_s, k_s, v_s, lf_s, cache_kt, cache_vt, cache_lft, page_table)
        next_q = ((mod_p[layer + 1], fa_w_in[j + 1], fa_b_in[j + 1], batch, seq) if layer + 1 < DEPTH else None)
        res = _out_proj(o_p.reshape(batch * seq, FA_WIDTH), gz_p, xp, mod_p[layer], fa_w_out[j], fa_b_out[j],
                        ln_g[layer], ln_b[layer], tm=ROW_TILE, next_q=next_q)
        xp, q_pm, gz_p = res if next_q is not None else (res, None, None)
        xs = _out_proj(o_s, gz_s, xs, mod_s[layer], fa_w_out[j], fa_b_out[j], ln_g[layer], ln_b[layer], tm=n_dec)

    y_prompt = xp.reshape(batch, seq, D_MODEL)
    y_sample = xs.reshape(n_dec, 1, D_MODEL)
    k_prompt = jnp.transpose(kt_p.reshape(batch, FA_HEADS, FA_HEAD_DIM, seq), (0, 3, 1, 2))
    v_prompt = jnp.transpose(vt_p.reshape(batch, FA_HEADS, FA_HEAD_DIM, seq), (0, 3, 1, 2))
    logf_prompt = jnp.transpose(lft_p, (0, 2, 1))
    k_sample = jnp.transpose(kt_s.reshape(1, FA_HEADS, FA_HEAD_DIM, n_dec), (3, 0, 1, 2))
    v_sample = jnp.transpose(vt_s.reshape(1, FA_HEADS, FA_HEAD_DIM, n_dec), (3, 0, 1, 2))
    logf_sample = jnp.transpose(lft_s, (2, 0, 1))
    chunk_v_sample = jnp.stack(chunk_rows).reshape(N_A_LAYERS, n_dec, 1, CM_WIDTH)
    return (y_prompt, y_sample, k_prompt, v_prompt, logf_prompt, k_sample, v_sample, logf_sample, chunk_v_sample)
```

```python
import functools

import jax
import jax.numpy as jnp
from jax import lax
from jax.experimental import pallas as pl
from jax.experimental.pallas import tpu as pltpu

F32 = jnp.float32
BF16 = jnp.bfloat16

D_MODEL = 1024
DEPTH = 4
N_A_LAYERS = 2
CHUNK = 128
CM_WIDTH = 2048
CM_GROUPS = 8
CM_GROUP_DIM = CM_WIDTH // CM_GROUPS
FA_HEADS = 16
FA_HEAD_DIM = 64
FA_WIDTH = FA_HEADS * FA_HEAD_DIM
HEAD_PAIRS = FA_HEADS // 2
PAGE_SIZE = 128
DEEPNORM_ALPHA = (2.0 * DEPTH) ** 0.25
LN_EPS = 1e-5
LANES = 128
NEG_BIG = -1e30
VMEM_LIMIT = 56 * 1024 * 1024

ROW_TILE = 512
ATT_TILE = 512
FLASH_ROWS = 32
Q_TILES_PER_STEP = 2
LOG2E = 1.4426950408889634


def _params(n_axes, vmem=VMEM_LIMIT):
    return pltpu.CompilerParams(dimension_semantics=("arbitrary",) * n_axes, vmem_limit_bytes=vmem)


def _dot(a, b):
    return jnp.dot(a, b, preferred_element_type=F32)


def _dot_nt(a, b):
    return lax.dot_general(a, b, (((1,), (1,)), ((), ())), preferred_element_type=F32)


def _layer_norm(x, g, b):
    mu = jnp.mean(x, axis=-1, keepdims=True)
    d = x - mu
    var = jnp.mean(d * d, axis=-1, keepdims=True)
    return d * lax.rsqrt(var + LN_EPS) * g + b


def _log_sigmoid(x):
    return -(jnp.maximum(-x, 0.0) + jnp.log1p(jnp.exp(-jnp.abs(x))))


def _split3_dot(x, w):
    hi = x.astype(BF16)
    r1 = x - hi.astype(F32)
    mid = r1.astype(BF16)
    lo = (r1 - mid.astype(F32)).astype(BF16)
    return _dot(hi, w) + _dot(mid, w) + _dot(lo, w)


def _const_spec(shape):
    nd = len(shape)
    return pl.BlockSpec(shape, lambda *_: (0,) * nd, pipeline_mode=pl.Buffered(1))


def _ada_kernel(c_ref, w_ref, b_ref, o_ref):
    a = jax.nn.silu(c_ref[...]).astype(BF16)
    o_ref[0] = _dot(a, w_ref[0].astype(BF16)) + b_ref[0]


def _ada_modulation(c_all, w_ada, b_ada):
    n = c_all.shape[0]
    nt = 3 * D_MODEL // D_MODEL
    return pl.pallas_call(
        _ada_kernel,
        grid=(DEPTH, nt),
        in_specs=[pl.BlockSpec((n, D_MODEL), lambda l, j: (0, 0)),
                  pl.BlockSpec((1, D_MODEL, D_MODEL), lambda l, j: (l, 0, j)),
                  pl.BlockSpec((1, 1, D_MODEL), lambda l, j: (l, 0, j))],
        out_specs=pl.BlockSpec((1, n, D_MODEL), lambda l, j: (l, 0, j)),
        out_shape=jax.ShapeDtypeStruct((DEPTH, n, 3 * D_MODEL), F32),
        compiler_params=_params(2),
        name="ada_modulation",
    )(c_all, w_ada, b_ada.reshape(DEPTH, 1, 3 * D_MODEL))


def _chunk_mlp_kernel(x_ref, mod_ref, win_ref, bin_ref, lvg_ref, lvb_ref, ws_ref, bs_ref,
                      wout_ref, bout_ref, lng_ref, lnb_ref, *rest, single_row_chunks):
    if single_row_chunks:
        o_ref, vn_ref, t_scr = rest
    else:
        o_ref, t_scr = rest
    rows = x_ref.shape[0]
    x = x_ref[...]
    m = mod_ref[0]
    shift, scale, gate = m[:, :D_MODEL], m[:, D_MODEL:2 * D_MODEL], m[:, 2 * D_MODEL:]
    h = (x * (1.0 + scale) + shift).astype(BF16)

    v = jax.nn.gelu(_dot(h, win_ref[:, CM_WIDTH:2 * CM_WIDTH]) + bin_ref[:, CM_WIDTH:2 * CM_WIDTH])
    vn = _layer_norm(v, lvg_ref[...], lvb_ref[...])
    if single_row_chunks:
        vn_ref[...] = vn
    else:
        r_i = lax.broadcasted_iota(jnp.int32, (CHUNK, CHUNK), 0)
        c_i = lax.broadcasted_iota(jnp.int32, (CHUNK, CHUNK), 1)
        causal = c_i <= r_i

    for g in range(CM_GROUPS):
        lo, hi = g * CM_GROUP_DIM, (g + 1) * CM_GROUP_DIM
        vn_g = vn[:, lo:hi]
        if single_row_chunks:
            sg = vn_g * ws_ref[:, lo:hi] + bs_ref[:, lo:hi]
        else:
            w_g = jnp.where(causal, ws_ref[g], 0.0).astype(BF16)
            b_g = bs_ref[:, g:g + 1]
            vb = vn_g.astype(BF16)
            sg = jnp.concatenate(
                [_dot(w_g, vb[c * CHUNK:(c + 1) * CHUNK]) + b_g for c in range(rows // CHUNK)], axis=0)
        u_g = jax.nn.gelu(_dot(h, win_ref[:, lo:hi]) + bin_ref[:, lo:hi])
        z_g = _dot(h, win_ref[:, 2 * CM_WIDTH + lo:2 * CM_WIDTH + hi]) + bin_ref[:, 2 * CM_WIDTH + lo:2 * CM_WIDTH + hi]
        t_scr[:, lo:hi] = (u_g * sg * jax.nn.silu(z_g)).astype(BF16)

    y = _dot(t_scr[...], wout_ref[...]) + bout_ref[...]
    r = DEEPNORM_ALPHA * x + (1.0 + gate) * y
    o_ref[...] = _layer_norm(r, lng_ref[...], lnb_ref[...])


def _chunk_mlp_layer(x, mod, w_in, b_in, lvg, lvb, w_s, b_s, w_out, b_out, ln_g, ln_b, *, single_row_chunks):
    rows = x.shape[0]
    nb, mrows = mod.shape[0], mod.shape[1]
    tm = rows if single_row_chunks else ROW_TILE
    tiles_per_batch = rows // nb // tm
    if single_row_chunks:
        ws_arg = jnp.repeat(w_s[:, 0, 0], CM_GROUP_DIM)[None]
        bs_arg = jnp.repeat(b_s[:, 0], CM_GROUP_DIM)[None]
    else:
        ws_arg, bs_arg = w_s, b_s.T
    out_shape = [jax.ShapeDtypeStruct((rows, D_MODEL), F32)]
    out_specs = [pl.BlockSpec((tm, D_MODEL), lambda i: (i, 0))]
    if single_row_chunks:
        out_shape.append(jax.ShapeDtypeStruct((rows, CM_WIDTH), F32))
        out_specs.append(pl.BlockSpec((tm, CM_WIDTH), lambda i: (i, 0)))
    res = pl.pallas_call(
        functools.partial(_chunk_mlp_kernel, single_row_chunks=single_row_chunks),
        grid=(rows // tm,),
        in_specs=[pl.BlockSpec((tm, D_MODEL), lambda i: (i, 0)),
                  pl.BlockSpec((1, mrows, 3 * D_MODEL), lambda i: (i // tiles_per_batch, 0, 0)),
                  _const_spec((D_MODEL, 3 * CM_WIDTH)), _const_spec((1, 3 * CM_WIDTH)),
                  _const_spec((1, CM_WIDTH)), _const_spec((1, CM_WIDTH)),
                  _const_spec(ws_arg.shape), _const_spec(bs_arg.shape),
                  _const_spec((CM_WIDTH, D_MODEL)), _const_spec((1, D_MODEL)),
                  _const_spec((1, D_MODEL)), _const_spec((1, D_MODEL))],
        out_specs=out_specs,
        out_shape=out_shape,
        scratch_shapes=[pltpu.VMEM((tm, CM_WIDTH), BF16)],
        compiler_params=_params(1),
        name="chunk_mlp_sample" if single_row_chunks else "chunk_mlp_prompt",
    )(x, mod, w_in.astype(BF16), b_in[None], lvg[None], lvb[None], ws_arg, bs_arg,
      w_out.astype(BF16), b_out[None], ln_g[None], ln_b[None])
    return res


def _kv_kernel(x_ref, wkvt_ref, bkvt_ref, wft_ref, bft_ref, *rest, prompt):
    if prompt:
        mod_ref, wq_ref, bq_ref, kt_ref, vt_ref, lft_ref, kb_ref, vb_ref, ct_ref, q_ref, gz_ref, carry_scr = rest
    else:
        wf_ref, bf_ref, kt_ref, vt_ref, lft_ref, k_ref, v_ref, lf_ref = rest
    rows = x_ref.shape[0]
    xb = x_ref[...].astype(BF16)
    kvt = _dot_nt(wkvt_ref[...], xb) + bkvt_ref[...]
    kt_ref[0] = kvt[:FA_WIDTH]
    vt_ref[0] = kvt[FA_WIDTH:]
    lft = _log_sigmoid(_dot_nt(wft_ref[...], xb) + bft_ref[...])
    lft_ref[0] = lft
    if not prompt:
        k_ref[...] = kvt[:FA_WIDTH].T
        v_ref[...] = kvt[FA_WIDTH:].T
        lf_ref[...] = _log_sigmoid(_dot(xb, wf_ref[...]) + bf_ref[...])[:, :FA_HEADS]
    else:
        ktb = kvt[:FA_WIDTH].astype(BF16)
        vb = kvt[FA_WIDTH:].T.astype(BF16)
        for j in range(HEAD_PAIRS):
            kb_ref[0, j, 0] = ktb[j * LANES:(j + 1) * LANES, :]
            vb_ref[0, j] = vb[:, j * LANES:(j + 1) * LANES]
        r_i = lax.broadcasted_iota(jnp.int32, (rows, rows), 0)
        c_i = lax.broadcasted_iota(jnp.int32, (rows, rows), 1)
        upper = jnp.where(r_i <= c_i, 1.0, 0.0).astype(BF16)

        @pl.when(pl.program_id(1) == 0)
        def _():
            carry_scr[...] = jnp.zeros_like(carry_scr)

        ct = _split3_dot(lft, upper) + carry_scr[:, 0:1]
        carry_scr[...] = jnp.broadcast_to(ct[:, rows - 1:rows], carry_scr.shape)
        ct2 = ct * LOG2E
        for j in range(HEAD_PAIRS):
            ct_ref[0, j, 0] = ct2[2 * j:2 * j + 2, :]
        _q_proj_store(x_ref[...], mod_ref[0], wq_ref, bq_ref, q_ref, gz_ref, True)


def _kv_weights(kv_w, kv_b):
    wkvt = kv_w[:, :2 * FA_WIDTH].T.astype(BF16)
    bkvt = kv_b[:2 * FA_WIDTH, None]
    wft = kv_w[:, 2 * FA_WIDTH:].T.astype(BF16)
    bft = kv_b[2 * FA_WIDTH:, None]
    return wkvt, bkvt, wft, bft


def _kv_prompt(x, kv_w, kv_b, batch, seq, mod, w_in, b_in):
    tm = ATT_TILE
    nt = seq // tm
    weights = _kv_weights(kv_w, kv_b)
    q_weights = (w_in.astype(BF16), b_in[None])
    row = lambda b, i: (b * nt + i, 0)
    return pl.pallas_call(
        functools.partial(_kv_kernel, prompt=True),
        grid=(batch, nt),
        in_specs=([pl.BlockSpec((tm, D_MODEL), row)] + [_const_spec(w.shape) for w in weights]
                  + [pl.BlockSpec((1, 1, 3 * D_MODEL), lambda b, i: (b, 0, 0))]
                  + [_const_spec(w.shape) for w in q_weights]),
        out_specs=[pl.BlockSpec((1, FA_WIDTH, tm), lambda b, i: (b, 0, i)),
                   pl.BlockSpec((1, FA_WIDTH, tm), lambda b, i: (b, 0, i)),
                   pl.BlockSpec((1, FA_HEADS, tm), lambda b, i: (b, 0, i)),
                   pl.BlockSpec((1, HEAD_PAIRS, 1, LANES, tm), lambda b, i: (b, 0, i, 0, 0)),
                   pl.BlockSpec((1, HEAD_PAIRS, tm, LANES), lambda b, i: (b, 0, i, 0)),
                   pl.BlockSpec((1, HEAD_PAIRS, 1, 2, tm), lambda b, i: (b, 0, i, 0, 0)),
                   pl.BlockSpec((1, HEAD_PAIRS, tm, LANES), lambda b, i: (b, 0, i, 0)),
                   pl.BlockSpec((tm, FA_WIDTH), row)],
        out_shape=[jax.ShapeDtypeStruct((batch, FA_WIDTH, seq), F32),
                   jax.ShapeDtypeStruct((batch, FA_WIDTH, seq), F32),
                   jax.ShapeDtypeStruct((batch, FA_HEADS, seq), F32),
                   jax.ShapeDtypeStruct((batch, HEAD_PAIRS, nt, LANES, tm), BF16),
                   jax.ShapeDtypeStruct((batch, HEAD_PAIRS, seq, LANES), BF16),
                   jax.ShapeDtypeStruct((batch, HEAD_PAIRS, nt, 2, tm), F32),
                   jax.ShapeDtypeStruct((batch, HEAD_PAIRS, seq, LANES), BF16),
                   jax.ShapeDtypeStruct((batch * seq, FA_WIDTH), BF16)],
        scratch_shapes=[pltpu.VMEM((FA_HEADS, LANES), F32)],
        compiler_params=_params(2),
        name="kv_prompt",
    )(x, *weights, mod, *q_weights)


def _kv_sample(x, kv_w, kv_b):
    rows = x.shape[0]
    weights = _kv_weights(kv_w, kv_b)
    wf = jnp.pad(kv_w[:, 2 * FA_WIDTH:], ((0, 0), (0, LANES - FA_HEADS))).astype(BF16)
    bf = jnp.pad(kv_b[2 * FA_WIDTH:], (0, LANES - FA_HEADS))[None]
    weights = weights + (wf, bf)
    full = lambda i: (0, 0)
    full3 = lambda i: (0, 0, 0)
    return pl.pallas_call(
        functools.partial(_kv_kernel, prompt=False),
        grid=(1,),
        in_specs=[pl.BlockSpec((rows, D_MODEL), full)] + [_const_spec(w.shape) for w in weights],
        out_specs=[pl.BlockSpec((1, FA_WIDTH, rows), full3), pl.BlockSpec((1, FA_WIDTH, rows), full3),
                   pl.BlockSpec((1, FA_HEADS, rows), full3),
                   pl.BlockSpec((rows, FA_WIDTH), full), pl.BlockSpec((rows, FA_WIDTH), full),
                   pl.BlockSpec((rows, FA_HEADS), full)],
        out_shape=[jax.ShapeDtypeStruct((1, FA_WIDTH, rows), F32), jax.ShapeDtypeStruct((1, FA_WIDTH, rows), F32),
                   jax.ShapeDtypeStruct((1, FA_HEADS, rows), F32),
                   jax.ShapeDtypeStruct((rows, FA_WIDTH), F32), jax.ShapeDtypeStruct((rows, FA_WIDTH), F32),
                   jax.ShapeDtypeStruct((rows, FA_HEADS), F32)],
        compiler_params=_params(1),
        name="kv_sample",
    )(x, *weights)


def _q_proj_kernel(x_ref, mod_ref, w_ref, b_ref, q_ref, gz_ref, *, pair_major):
    _q_proj_store(x_ref[...], mod_ref[0], w_ref, b_ref, q_ref, gz_ref, pair_major)


def _q_proj_store(x, m, w_ref, b_ref, q_ref, gz_ref, pair_major):
    h = (x * (1.0 + m[:, D_MODEL:2 * D_MODEL]) + m[:, :D_MODEL]).astype(BF16)
    p = _dot(h, w_ref[...]) + b_ref[...]
    if pair_major:
        q = (p[:, :FA_WIDTH] * (LOG2E * FA_HEAD_DIM ** -0.5)).astype(BF16)
        for j in range(HEAD_PAIRS):
            q_ref[0, j] = q[:, j * LANES:(j + 1) * LANES]
    else:
        q_ref[...] = (p[:, :FA_WIDTH] * (FA_HEAD_DIM ** -0.5)).astype(BF16)
    gz_ref[...] = jax.nn.silu(p[:, FA_WIDTH:]).astype(gz_ref.dtype)


def _q_proj(x, mod, w_in, b_in, *, batch=None, seq=None):
    rows = x.shape[0]
    nb, mrows = mod.shape[0], mod.shape[1]
    pair_major = batch is not None
    tm = ROW_TILE if pair_major else rows
    tpb = rows // nb // tm
    if pair_major:
        q_spec = pl.BlockSpec((1, HEAD_PAIRS, tm, LANES), lambda i: (i // tpb, 0, i % tpb, 0))
        q_shape = jax.ShapeDtypeStruct((batch, HEAD_PAIRS, seq, LANES), BF16)
    else:
        q_spec = pl.BlockSpec((tm, FA_WIDTH), lambda i: (i, 0))
        q_shape = jax.ShapeDtypeStruct((rows, FA_WIDTH), BF16)
    return pl.pallas_call(
        functools.partial(_q_proj_kernel, pair_major=pair_major),
        grid=(rows // tm,),
        in_specs=[pl.BlockSpec((tm, D_MODEL), lambda i: (i, 0)),
                  pl.BlockSpec((1, mrows, 3 * D_MODEL), lambda i: (i // tpb, 0, 0)),
                  _const_spec((D_MODEL, 2 * FA_WIDTH)), _const_spec((1, 2 * FA_WIDTH))],
        out_specs=[q_spec, pl.BlockSpec((tm, FA_WIDTH), lambda i: (i, 0))],
        out_shape=[q_shape, jax.ShapeDtypeStruct((rows, FA_WIDTH), BF16)],
        compiler_params=_params(1),
        name="q_proj_prompt" if pair_major else "q_proj_sample",
    )(x, mod, w_in.astype(BF16), b_in[None])


def _out_proj_kernel(o_ref, gz_ref, x_ref, mod_ref, w_ref, b_ref, g_ref, bb_ref, *rest, next_q):
    if next_q:
        modn_ref, wq_ref, bq_ref, y_ref, q_ref, gzn_ref = rest
    else:
        (y_ref,) = rest
    t = (o_ref[...].astype(F32) * gz_ref[...].astype(F32)).astype(BF16)
    y = _dot(t, w_ref[...]) + b_ref[...]
    gate = mod_ref[0][:, 2 * D_MODEL:]
    r = DEEPNORM_ALPHA * x_ref[...] + (1.0 + gate) * y
    x_new = _layer_norm(r, g_ref[...], bb_ref[...])
    y_ref[...] = x_new
    if next_q:
        _q_proj_store(x_new, modn_ref[0], wq_ref, bq_ref, q_ref, gzn_ref, True)


def _out_proj(o, gz, x, mod, w_out, b_out, ln_g, ln_b, *, tm, next_q=None):
    rows = x.shape[0]
    nb, mrows = mod.shape[0], mod.shape[1]
    tpb = rows // nb // tm
    row = lambda i: (i, 0)
    mod_spec = pl.BlockSpec((1, mrows, 3 * D_MODEL), lambda i: (i // tpb, 0, 0))
    in_specs = [pl.BlockSpec((tm, FA_WIDTH), row), pl.BlockSpec((tm, FA_WIDTH), row),
                pl.BlockSpec((tm, D_MODEL), row), mod_spec,
                _const_spec((FA_WIDTH, D_MODEL)), _const_spec((1, D_MODEL)),
                _const_spec((1, D_MODEL)), _const_spec((1, D_MODEL))]
    args = [o, gz, x, mod, w_out.astype(BF16), b_out[None], ln_g[None], ln_b[None]]
    out_specs = [pl.BlockSpec((tm, D_MODEL), row)]
    out_shape = [jax.ShapeDtypeStruct((rows, D_MODEL), F32)]
    if next_q is not None:
        mod_n, w_in, b_in, batch, seq = next_q
        in_specs += [mod_spec, _const_spec((D_MODEL, 2 * FA_WIDTH)), _const_spec((1, 2 * FA_WIDTH))]
        args += [mod_n, w_in.astype(BF16), b_in[None]]
        out_specs += [pl.BlockSpec((1, HEAD_PAIRS, tm, LANES), lambda i: (i // tpb, 0, i % tpb, 0)),
                      pl.BlockSpec((tm, FA_WIDTH), row)]
        out_shape += [jax.ShapeDtypeStruct((batch, HEAD_PAIRS, seq, LANES), BF16),
                      jax.ShapeDtypeStruct((rows, FA_WIDTH), BF16)]
    res = pl.pallas_call(
        functools.partial(_out_proj_kernel, next_q=next_q is not None),
        grid=(rows // tm,),
        in_specs=in_specs,
        out_specs=out_specs,
        out_shape=out_shape,
        compiler_params=_params(1),
        name="out_proj",
    )(*args)
    return res if next_q is not None else res[0]


def _flash_q_tile(i, q2, k_ref, v_ref, c_ref, scr, tile):
    s_scr = (scr[0:2], scr[2:4])
    m_scr = (scr[4:6], scr[6:8])
    p_scr, alpha_scr, acc_scr = scr[8:10], scr[10:12], scr[12:14]
    lane = lax.broadcasted_iota(jnp.int32, (1, LANES), 1)
    first = lane < FA_HEAD_DIM
    sub8 = lax.broadcasted_iota(jnp.int32, (8, tile), 0)

    half = tile // 2

    def logits(t, q_heads, buf, diagonal=False):
        c2 = c_ref[0, 0, t]
        hi = c2.astype(BF16).astype(F32)
        mid = (c2 - hi).astype(BF16).astype(F32)
        lo = c2 - hi - mid
        pieces = (hi[0:1], mid[0:1], lo[0:1], hi[1:2], mid[1:2], lo[1:2])
        c_rows = jnp.zeros((8, tile), F32)
        for r, piece in enumerate(pieces):
            c_rows = jnp.where(sub8 == r, piece, c_rows)
        c_block = jnp.concatenate([c_rows, jnp.zeros((LANES - 8, tile), F32)], axis=0).astype(BF16)
        k_aug = jnp.concatenate([k_ref[0, 0, t], c_block], axis=0)
        for a in range(2):
            if diagonal:
                s_scr[buf][a][:half, :half] = _dot(q_heads[a][:half], k_aug[:, :half])
                s_scr[buf][a][half:, :] = _dot(q_heads[a][half:], k_aug)
            else:
                s_scr[buf][a][...] = _dot(q_heads[a], k_aug)

    def softmax_blocks(a, buf, masked):
        for r0 in range(0, tile, FLASH_ROWS):
            rows = slice(r0, r0 + FLASH_ROWS)
            ncols = min(tile, -(-(r0 + FLASH_ROWS) // LANES) * LANES) if masked else tile
            s = s_scr[buf][a][rows, :ncols]
            if masked:
                r_i = r0 + lax.broadcasted_iota(jnp.int32, (FLASH_ROWS, ncols), 0)
                c_i = lax.broadcasted_iota(jnp.int32, (FLASH_ROWS, ncols), 1)
                s = jnp.where(c_i <= r_i, s, NEG_BIG)
            m_old = m_scr[buf][a][rows, :]
            m_new = jnp.maximum(m_old, jnp.max(s, axis=-1, keepdims=True))
            m_scr[1 - buf][a][rows, :] = m_new
            alpha_scr[a][rows, :] = jnp.exp2(m_old - m_new)
            m_wide = jnp.concatenate([m_new] * (ncols // LANES), axis=1)
            p_scr[a][rows, :ncols] = jnp.exp2((s - m_wide).astype(BF16))
            fill_to = half if r0 < half else tile
            if masked and ncols < fill_to:
                p_scr[a][rows, ncols:fill_to] = jnp.zeros((FLASH_ROWS, fill_to - ncols), BF16)

    def consume(t, buf, masked):
        v2 = v_ref[0, 0, pl.ds(pl.multiple_of(t * tile, tile), tile), :]
        one = jnp.ones_like(v2)
        v_heads = (jnp.where(first, v2, one), jnp.where(first, one, v2))
        for a in range(2):
            softmax_blocks(a, buf, masked)
            if masked:
                pv = jnp.concatenate([_dot(p_scr[a][:half, :half], v_heads[a][:half]),
                                      _dot(p_scr[a][half:, :], v_heads[a])], axis=0)
            else:
                pv = _dot(p_scr[a][...], v_heads[a])
            acc_scr[a][...] = alpha_scr[a][...] * acc_scr[a][...] + pv

    col_q = lax.broadcasted_iota(jnp.int32, (tile, LANES), 1)
    zero = jnp.zeros_like(q2)
    q_heads = (jnp.concatenate([jnp.where(first, q2, zero),
                                jnp.where(col_q < 3, -1.0, 0.0).astype(BF16)], axis=1),
               jnp.concatenate([jnp.where(first, zero, q2),
                                jnp.where((col_q >= 3) & (col_q < 6), -1.0, 0.0).astype(BF16)], axis=1))
    for a in range(2):
        m_scr[0][a][...] = jnp.full((tile, LANES), NEG_BIG, F32)
        acc_scr[a][...] = jnp.zeros((tile, LANES), F32)

    logits(0, q_heads, 0)

    def tile_pair(u, _):
        logits(2 * u + 1, q_heads, 1)
        consume(2 * u, 0, False)
        logits(2 * u + 2, q_heads, 0)
        consume(2 * u + 1, 1, False)
        return 0

    lax.fori_loop(0, i // 2, tile_pair, 0)

    @pl.when(i % 2 == 1)
    def _():
        logits(i, q_heads, 1, diagonal=True)
        consume(i - 1, 0, False)
        consume(i, 1, True)

    @pl.when(i % 2 == 0)
    def _():
        consume(i, 0, True)

    acc_a, acc_b = acc_scr[0][...], acc_scr[1][...]
    o_a = acc_a / pltpu.roll(acc_a, FA_HEAD_DIM, axis=1)
    o_b = acc_b / pltpu.roll(acc_b, FA_HEAD_DIM, axis=1)
    return jnp.where(first, o_a, o_b)


def _decode_group(g, n_groups, q_ref, kn_ref, vn_ref, lfn_ref, k_refs, v_refs, lf_refs, o_ref,
                  m_scr, l_scr, acc_scr, off_scr):
    pg = len(k_refs)

    @pl.when(g == 0)
    def _():
        m_scr[...] = jnp.full(m_scr.shape, NEG_BIG, F32)
        l_scr[...] = jnp.zeros(l_scr.shape, F32)
        acc_scr[...] = jnp.zeros(acc_scr.shape, F32)
        off_scr[...] = jnp.zeros(off_scr.shape, F32)

    head = lax.broadcasted_iota(jnp.int32, (FA_HEADS, FA_WIDTH), 0)
    col = lax.broadcasted_iota(jnp.int32, (FA_HEADS, FA_WIDTH), 1)
    own = (col >= head * FA_HEAD_DIM) & (col < (head + 1) * FA_HEAD_DIM)
    q_rows = jnp.where(own, jnp.broadcast_to(q_ref[0].astype(F32), (FA_HEADS, FA_WIDTH)), 0.0)
    q_bf = q_rows.astype(BF16)

    r_i = lax.broadcasted_iota(jnp.int32, (PAGE_SIZE, PAGE_SIZE), 0)
    c_i = lax.broadcasted_iota(jnp.int32, (PAGE_SIZE, PAGE_SIZE), 1)
    upper = jnp.where(r_i <= c_i, 1.0, 0.0).astype(BF16)
    lf_all = jnp.concatenate([lf_refs[p][0] for p in range(pg)], axis=0)
    cum_all = _split3_dot(lf_all, upper)
    off = off_scr[:, 0:1]
    s_parts = []
    for p in range(pg):
        cum_p = cum_all[p * FA_HEADS:(p + 1) * FA_HEADS]
        s_parts.append(_dot(q_bf, k_refs[p][0].astype(BF16)) - (off + cum_p))
        off = off + cum_p[:, PAGE_SIZE - 1:PAGE_SIZE]
    off_scr[...] = jnp.broadcast_to(off, off_scr.shape)
    s = jnp.concatenate(s_parts, axis=1)
    m_old = m_scr[:, 0:1]
    m_new = jnp.maximum(m_old, jnp.max(s, axis=1, keepdims=True))
    alpha = jnp.exp(m_old - m_new)
    prob = jnp.exp(s - m_new)
    l_new = alpha * l_scr[:, 0:1] + jnp.sum(prob, axis=1, keepdims=True)
    pb = prob.astype(BF16)
    acc = alpha * acc_scr[...]
    for p in range(pg):
        acc = acc + _dot_nt(pb[:, p * PAGE_SIZE:(p + 1) * PAGE_SIZE], v_refs[p][0].astype(BF16))
    m_scr[...] = jnp.broadcast_to(m_new, m_scr.shape)
    l_scr[...] = jnp.broadcast_to(l_new, l_scr.shape)
    acc_scr[...] = acc

    @pl.when(g == n_groups - 1)
    def _():
        c_query = off + lfn_ref[0]
        s_new = jnp.sum(q_rows * kn_ref[0], axis=1, keepdims=True)
        m_past = m_new + c_query
        m_fin = jnp.maximum(m_past, s_new)
        w_past = jnp.exp(m_past - m_fin)
        w_new = jnp.exp(s_new - m_fin)
        o_full = (acc * w_past + w_new * vn_ref[0]) / (l_new * w_past + w_new)
        o_ref[0] = jnp.sum(jnp.where(own, o_full, 0.0), axis=0, keepdims=True)


def _attention_kernel(pt_ref, q_ref, k_ref, v_ref, c_ref, qs_ref, kn_ref, vn_ref, lfn_ref, *refs,
                      tile, pg, n_groups):
    del pt_ref
    k_refs, v_refs, lf_refs = refs[:pg], refs[pg:2 * pg], refs[2 * pg:3 * pg]
    o_ref, os_ref = refs[3 * pg:3 * pg + 2]
    scr = refs[3 * pg + 2:]
    i2 = pl.program_id(2)

    def q_tile(h, _):
        rows = pl.ds(pl.multiple_of(h * tile, tile), tile)
        o = _flash_q_tile(Q_TILES_PER_STEP * i2 + h, q_ref[0, 0, rows, :], k_ref, v_ref, c_ref, scr[:14], tile)
        o_ref[0, rows, :] = o.astype(o_ref.dtype)
        return 0

    lax.fori_loop(0, Q_TILES_PER_STEP, q_tile, 0)
    step = (pl.program_id(0) * pl.num_programs(1) + pl.program_id(1)) * pl.num_programs(2) + i2
    _decode_group(step % n_groups, n_groups, qs_ref, kn_ref, vn_ref, lfn_ref, k_refs, v_refs, lf_refs, os_ref,
                  *scr[14:])


def _attention_layer(q_pm, k_pm, v_pm, c_t, batch, seq,
                     q_s, k_new, v_new, lf_new, cache_kt, cache_vt, cache_lft, page_table):
    tile = ATT_TILE
    nt = seq // tile
    qt = Q_TILES_PER_STEP
    nq = nt // qt
    n, n_pages = page_table.shape
    steps = batch * HEAD_PAIRS * nq
    n_groups = steps // n
    pg = n_pages // n_groups
    assert nq * qt == nt and n_groups * n == steps and pg * n_groups == n_pages

    def sample_of(b, j, i):
        return ((b * HEAD_PAIRS + j) * nq + i) // n_groups

    def page_map(p):
        def index(b, j, i, pt):
            step = (b * HEAD_PAIRS + j) * nq + i
            return (pt[step // n_groups, (step % n_groups) * pg + p], 0, 0)
        return index

    row = lambda b, j, i, pt: (sample_of(b, j, i), 0, 0)
    in_specs = [pl.BlockSpec((1, 1, qt * tile, LANES), lambda b, j, i, pt: (b, j, i, 0)),
                pl.BlockSpec((1, 1, nt, LANES, tile), lambda b, j, i, pt: (b, j, 0, 0, 0)),
                pl.BlockSpec((1, 1, seq, LANES), lambda b, j, i, pt: (b, j, 0, 0)),
                pl.BlockSpec((1, 1, nt, 2, tile), lambda b, j, i, pt: (b, j, 0, 0, 0)),
                pl.BlockSpec((1, 1, FA_WIDTH), row), pl.BlockSpec((1, 1, FA_WIDTH), row),
                pl.BlockSpec((1, 1, FA_WIDTH), row), pl.BlockSpec((1, FA_HEADS, 1), row)]
    in_specs += [pl.BlockSpec((1, FA_WIDTH, PAGE_SIZE), page_map(p)) for p in range(pg)]
    in_specs += [pl.BlockSpec((1, FA_WIDTH, PAGE_SIZE), page_map(p)) for p in range(pg)]
    in_specs += [pl.BlockSpec((1, FA_HEADS, PAGE_SIZE), page_map(p)) for p in range(pg)]
    o_p, o_s = pl.pallas_call(
        functools.partial(_attention_kernel, tile=tile, pg=pg, n_groups=n_groups),
        grid_spec=pltpu.PrefetchScalarGridSpec(
            num_scalar_prefetch=1, grid=(batch, HEAD_PAIRS, nq), in_specs=in_specs,
            out_specs=[pl.BlockSpec((1, qt * tile, LANES), lambda b, j, i, pt: (b, i, j)),
                       pl.BlockSpec((1, 1, FA_WIDTH), row)],
            scratch_shapes=([pltpu.VMEM((tile, tile), F32)] * 4
                            + [pltpu.VMEM((tile, LANES), F32)] * 4
                            + [pltpu.VMEM((tile, tile), BF16)] * 2
                            + [pltpu.VMEM((tile, LANES), F32)] * 2
                            + [pltpu.VMEM((tile, LANES), F32)] * 2
                            + [pltpu.VMEM((FA_HEADS, LANES), F32)] * 2
                            + [pltpu.VMEM((FA_HEADS, FA_WIDTH), F32)]
                            + [pltpu.VMEM((FA_HEADS, LANES), F32)])),
        out_shape=[jax.ShapeDtypeStruct((batch, seq, FA_WIDTH), BF16),
                   jax.ShapeDtypeStruct((n, 1, FA_WIDTH), F32)],
        compiler_params=_params(3),
        name="attention_layer",
    )(page_table, q_pm, k_pm, v_pm, c_t,
      q_s.reshape(n, 1, FA_WIDTH), k_new.reshape(n, 1, FA_WIDTH), v_new.reshape(n, 1, FA_WIDTH),
      lf_new.reshape(n, FA_HEADS, 1), *([cache_kt] * pg), *([cache_vt] * pg), *([cache_lft] * pg))
    return o_p, o_s.reshape(n, FA_WIDTH)


def kernel(x_prompt, x_sample, c_prompt, c_sample, cache_k, cache_v, cache_logf, page_table, w_ada, b_ada, ln_g, ln_b, cm_w_in, cm_b_in, cm_ln_v_g, cm_ln_v_b, cm_w_s, cm_b_s, cm_w_out, cm_b_out, kv_w, kv_b, fa_w_in, fa_b_in, fa_w_out, fa_b_out):
    batch, seq, _ = x_prompt.shape
    n_dec = x_sample.shape[0]

    n_c = batch + n_dec
    c_all = jnp.pad(jnp.concatenate([c_prompt, c_sample], axis=0), ((0, (-n_c) % 8), (0, 0)))
    mod = _ada_modulation(c_all, w_ada, b_ada)
    mod_p = mod[:, :batch].reshape(DEPTH, batch, 1, 3 * D_MODEL)
    mod_s = mod[:, batch:n_c].reshape(DEPTH, 1, n_dec, 3 * D_MODEL)

    xp = x_prompt.reshape(batch * seq, D_MODEL)
    xs = x_sample.reshape(n_dec, D_MODEL)
    chunk_rows = []
    for layer in range(N_A_LAYERS):
        args = (cm_w_in[layer], cm_b_in[layer], cm_ln_v_g[layer], cm_ln_v_b[layer], cm_w_s[layer], cm_b_s[layer],
                cm_w_out[layer], cm_b_out[layer], ln_g[layer], ln_b[layer])
        (xp,) = _chunk_mlp_layer(xp, mod_p[layer], *args, single_row_chunks=False)
        xs, vn_s = _chunk_mlp_layer(xs, mod_s[layer], *args, single_row_chunks=True)
        chunk_rows.append(vn_s)

    kt_p, vt_p, lft_p, kb_p, vb_p, ct_p, q_pm, gz_p = _kv_prompt(
        xp, kv_w, kv_b, batch, seq, mod_p[N_A_LAYERS], fa_w_in[0], fa_b_in[0])
    kt_s, vt_s, lft_s, k_s, v_s, lf_s = _kv_sample(xs, kv_w, kv_b)

    n_phys = cache_k.shape[0]
    cache_kt = jnp.transpose(cache_k, (0, 2, 3, 1)).reshape(n_phys, FA_WIDTH, PAGE_SIZE)
    cache_vt = jnp.transpose(cache_v, (0, 2, 3, 1)).reshape(n_phys, FA_WIDTH, PAGE_SIZE)
    cache_lft = jnp.swapaxes(cache_logf, 1, 2)

    for j in range(DEPTH - N_A_LAYERS):
        layer = N_A_LAYERS + j
        q_s, gz_s = _q_proj(xs, mod_s[layer], fa_w_in[j], fa_b_in[j])
        o_p, o_s = _attention_layer(q_pm, kb_p, vb_p, ct_p, batch, seq,
                                    q_s, k_s, v_s, lf_s, cache_kt, cache_vt, cache_lft, page_table)
        next_q = ((mod_p[layer + 1], fa_w_in[j + 1], fa_b_in[j + 1], batch, seq) if layer + 1 < DEPTH else None)
        res = _out_proj(o_p.reshape(batch * seq, FA_WIDTH), gz_p, xp, mod_p[layer], fa_w_out[j], fa_b_out[j],
                        ln_g[layer], ln_b[layer], tm=ROW_TILE, next_q=next_q)
        xp, q_pm, gz_p = res if next_q is not None else (res, None, None)
        xs = _out_proj(o_s, gz_s, xs, mod_s[layer], fa_w_out[j], fa_b_out[j], ln_g[layer], ln_b[layer], tm=n_dec)

    y_prompt = xp.reshape(batch, seq, D_MODEL)
    y_sample = xs.reshape(n_dec, 1, D_MODEL)
    k_prompt = jnp.transpose(kt_p.reshape(batch, FA_HEADS, FA_HEAD_DIM, seq), (0, 3, 1, 2))
    v_prompt = jnp.transpose(vt_p.reshape(batch, FA_HEADS, FA_HEAD_DIM, seq), (0, 3, 1, 2))
    logf_prompt = jnp.transpose(lft_p, (0, 2, 1))
    k_sample = jnp.transpose(kt_s.reshape(1, FA_HEADS, FA_HEAD_DIM, n_dec), (3, 0, 1, 2))
    v_sample = jnp.transpose(vt_s.reshape(1, FA_HEADS, FA_HEAD_DIM, n_dec), (3, 0, 1, 2))
    logf_sample = jnp.transpose(lft_s, (2, 0, 1))
    chunk_v_sample = jnp.stack(chunk_rows).reshape(N_A_LAYERS, n_dec, 1, CM_WIDTH)
    return (y_prompt, y_sample, k_prompt, v_prompt, logf_prompt, k_sample, v_sample, logf_sample, chunk_v_sample)
```
